```python
import jax
import jax.numpy as jnp
from jax import lax
import numpy as np


D_MODEL = 1024
BATCH = 8
SEQ = 16384
DEPTH = 4

N_HEADS = 8
HEAD_DIM = 128
GDN_WIDTH = N_HEADS * HEAD_DIM
SC_WIDTH = D_MODEL
CONV_W = 3
CHUNK = 64
D_FF = 4 * D_MODEL
LN_EPS = 1e-5
RMS_EPS = 1e-6
L2_EPS = 1e-6
DEEPNORM_ALPHA = (2 * DEPTH) ** 0.25
DEEPNORM_BETA = (8 * DEPTH) ** -0.25

IN_SPLITS = (3 * GDN_WIDTH,
             GDN_WIDTH,
             2 * N_HEADS,
             2 * N_HEADS,
             SC_WIDTH,
             SC_WIDTH,
             SC_WIDTH,
             D_MODEL,
             D_MODEL)
IN_COLS = sum(IN_SPLITS)

kernel_name = 'hybrid_gdn_shortconv_deepnorm_encoder'


def _layer_norm(x, g, b):
    xf = x.astype(jnp.float32)
    mu = jnp.mean(xf, axis=-1, keepdims=True)
    var = jnp.mean(jnp.square(xf - mu), axis=-1, keepdims=True)
    y = (xf - mu) * lax.rsqrt(var + LN_EPS) * g.astype(jnp.float32) + b.astype(jnp.float32)
    return y.astype(x.dtype)


def _l2norm(x):
    return x * lax.rsqrt(jnp.sum(jnp.square(x), axis=-1, keepdims=True) + L2_EPS)


def _centred_dwconv(x, w):
    c = x.shape[-1]
    return lax.conv_general_dilated(
        x, w[:, None, :].astype(x.dtype), window_strides=(1,),
        padding=[(CONV_W // 2, CONV_W // 2)],
        dimension_numbers=('NWC', 'WIO', 'NWC'), feature_group_count=c)


def _split_in(proj):
    idx = []
    acc = 0
    for s in IN_SPLITS[:-1]:
        acc += s
        idx.append(acc)
    return jnp.split(proj, idx, axis=-1)


def _gated_delta_chunked(q, k, v, g, beta):
    b, t, h, dk = q.shape
    dv = v.shape[-1]
    n = t // CHUNK

    def blk(a):
        return a.reshape(b, n, CHUNK, h, -1).transpose(1, 0, 3, 2, 4)

    q, k, v = blk(q), blk(k), blk(v)
    g = g.reshape(b, n, CHUNK, h).transpose(1, 0, 3, 2)
    beta = beta.reshape(b, n, CHUNK, h).transpose(1, 0, 3, 2)
    gc = jnp.cumsum(g, axis=-1)
    incl = jnp.tril(jnp.ones((CHUNK, CHUNK), dtype=bool))
    strict = jnp.tril(jnp.ones((CHUNK, CHUNK), dtype=bool), -1)
    diff = gc[..., :, None] - gc[..., None, :]
    decay = jnp.where(incl, jnp.exp(jnp.where(incl, diff, 0.0)), 0.0)
    kb = k * beta[..., None]
    lmat = jnp.where(strict, jnp.einsum('nbhcd,nbhsd->nbhcs', kb, k) * decay, 0.0)
    eye = jnp.eye(CHUNK, dtype=lmat.dtype)
    tmat = lax.linalg.triangular_solve(lmat + eye, jnp.broadcast_to(eye, lmat.shape),
                                       left_side=True, lower=True, unit_diagonal=True)
    u = jnp.einsum('nbhcs,nbhsd->nbhcd', tmat, v * beta[..., None])
    w = jnp.einsum('nbhcs,nbhsd->nbhcd', tmat, kb * jnp.exp(gc)[..., None])
    attn = jnp.einsum('nbhcd,nbhsd->nbhcs', q, k) * decay
    q_dec = q * jnp.exp(gc)[..., None]
    k_dec = k * jnp.exp(gc[..., -1:] - gc)[..., None]
    g_last = jnp.exp(gc[..., -1])

    def step(state, xs):
        u_i, w_i, a_i, qd_i, kd_i, gl_i = xs
        v_new = u_i - jnp.einsum('bhcd,bhde->bhce', w_i, state)
        o_i = jnp.einsum('bhcd,bhde->bhce', qd_i, state) + jnp.einsum('bhcs,bhse->bhce', a_i, v_new)
        state = state * gl_i[..., None, None] + jnp.einsum('bhcd,bhce->bhde', kd_i, v_new)
        return state, o_i

    s0 = jnp.zeros((b, h, dk, dv), dtype=jnp.float32)
    _, o = lax.scan(step, s0, (u, w, attn, q_dec, k_dec, g_last))
    return o.transpose(1, 0, 3, 2, 4).reshape(b, t, h, dv)


def _gdn_branch(qkv, z, a, bb, conv_w, a_log, dt_bias, norm_w, w_o):
    bsz, t, _ = qkv.shape
    out_dtype = qkv.dtype
    qkv = jax.nn.silu(_centred_dwconv(qkv, conv_w)).astype(jnp.float32)
    q, k, v = jnp.split(qkv, 3, axis=-1)
    q = _l2norm(q.reshape(bsz, t, N_HEADS, HEAD_DIM)) * (HEAD_DIM ** -0.5)
    k = _l2norm(k.reshape(bsz, t, N_HEADS, HEAD_DIM))
    v = v.reshape(bsz, t, N_HEADS, HEAD_DIM)
    a = a.astype(jnp.float32).reshape(bsz, t, 2, N_HEADS)
    bb = bb.astype(jnp.float32).reshape(bsz, t, 2, N_HEADS)
    g = -jnp.exp(a_log.astype(jnp.float32)) * jax.nn.softplus(a + dt_bias.astype(jnp.float32))
    beta = jax.nn.sigmoid(bb)
    o_fwd = _gated_delta_chunked(q, k, v, g[:, :, 0], beta[:, :, 0])
    fl = lambda arr: jnp.flip(arr, axis=1)
    o_bwd = fl(_gated_delta_chunked(fl(q), fl(k), fl(v), fl(g[:, :, 1]), fl(beta[:, :, 1])))
    o = o_fwd + o_bwd
    zf = z.astype(jnp.float32).reshape(bsz, t, N_HEADS, HEAD_DIM)
    o = (o * lax.rsqrt(jnp.mean(jnp.square(o), axis=-1, keepdims=True) + RMS_EPS)
         * norm_w.astype(jnp.float32) * jax.nn.silu(zf))
    return o.reshape(bsz, t, GDN_WIDTH).astype(out_dtype) @ w_o


def _shortconv_branch(sc_b, sc_c, sc_x, conv_w, w_o):
    return (sc_b * _centred_dwconv(sc_c * sc_x, conv_w)) @ w_o


def _fwd_setup_inputs(seed: int = 0) -> dict:
    key = jax.random.key(seed)
    ks = jax.random.split(key, 20)
    f32 = jnp.float32
    nrm = lambda k, shape, s: jax.random.normal(k, shape, f32) * s
    x = jax.random.normal(ks[0], (BATCH, SEQ, D_MODEL), f32)
    w_in = nrm(ks[1], (DEPTH, D_MODEL, IN_COLS), D_MODEL ** -0.5)
    conv_qkv = nrm(ks[2], (DEPTH, CONV_W, 3 * GDN_WIDTH), CONV_W ** -0.5)
    a_log = jnp.log(jax.random.uniform(ks[3], (DEPTH, 2, N_HEADS), f32, 1.0, 16.0))
    dt = jnp.exp(jax.random.uniform(ks[4], (DEPTH, 2, N_HEADS), f32,
                                    math_log(1e-3), math_log(1e-1)))
    dt_bias = dt + jnp.log(-jnp.expm1(-dt))
    gdn_norm_w = 1.0 + nrm(ks[5], (DEPTH, HEAD_DIM), 0.02)
    w_o_gdn = nrm(ks[6], (DEPTH, GDN_WIDTH, D_MODEL), GDN_WIDTH ** -0.5)
    conv_sc = nrm(ks[7], (DEPTH, CONV_W, SC_WIDTH), CONV_W ** -0.5)
    w_o_sc = nrm(ks[8], (DEPTH, SC_WIDTH, D_MODEL), SC_WIDTH ** -0.5)
    w_out = nrm(ks[9], (DEPTH, D_MODEL, D_MODEL), DEEPNORM_BETA * D_MODEL ** -0.5)
    ln1_g = 1.0 + nrm(ks[10], (DEPTH, D_MODEL), 0.02)
    ln1_b = nrm(ks[11], (DEPTH, D_MODEL), 0.02)
    w_up = nrm(ks[12], (DEPTH, D_MODEL, D_FF), D_MODEL ** -0.5)
    b_up = nrm(ks[13], (DEPTH, D_FF), 0.02)
    w_down = nrm(ks[14], (DEPTH, D_FF, D_MODEL), DEEPNORM_BETA * D_FF ** -0.5)
    b_down = nrm(ks[15], (DEPTH, D_MODEL), 0.02)
    ln2_g = 1.0 + nrm(ks[16], (DEPTH, D_MODEL), 0.02)
    ln2_b = nrm(ks[17], (DEPTH, D_MODEL), 0.02)
    return {'x': x, 'w_in': w_in, 'conv_qkv': conv_qkv, 'a_log': a_log, 'dt_bias': dt_bias,
            'gdn_norm_w': gdn_norm_w, 'w_o_gdn': w_o_gdn, 'conv_sc': conv_sc, 'w_o_sc': w_o_sc,
            'w_out': w_out, 'ln1_g': ln1_g, 'ln1_b': ln1_b, 'w_up': w_up, 'b_up': b_up,
            'w_down': w_down, 'b_down': b_down, 'ln2_g': ln2_g, 'ln2_b': ln2_b}


def math_log(v):
    return float(np.log(v))


def _fwd_reference(x, w_in, conv_qkv, a_log, dt_bias, gdn_norm_w, w_o_gdn, conv_sc, w_o_sc,
              w_out, ln1_g, ln1_b, w_up, b_up, w_down, b_down, ln2_g, ln2_b):
    for l in range(DEPTH):
        proj = x @ w_in[l]
        qkv, z, a, bb, sc_b, sc_c, sc_x, gate_a, gate_b = _split_in(proj)
        y_a = _gdn_branch(qkv, z, a, bb, conv_qkv[l], a_log[l], dt_bias[l],
                          gdn_norm_w[l], w_o_gdn[l])
        y_b = _shortconv_branch(sc_b, sc_c, sc_x, conv_sc[l], w_o_sc[l])
        mixed = jax.nn.sigmoid(gate_a) * y_a + jax.nn.sigmoid(gate_b) * y_b
        x = _layer_norm(DEEPNORM_ALPHA * x + mixed @ w_out[l], ln1_g[l], ln1_b[l])
        h = jnp.square(jax.nn.relu(x @ w_up[l] + b_up[l]))
        x = _layer_norm(DEEPNORM_ALPHA * x + h @ w_down[l] + b_down[l], ln2_g[l], ln2_b[l])
    return x


import jax as _jax
import jax.numpy as _jnp

TWIN_FORMAT = 'train_step'
FWD_PARAMS = ['x', 'w_in', 'conv_qkv', 'a_log', 'dt_bias', 'gdn_norm_w', 'w_o_gdn', 'conv_sc', 'w_o_sc', 'w_out', 'ln1_g', 'ln1_b', 'w_up', 'b_up', 'w_down', 'b_down', 'ln2_g', 'ln2_b']
TWIN_WEIGHTS = ['w_in', 'conv_qkv', 'a_log', 'dt_bias', 'gdn_norm_w', 'w_o_gdn', 'conv_sc', 'w_o_sc', 'w_out', 'ln1_g', 'ln1_b', 'w_up', 'b_up', 'w_down', 'b_down', 'ln2_g', 'ln2_b']
TWIN_DIFF_INPUT = 'x'
TWIN_INPUTS = ['x', 'w_in', 'conv_qkv', 'a_log', 'dt_bias', 'gdn_norm_w', 'w_o_gdn', 'conv_sc', 'w_o_sc', 'w_out', 'ln1_g', 'ln1_b', 'w_up', 'b_up', 'w_down', 'b_down', 'ln2_g', 'ln2_b', 'loss_target', 'm_w_in', 'm_conv_qkv', 'm_a_log', 'm_dt_bias', 'm_gdn_norm_w', 'm_w_o_gdn', 'm_conv_sc', 'm_w_o_sc', 'm_w_out', 'm_ln1_g', 'm_ln1_b', 'm_w_up', 'm_b_up', 'm_w_down', 'm_b_down', 'm_ln2_g', 'm_ln2_b', 'v_w_in', 'v_conv_qkv', 'v_a_log', 'v_dt_bias', 'v_gdn_norm_w', 'v_w_o_gdn', 'v_conv_sc', 'v_w_o_sc', 'v_w_out', 'v_ln1_g', 'v_ln1_b', 'v_w_up', 'v_b_up', 'v_w_down', 'v_b_down', 'v_ln2_g', 'v_ln2_b']
TWIN_OUTPUTS = ['loss', 'grad_x', 'grad_w_in', 'grad_conv_qkv', 'grad_a_log', 'grad_dt_bias', 'grad_gdn_norm_w', 'grad_w_o_gdn', 'grad_conv_sc', 'grad_w_o_sc', 'grad_w_out', 'grad_ln1_g', 'grad_ln1_b', 'grad_w_up', 'grad_b_up', 'grad_w_down', 'grad_b_down', 'grad_ln2_g', 'grad_ln2_b', 'delta_w_in', 'delta_conv_qkv', 'delta_a_log', 'delta_dt_bias', 'delta_gdn_norm_w', 'delta_w_o_gdn', 'delta_conv_sc', 'delta_w_o_sc', 'delta_w_out', 'delta_ln1_g', 'delta_ln1_b', 'delta_w_up', 'delta_b_up', 'delta_w_down', 'delta_b_down', 'delta_ln2_g', 'delta_ln2_b', 'new_m_w_in', 'new_m_conv_qkv', 'new_m_a_log', 'new_m_dt_bias', 'new_m_gdn_norm_w', 'new_m_w_o_gdn', 'new_m_conv_sc', 'new_m_w_o_sc', 'new_m_w_out', 'new_m_ln1_g', 'new_m_ln1_b', 'new_m_w_up', 'new_m_b_up', 'new_m_w_down', 'new_m_b_down', 'new_m_ln2_g', 'new_m_ln2_b', 'new_v_w_in', 'new_v_conv_qkv', 'new_v_a_log', 'new_v_dt_bias', 'new_v_gdn_norm_w', 'new_v_w_o_gdn', 'new_v_conv_sc', 'new_v_w_o_sc', 'new_v_w_out', 'new_v_ln1_g', 'new_v_ln1_b', 'new_v_w_up', 'new_v_b_up', 'new_v_w_down', 'new_v_b_down', 'new_v_ln2_g', 'new_v_ln2_b']
TWIN_LEAF_KINDS = {'loss': 'loss', 'grad_x': 'grad_x', 'grad_w_in': 'grad_w', 'grad_conv_qkv': 'grad_w', 'grad_a_log': 'grad_w', 'grad_dt_bias': 'grad_w', 'grad_gdn_norm_w': 'grad_w', 'grad_w_o_gdn': 'grad_w', 'grad_conv_sc': 'grad_w', 'grad_w_o_sc': 'grad_w', 'grad_w_out': 'grad_w', 'grad_ln1_g': 'grad_w', 'grad_ln1_b': 'grad_w', 'grad_w_up': 'grad_w', 'grad_b_up': 'grad_w', 'grad_w_down': 'grad_w', 'grad_b_down': 'grad_w', 'grad_ln2_g': 'grad_w', 'grad_ln2_b': 'grad_w', 'delta_w_in': 'delta_w', 'delta_conv_qkv': 'delta_w', 'delta_a_log': 'delta_w', 'delta_dt_bias': 'delta_w', 'delta_gdn_norm_w': 'delta_w', 'delta_w_o_gdn': 'delta_w', 'delta_conv_sc': 'delta_w', 'delta_w_o_sc': 'delta_w', 'delta_w_out': 'delta_w', 'delta_ln1_g': 'delta_w', 'delta_ln1_b': 'delta_w', 'delta_w_up': 'delta_w', 'delta_b_up': 'delta_w', 'delta_w_down': 'delta_w', 'delta_b_down': 'delta_w', 'delta_ln2_g': 'delta_w', 'delta_ln2_b': 'delta_w', 'new_m_w_in': 'new_m', 'new_m_conv_qkv': 'new_m', 'new_m_a_log': 'new_m', 'new_m_dt_bias': 'new_m', 'new_m_gdn_norm_w': 'new_m', 'new_m_w_o_gdn': 'new_m', 'new_m_conv_sc': 'new_m', 'new_m_w_o_sc': 'new_m', 'new_m_w_out': 'new_m', 'new_m_ln1_g': 'new_m', 'new_m_ln1_b': 'new_m', 'new_m_w_up': 'new_m', 'new_m_b_up': 'new_m', 'new_m_w_down': 'new_m', 'new_m_b_down': 'new_m', 'new_m_ln2_g': 'new_m', 'new_m_ln2_b': 'new_m', 'new_v_w_in': 'new_v', 'new_v_conv_qkv': 'new_v', 'new_v_a_log': 'new_v', 'new_v_dt_bias': 'new_v', 'new_v_gdn_norm_w': 'new_v', 'new_v_w_o_gdn': 'new_v', 'new_v_conv_sc': 'new_v', 'new_v_w_o_sc': 'new_v', 'new_v_w_out': 'new_v', 'new_v_ln1_g': 'new_v', 'new_v_ln1_b': 'new_v', 'new_v_w_up': 'new_v', 'new_v_b_up': 'new_v', 'new_v_w_down': 'new_v', 'new_v_b_down': 'new_v', 'new_v_ln2_g': 'new_v', 'new_v_ln2_b': 'new_v'}


def _forward(args):
    return _fwd_reference(*[args[k] for k in FWD_PARAMS])


def _output_shape():
    def fwd():
        inp = _fwd_setup_inputs(0)
        return _fwd_reference(*[inp[k] for k in FWD_PARAMS])
    out = _jax.eval_shape(fwd)
    return out.shape, out.dtype

N_MICROBATCH = 1
ADAM_LR = 0.001
ADAM_B1 = 0.9
ADAM_B2 = 0.999
ADAM_EPS = 1e-08
ADAM_WD = 0.01
ADAM_STEP = 10
PER_EXAMPLE_BATCH_AXIS = {'x': 0, 'loss_target': 0}
SHARED_INPUTS = []
_WEIGHT_DTYPES = {'w_in': _jnp.float32, 'conv_qkv': _jnp.float32, 'a_log': _jnp.float32, 'dt_bias': _jnp.float32, 'gdn_norm_w': _jnp.float32, 'w_o_gdn': _jnp.float32, 'conv_sc': _jnp.float32, 'w_o_sc': _jnp.float32, 'w_out': _jnp.float32, 'ln1_g': _jnp.float32, 'ln1_b': _jnp.float32, 'w_up': _jnp.float32, 'b_up': _jnp.float32, 'w_down': _jnp.float32, 'b_down': _jnp.float32, 'ln2_g': _jnp.float32, 'ln2_b': _jnp.float32}
MOMENT_SCALE = {'w_in': 3.449811e-02, 'conv_qkv': 2.517994e-02, 'a_log': 1.260905e-01, 'dt_bias': 1.241743e-01, 'gdn_norm_w': 1.083761e-01, 'w_o_gdn': 3.761926e-02, 'conv_sc': 5.107935e-02, 'w_o_sc': 5.065687e-02, 'w_out': 1.489107e-01, 'ln1_g': 3.115284e+00, 'ln1_b': 1.773132e+00, 'w_up': 6.573089e-02, 'b_up': 1.495988e-01, 'w_down': 4.538545e-01, 'b_down': 1.048813e+00, 'ln2_g': 6.435243e+01, 'ln2_b': 1.446274e+01}


def _to_microbatches(a, axis):
    t = _jnp.moveaxis(a, axis, 0)
    t = t.reshape((N_MICROBATCH, t.shape[0] // N_MICROBATCH) + t.shape[1:])
    return _jnp.moveaxis(t, 1, axis + 1)


def setup_inputs(seed: int = 0) -> dict:
    inp = _fwd_setup_inputs(seed)
    key = _jax.random.fold_in(_jax.random.key(seed), 7919)
    shape, _ = _output_shape()
    out = dict(inp)
    out["loss_target"] = _jax.random.normal(_jax.random.fold_in(key, 0), shape, _jnp.float32)
    for i, name in enumerate(TWIN_WEIGHTS):
        w = inp[name].astype(_jnp.float32)
        if MOMENT_SCALE is None:
            s = _jnp.sqrt(_jnp.mean(_jnp.square(w)) + 1e-30)
        else:
            s = MOMENT_SCALE[name]
        km, kv = _jax.random.split(_jax.random.fold_in(key, i + 1))
        out[name] = w
        out["m_" + name] = s * _jax.random.normal(km, w.shape, _jnp.float32)
        out["v_" + name] = (s * s) * _jax.random.uniform(kv, w.shape, _jnp.float32, 0.5, 1.5)
    if N_MICROBATCH > 1:
        for name, axis in PER_EXAMPLE_BATCH_AXIS.items():
            out[name] = _to_microbatches(out[name], axis)
    return {'x': out['x'], 'w_in': out['w_in'], 'conv_qkv': out['conv_qkv'], 'a_log': out['a_log'], 'dt_bias': out['dt_bias'], 'gdn_norm_w': out['gdn_norm_w'], 'w_o_gdn': out['w_o_gdn'], 'conv_sc': out['conv_sc'], 'w_o_sc': out['w_o_sc'], 'w_out': out['w_out'], 'ln1_g': out['ln1_g'], 'ln1_b': out['ln1_b'], 'w_up': out['w_up'], 'b_up': out['b_up'], 'w_down': out['w_down'], 'b_down': out['b_down'], 'ln2_g': out['ln2_g'], 'ln2_b': out['ln2_b'], 'loss_target': out['loss_target'], 'm_w_in': out['m_w_in'], 'm_conv_qkv': out['m_conv_qkv'], 'm_a_log': out['m_a_log'], 'm_dt_bias': out['m_dt_bias'], 'm_gdn_norm_w': out['m_gdn_norm_w'], 'm_w_o_gdn': out['m_w_o_gdn'], 'm_conv_sc': out['m_conv_sc'], 'm_w_o_sc': out['m_w_o_sc'], 'm_w_out': out['m_w_out'], 'm_ln1_g': out['m_ln1_g'], 'm_ln1_b': out['m_ln1_b'], 'm_w_up': out['m_w_up'], 'm_b_up': out['m_b_up'], 'm_w_down': out['m_w_down'], 'm_b_down': out['m_b_down'], 'm_ln2_g': out['m_ln2_g'], 'm_ln2_b': out['m_ln2_b'], 'v_w_in': out['v_w_in'], 'v_conv_qkv': out['v_conv_qkv'], 'v_a_log': out['v_a_log'], 'v_dt_bias': out['v_dt_bias'], 'v_gdn_norm_w': out['v_gdn_norm_w'], 'v_w_o_gdn': out['v_w_o_gdn'], 'v_conv_sc': out['v_conv_sc'], 'v_w_o_sc': out['v_w_o_sc'], 'v_w_out': out['v_w_out'], 'v_ln1_g': out['v_ln1_g'], 'v_ln1_b': out['v_ln1_b'], 'v_w_up': out['v_w_up'], 'v_b_up': out['v_b_up'], 'v_w_down': out['v_w_down'], 'v_b_down': out['v_b_down'], 'v_ln2_g': out['v_ln2_g'], 'v_ln2_b': out['v_ln2_b']}


def _loss(weights, diff, rest, loss_target):
    with _jax.named_scope("forward"):
        args = {**rest, TWIN_DIFF_INPUT: diff, **{k: w.astype(_WEIGHT_DTYPES[k]) for k, w in weights.items()}}
        y = _forward(args)
    with _jax.named_scope("loss_head"):
        err = _jnp.square(y.astype(_jnp.float32) - loss_target)
        return 0.5 * _jnp.sum(_jnp.mean(err, axis=-1)) if err.ndim else 0.5 * err


def _adamw(w, g, m, v):
    m = ADAM_B1 * m + (1.0 - ADAM_B1) * g
    v = ADAM_B2 * v + (1.0 - ADAM_B2) * _jnp.square(g)
    m_hat = m / (1.0 - ADAM_B1 ** ADAM_STEP)
    v_hat = v / (1.0 - ADAM_B2 ** ADAM_STEP)
    delta = -ADAM_LR * (m_hat / (_jnp.sqrt(v_hat) + ADAM_EPS) + ADAM_WD * w)
    return delta, m, v


def reference(x, w_in, conv_qkv, a_log, dt_bias, gdn_norm_w, w_o_gdn, conv_sc, w_o_sc, w_out, ln1_g, ln1_b, w_up, b_up, w_down, b_down, ln2_g, ln2_b, loss_target, m_w_in, m_conv_qkv, m_a_log, m_dt_bias, m_gdn_norm_w, m_w_o_gdn, m_conv_sc, m_w_o_sc, m_w_out, m_ln1_g, m_ln1_b, m_w_up, m_b_up, m_w_down, m_b_down, m_ln2_g, m_ln2_b, v_w_in, v_conv_qkv, v_a_log, v_dt_bias, v_gdn_norm_w, v_w_o_gdn, v_conv_sc, v_w_o_sc, v_w_out, v_ln1_g, v_ln1_b, v_w_up, v_b_up, v_w_down, v_b_down, v_ln2_g, v_ln2_b):
    given = dict(x=x, w_in=w_in, conv_qkv=conv_qkv, a_log=a_log, dt_bias=dt_bias, gdn_norm_w=gdn_norm_w, w_o_gdn=w_o_gdn, conv_sc=conv_sc, w_o_sc=w_o_sc, w_out=w_out, ln1_g=ln1_g, ln1_b=ln1_b, w_up=w_up, b_up=b_up, w_down=w_down, b_down=b_down, ln2_g=ln2_g, ln2_b=ln2_b, loss_target=loss_target, m_w_in=m_w_in, m_conv_qkv=m_conv_qkv, m_a_log=m_a_log, m_dt_bias=m_dt_bias, m_gdn_norm_w=m_gdn_norm_w, m_w_o_gdn=m_w_o_gdn, m_conv_sc=m_conv_sc, m_w_o_sc=m_w_o_sc, m_w_out=m_w_out, m_ln1_g=m_ln1_g, m_ln1_b=m_ln1_b, m_w_up=m_w_up, m_b_up=m_b_up, m_w_down=m_w_down, m_b_down=m_b_down, m_ln2_g=m_ln2_g, m_ln2_b=m_ln2_b, v_w_in=v_w_in, v_conv_qkv=v_conv_qkv, v_a_log=v_a_log, v_dt_bias=v_dt_bias, v_gdn_norm_w=v_gdn_norm_w, v_w_o_gdn=v_w_o_gdn, v_conv_sc=v_conv_sc, v_w_o_sc=v_w_o_sc, v_w_out=v_w_out, v_ln1_g=v_ln1_g, v_ln1_b=v_ln1_b, v_w_up=v_w_up, v_b_up=v_b_up, v_w_down=v_w_down, v_b_down=v_b_down, v_ln2_g=v_ln2_g, v_ln2_b=v_ln2_b)
    weights = {n: given[n] for n in TWIN_WEIGHTS}
    shared = {n: given[n] for n in SHARED_INPUTS}
    per_example = {n: given[n] for n in ['x']}
    grad_fn = _jax.value_and_grad(_loss, argnums=(0, 1))

    def one_microbatch(ex, loss_target):
        ex = dict(ex)
        diff = ex.pop(TWIN_DIFF_INPUT)
        return grad_fn(weights, diff, {**shared, **ex}, loss_target)

    if N_MICROBATCH == 1:
        loss, (grad_w, grad_x) = one_microbatch(per_example, given["loss_target"])
    else:
        def body(carry, xs):
            loss_sum, grad_sum = carry
            l_k, (gw_k, gx_k) = one_microbatch(xs[0], xs[1])
            with _jax.named_scope("update"):
                return (loss_sum + l_k, _jax.tree.map(_jnp.add, grad_sum, gw_k)), gx_k

        init = (_jnp.zeros((), _jnp.float32), _jax.tree.map(_jnp.zeros_like, weights))
        (loss, grad_w), grad_x = _jax.lax.scan(body, init, (per_example, given["loss_target"]))
    with _jax.named_scope("update"):
        delta_w, new_m, new_v = {}, {}, {}
        for n in TWIN_WEIGHTS:
            delta_w[n], new_m[n], new_v[n] = _adamw(weights[n], grad_w[n], given["m_" + n], given["v_" + n])
    return (loss, grad_x, *[grad_w[n] for n in TWIN_WEIGHTS], *[delta_w[n] for n in TWIN_WEIGHTS],
            *[new_m[n] for n in TWIN_WEIGHTS], *[new_v[n] for n in TWIN_WEIGHTS])
```

```python
import functools

import jax
import jax.numpy as jnp
from jax import lax
from jax.experimental import pallas as pl
from jax.experimental.pallas import tpu as pltpu

F32 = jnp.float32
BF16 = jnp.bfloat16

D_MODEL = 1024
N_HEADS = 8
HEAD_DIM = 128
CHUNK = 64
D_FF = 4 * D_MODEL
DEPTH = 4
N_DEV = 8
LN_EPS = 1e-5
RMS_EPS = 1e-6
L2_EPS = 1e-6
ALPHA = (2 * DEPTH) ** 0.25
MAIN_COLS = 9 * D_MODEL
QKVZ_COLS = 4 * D_MODEL
AB_COLS = 4 * N_HEADS
W_IN_COLS = MAIN_COLS + AB_COLS
LANES = 128
SUBLANES = 8
VMEM_LIMIT = 48 * 1024 * 1024
MM_TILE = 1024
ROW_TILE = 256
WIDE_ROW_TILE = 128
ADAM = dict(lr=0.001, b1=0.9, b2=0.999, eps=1e-08, wd=0.01, step=10)
MESH_AXES = ("x", "y", "c")


def _params(sem=None):
    return pltpu.CompilerParams(dimension_semantics=sem, vmem_limit_bytes=VMEM_LIMIT)


_DIMS = {"nn": (1, 0), "nt": (1, 1), "tn": (0, 0)}


def _mm(a, b, mode, name, *, out_dtype=F32, b_off=0, kdim=None, addends=()):
    if mode == "nn":
        (m, k), n = a.shape, b.shape[1]
    elif mode == "nt":
        (m, k), n = a.shape, b.shape[0]
        k = kdim or k
    else:
        (k, m), n = a.shape, b.shape[1]
    tm, tn, tk = min(m, MM_TILE), min(n, MM_TILE), min(k, MM_TILE)
    assert m % tm == 0 and n % tn == 0 and k % tk == 0 and b_off % tk == 0
    nk = k // tk
    koff = b_off // tk
    ca, cb = _DIMS[mode]
    scales = tuple(s for s, _ in addends)
    na = len(addends)

    def body(a_ref, b_ref, *rest):
        add_refs, o_ref = rest[:na], rest[na]
        kk = pl.program_id(2)
        p = lax.dot_general(a_ref[...].astype(BF16), b_ref[...].astype(BF16), (((ca,), (cb,)), ((), ())),
                            preferred_element_type=F32)

        def finish(r):
            for s, ref in zip(scales, add_refs):
                r = r + s * ref[...].astype(F32)
            o_ref[...] = r.astype(o_ref.dtype)

        if nk == 1:
            finish(p)
        else:
            acc = rest[na + 1]

            @pl.when(kk == 0)
            def _():
                acc[...] = p

            @pl.when(kk > 0)
            def _():
                acc[...] += p

            @pl.when(kk == nk - 1)
            def _():
                finish(acc[...])

    if mode == "nn":
        a_spec = pl.BlockSpec((tm, tk), lambda i, j, kk: (i, kk))
        b_spec = pl.BlockSpec((tk, tn), lambda i, j, kk: (kk, j))
    elif mode == "nt":
        a_spec = pl.BlockSpec((tm, tk), lambda i, j, kk: (i, kk))
        b_spec = pl.BlockSpec((tn, tk), lambda i, j, kk: (j, kk + koff))
    else:
        a_spec = pl.BlockSpec((tk, tm), lambda i, j, kk: (kk, i))
        b_spec = pl.BlockSpec((tk, tn), lambda i, j, kk: (kk, j))
    o_spec = pl.BlockSpec((tm, tn), lambda i, j, kk: (i, j))
    return pl.pallas_call(
        body, name=name, grid=(m // tm, n // tn, nk),
        in_specs=[a_spec, b_spec] + [o_spec] * na, out_specs=o_spec,
        out_shape=jax.ShapeDtypeStruct((m, n), out_dtype),
        scratch_shapes=[pltpu.VMEM((tm, tn), F32)] if nk > 1 else [],
        compiler_params=_params(("parallel", "parallel", "arbitrary")),
    )(a, b, *[arr for _, arr in addends])


def _tile_call(name, body, t, tm, tiled, halo, params, outs, accs):
    tm = min(tm, t)
    assert t % tm == 0 and tm % SUBLANES == 0
    steps = t // tm
    hb = tm // SUBLANES
    nt, nh, npar, no = len(tiled), len(halo), len(params), len(outs)

    def kern(*refs):
        i = pl.program_id(0)
        t_refs = refs[:nt + nh]
        h_refs = refs[nt + nh:nt + 3 * nh]
        p_refs = refs[nt + 3 * nh:nt + 3 * nh + npar]
        o_refs = refs[nt + 3 * nh + npar:nt + 3 * nh + npar + no]
        a_refs = refs[nt + 3 * nh + npar + no:]
        tiles = [r[...] for r in t_refs]
        halos = []
        for j in range(nh):
            prev = h_refs[2 * j][SUBLANES - 1:SUBLANES, :].astype(F32)
            nxt = h_refs[2 * j + 1][0:1, :].astype(F32)
            halos.append((jnp.where(i > 0, prev, 0.0), jnp.where(i < steps - 1, nxt, 0.0)))
        o_vals, a_vals = body(tiles, halos, [r[...] for r in p_refs])
        for ref, val in zip(o_refs, o_vals):
            ref[...] = val.astype(ref.dtype)
        for ref, val in zip(a_refs, a_vals):
            @pl.when(i == 0)
            def _(ref=ref, val=val):
                ref[...] = val

            @pl.when(i > 0)
            def _(ref=ref, val=val):
                ref[...] += val

    in_specs, args = [], []
    for arr, nc, cb in list(tiled) + list(halo):
        in_specs.append(pl.BlockSpec((tm, nc), lambda i, cb=cb: (i, cb)))
        args.append(arr)
    last = t // SUBLANES - 1
    for arr, nc, cb in halo:
        in_specs.append(pl.BlockSpec((SUBLANES, nc), lambda i, cb=cb: (jnp.maximum(i * hb - 1, 0), cb)))
        in_specs.append(pl.BlockSpec((SUBLANES, nc), lambda i, cb=cb: (jnp.minimum((i + 1) * hb, last), cb)))
        args += [arr, arr]
    for arr in params:
        in_specs.append(pl.BlockSpec(arr.shape, lambda i: (0, 0)))
        args.append(arr)
    out_specs = [pl.BlockSpec((tm, nc), lambda i: (i, 0)) for nc, _ in outs]
    out_specs += [pl.BlockSpec(shape, lambda i: (0, 0)) for shape in accs]
    out_shape = [jax.ShapeDtypeStruct((t, nc), dt) for nc, dt in outs]
    out_shape += [jax.ShapeDtypeStruct(shape, F32) for shape in accs]
    res = pl.pallas_call(kern, name=name, grid=(steps,), in_specs=in_specs, out_specs=out_specs,
                         out_shape=out_shape, compiler_params=_params(("arbitrary",)))(*args)
    return res[:no], res[no:]


def _row_iota(x):
    return lax.broadcasted_iota(jnp.int32, x.shape, 0)


def _lane_iota(x):
    return lax.broadcasted_iota(jnp.int32, x.shape, 1)


def _shift_down(x, first_row):
    return jnp.where(_row_iota(x) == 0, first_row, pltpu.roll(x, 1, 0))


def _shift_up(x, last_row):
    n = x.shape[0]
    return jnp.where(_row_iota(x) == n - 1, last_row, pltpu.roll(x, n - 1, 0))


def _taps(w):
    return w[0:1, :], w[1:2, :], w[2:3, :]


def _tap_rows(d0, d1, d2, rows=SUBLANES):
    r = lax.broadcasted_iota(jnp.int32, (rows, d0.shape[1]), 0)
    return jnp.where(r == 0, d0, jnp.where(r == 1, d1, jnp.where(r == 2, d2, 0.0)))


def _colsum(x):
    return jnp.sum(x, axis=0, keepdims=True)


def _silu(x):
    return x * jax.nn.sigmoid(x)


def _softplus(x):
    return jnp.maximum(x, 0.0) + jnp.log(1.0 + jnp.exp(-jnp.abs(x)))


def _heads(x):
    return [x[:, h * HEAD_DIM:(h + 1) * HEAD_DIM] for h in range(x.shape[1] // HEAD_DIM)]


def _post_conv(c):
    blocks = _heads(_silu(c))
    out = []
    for j, blk in enumerate(blocks):
        if j < 2 * N_HEADS:
            blk = blk * lax.rsqrt(jnp.sum(blk * blk, axis=-1, keepdims=True) + L2_EPS)
        if j < N_HEADS:
            blk = blk * (HEAD_DIM ** -0.5)
        out.append(blk)
    return jnp.concatenate(out, axis=1)


def _gating(ab, a_log, dt_bias):
    lane = _lane_iota(ab)
    g = -jnp.exp(a_log) * _softplus(ab + dt_bias)
    return jnp.where(lane < 2 * N_HEADS, g, jnp.where(lane < AB_COLS, jax.nn.sigmoid(ab), 0.0))


def _gate_norm(o_f, o_b, z, norm_w):
    out = []
    for oh, zh in zip(_heads(o_f + o_b), _heads(z)):
        out.append(oh * lax.rsqrt(jnp.mean(oh * oh, axis=-1, keepdims=True) + RMS_EPS) * norm_w * _silu(zh))
    return jnp.concatenate(out, axis=1)


def _mix(gate_a, gate_b, y_a, y_b):
    return jax.nn.sigmoid(gate_a) * y_a + jax.nn.sigmoid(gate_b) * y_b


def _layer_norm(u, g, b):
    mu = jnp.mean(u, axis=-1, keepdims=True)
    var = jnp.mean(jnp.square(u - mu), axis=-1, keepdims=True)
    return (u - mu) * lax.rsqrt(var + LN_EPS) * g + b


def _ln1(x, r, g, b):
    return _layer_norm(ALPHA * x + r, g, b)


def _ln2(x, r, bias, g, b):
    return _layer_norm(ALPHA * x + r + bias, g, b)


def _relu2(hpre, bias):
    return jnp.square(jnp.maximum(hpre + bias, 0.0))


def _qkv_conv_fwd(proj, conv_w, t):
    def body(tiles, halos, params):
        (x,), ((xp, xn),), (w,) = tiles, halos, params
        w0, w1, w2 = _taps(w)
        c = w0 * _shift_down(x, xp) + w1 * x + w2 * _shift_up(x, xn)
        return [c, _post_conv(c)], []

    (c, qkvn), _ = _tile_call("qkv_conv_fwd", body, t, WIDE_ROW_TILE, [], [(proj, 3 * D_MODEL, 0)], [conv_w],
                              [(3 * D_MODEL, F32), (3 * D_MODEL, F32)], [])
    return c, qkvn


def _gating_fwd(proj_ab, a_log, dt_bias, t):
    def body(tiles, halos, params):
        return [_gating(tiles[0], params[0], params[1])], []

    (gb,), _ = _tile_call("gating_fwd", body, t, ROW_TILE, [(proj_ab, LANES, 0)], [], [a_log, dt_bias],
                          [(LANES, F32)], [])
    return gb


def _gate_norm_fwd(o_f, o_b, proj, norm_w, t):
    def body(tiles, halos, params):
        return [_gate_norm(tiles[0], tiles[1], tiles[2], params[0])], []

    (og,), _ = _tile_call("gate_norm_fwd", body, t, ROW_TILE,
                          [(o_f, D_MODEL, 0), (o_b, D_MODEL, 0), (proj, D_MODEL, 3)], [], [norm_w],
                          [(D_MODEL, BF16)], [])
    return og


def _sc_fwd(proj, conv_w, t):
    def body(tiles, halos, params):
        (sb,), ((cp, cn), (xp, xn)), (w,) = tiles[:1], halos, params
        sc, sx = tiles[1], tiles[2]
        w0, w1, w2 = _taps(w)
        u = sc * sx
        return [sb * (w0 * _shift_down(u, cp * xp) + w1 * u + w2 * _shift_up(u, cn * xn))], []

    (s,), _ = _tile_call("sc_fwd", body, t, ROW_TILE, [(proj, D_MODEL, 4)],
                         [(proj, D_MODEL, 5), (proj, D_MODEL, 6)], [conv_w], [(D_MODEL, BF16)], [])
    return s


def _mix_fwd(proj, y_a, y_b, t):
    def body(tiles, halos, params):
        return [_mix(*tiles)], []

    (mixed,), _ = _tile_call("mix_fwd", body, t, ROW_TILE,
                             [(proj, D_MODEL, 7), (proj, D_MODEL, 8), (y_a, D_MODEL, 0), (y_b, D_MODEL, 0)], [], [],
                             [(D_MODEL, BF16)], [])
    return mixed


def _ln1_fwd(x, r, g, b, t):
    def body(tiles, halos, params):
        y = _ln1(tiles[0], tiles[1], params[0], params[1])
        return [y, y], []

    (y, y16), _ = _tile_call("ln1_fwd", body, t, ROW_TILE, [(x, D_MODEL, 0), (r, D_MODEL, 0)], [], [g, b],
                             [(D_MODEL, F32), (D_MODEL, BF16)], [])
    return y, y16


def _ln2_fwd(x, r, bias, g, b, t):
    def body(tiles, halos, params):
        y = _ln2(tiles[0], tiles[1], params[0], params[1], params[2])
        return [y, y], []

    (y, y16), _ = _tile_call("ln2_fwd", body, t, ROW_TILE, [(x, D_MODEL, 0), (r, D_MODEL, 0)], [], [bias, g, b],
                             [(D_MODEL, F32), (D_MODEL, BF16)], [])
    return y, y16


def _relu2_fwd(hpre, bias, t):
    def body(tiles, halos, params):
        return [_relu2(tiles[0], params[0])], []

    (h,), _ = _tile_call("relu2_fwd", body, t, ROW_TILE, [(hpre, D_FF, 0)], [], [bias], [(D_FF, BF16)], [])
    return h


def _loss_stage(y, target, t):
    def body(tiles, halos, params):
        d = tiles[0] - tiles[1]
        part = 0.5 * jnp.sum(jnp.mean(d * d, axis=-1, keepdims=True), axis=0, keepdims=True)
        return [d * (1.0 / D_MODEL)], [jnp.broadcast_to(part, (1, LANES))]

    (dy,), (loss,) = _tile_call("loss", body, t, ROW_TILE, [(y, D_MODEL, 0), (target, D_MODEL, 0)], [], [],
                                [(D_MODEL, F32)], [(1, LANES)])
    return dy, loss[0, 0]


def _ln2_bwd(x, r, bias, g, b, dy, t):
    def body(tiles, halos, params):
        _, vjp = jax.vjp(_ln2, tiles[0], tiles[1], params[0], params[1], params[2])
        dx, dr, dbias, dg, db = vjp(tiles[2])
        return [dx, dr], [dbias, dg, db]

    return _tile_call("ln2_bwd", body, t, ROW_TILE, [(x, D_MODEL, 0), (r, D_MODEL, 0), (dy, D_MODEL, 0)], [],
                      [bias, g, b], [(D_MODEL, F32), (D_MODEL, BF16)], [(1, D_MODEL)] * 3)


def _ln1_bwd(x, r, g, b, dy, t):
    def body(tiles, halos, params):
        _, vjp = jax.vjp(_ln1, tiles[0], tiles[1], params[0], params[1])
        dx, dr, dg, db = vjp(tiles[2])
        return [dx, dr], [dg, db]

    return _tile_call("ln1_bwd", body, t, ROW_TILE, [(x, D_MODEL, 0), (r, D_MODEL, 0), (dy, D_MODEL, 0)], [],
                      [g, b], [(D_MODEL, F32), (D_MODEL, BF16)], [(1, D_MODEL)] * 2)


def _relu2_bwd(hpre, bias, dh, t):
    def body(tiles, halos, params):
        _, vjp = jax.vjp(_relu2, tiles[0], params[0])
        dhpre, dbias = vjp(tiles[1])
        return [dhpre], [dbias]

    (dhpre,), (dbias,) = _tile_call("relu2_bwd", body, t, WIDE_ROW_TILE, [(hpre, D_FF, 0), (dh, D_FF, 0)], [],
                                    [bias], [(D_FF, BF16)], [(1, D_FF)])
    return dhpre, dbias


def _mix_bwd(proj, y_a, y_b, dmixed, t):
    def body(tiles, halos, params):
        _, vjp = jax.vjp(_mix, *tiles[:4])
        dga, dgb, dya, dyb = vjp(tiles[4])
        return [jnp.concatenate([dga, dgb], axis=1), dya, dyb], []

    (dgates, dya, dyb), _ = _tile_call(
        "mix_bwd", body, t, ROW_TILE,
        [(proj, D_MODEL, 7), (proj, D_MODEL, 8), (y_a, D_MODEL, 0), (y_b, D_MODEL, 0), (dmixed, D_MODEL, 0)], [], [],
        [(2 * D_MODEL, BF16), (D_MODEL, BF16), (D_MODEL, BF16)], [])
    return dgates, dya, dyb


def _sc_bwd(proj, conv_w, ds, t):
    def body(tiles, halos, params):
        ds_, sb, sc, sx = tiles
        (dsp, dsn), (sbp, sbn), (scp, scn), (sxp, sxn) = halos
        w0, w1, w2 = _taps(params[0])
        u = sc * sx
        u_prev, u_next = _shift_down(u, scp * sxp), _shift_up(u, scn * sxn)
        dconv = ds_ * sb
        du = w0 * _shift_up(dconv, dsn * sbn) + w1 * dconv + w2 * _shift_down(dconv, dsp * sbp)
        dsb = ds_ * (w0 * u_prev + w1 * u + w2 * u_next)
        dw = _tap_rows(_colsum(dconv * u_prev), _colsum(dconv * u), _colsum(dconv * u_next))
        return [jnp.concatenate([dsb, du * sx, du * sc], axis=1)], [dw]

    (dsc,), (dw,) = _tile_call("sc_bwd", body, t, ROW_TILE, [],
                               [(ds, D_MODEL, 0), (proj, D_MODEL, 4), (proj, D_MODEL, 5), (proj, D_MODEL, 6)],
                               [conv_w], [(3 * D_MODEL, BF16)], [(SUBLANES, D_MODEL)])
    return dsc, dw


def _gate_norm_bwd(o_f, o_b, proj, norm_w, dog, t):
    def body(tiles, halos, params):
        _, vjp = jax.vjp(_gate_norm, tiles[0], tiles[1], tiles[2], params[0])
        do, _, dz, dnw = vjp(tiles[3])
        return [do, dz], [dnw]

    (do, dz), (dnw,) = _tile_call(
        "gate_norm_bwd", body, t, ROW_TILE,
        [(o_f, D_MODEL, 0), (o_b, D_MODEL, 0), (proj, D_MODEL, 3), (dog, D_MODEL, 0)], [], [norm_w],
        [(D_MODEL, F32), (D_MODEL, BF16)], [(1, HEAD_DIM)])
    return do, dz, dnw


def _post_conv_bwd(c, dq_f, dq_b, t):
    def body(tiles, halos, params):
        _, vjp = jax.vjp(_post_conv, tiles[0])
        return [vjp(tiles[1] + tiles[2])[0]], []

    (dc,), _ = _tile_call("post_conv_bwd", body, t, WIDE_ROW_TILE,
                          [(c, 3 * D_MODEL, 0), (dq_f, 3 * D_MODEL, 0), (dq_b, 3 * D_MODEL, 0)], [], [],
                          [(3 * D_MODEL, F32)], [])
    return dc


def _qkv_conv_bwd(proj, conv_w, dc, t):
    def body(tiles, halos, params):
        (dcp, dcn), (xp, xn) = halos
        d, x = tiles
        w0, w1, w2 = _taps(params[0])
        dx = w0 * _shift_up(d, dcn) + w1 * d + w2 * _shift_down(d, dcp)
        dw = _tap_rows(_colsum(d * _shift_down(x, xp)), _colsum(d * x), _colsum(d * _shift_up(x, xn)))
        return [dx], [dw]

    (dqkv,), (dw,) = _tile_call("qkv_conv_bwd", body, t, WIDE_ROW_TILE, [],
                                [(dc, 3 * D_MODEL, 0), (proj, 3 * D_MODEL, 0)], [conv_w],
                                [(3 * D_MODEL, BF16)], [(SUBLANES, 3 * D_MODEL)])
    return dqkv, dw


def _gating_bwd(proj_ab, a_log, dt_bias, dgb_f, dgb_b, t):
    def body(tiles, halos, params):
        _, vjp = jax.vjp(_gating, tiles[0], params[0], params[1])
        dab, dal, ddt = vjp(tiles[1] + tiles[2])
        return [dab], [dal, ddt]

    (dab,), (dal, ddt) = _tile_call("gating_bwd", body, t, ROW_TILE,
                                    [(proj_ab, LANES, 0), (dgb_f, LANES, 0), (dgb_b, LANES, 0)], [],
                                    [a_log, dt_bias], [(LANES, BF16)], [(1, LANES)] * 2)
    return dab, dal, ddt


@functools.partial(jax.custom_vjp, nondiff_argnums=(2, 3))
def _dot(a, b, ca, cb):
    return lax.dot_general(a.astype(BF16), b.astype(BF16), (((ca,), (cb,)), ((), ())), preferred_element_type=F32)


def _dot_fwd(a, b, ca, cb):
    return _dot(a, b, ca, cb), (a, b)


def _dot_bwd(ca, cb, res, ct):
    a, b = res
    fa, fb = 1 - ca, 1 - cb
    da = _dot(ct, b, 1, fb) if ca == 1 else _dot(b, ct, fb, 1)
    db = _dot(a, ct, fa, 0) if cb == 0 else _dot(ct, a, 0, fa)
    return da, db


_dot.defvjp(_dot_fwd, _dot_bwd)


def _dot3(a, b, ca, cb):
    ah, bh = a.astype(BF16).astype(F32), b.astype(BF16).astype(F32)
    return _dot(ah, bh, ca, cb) + _dot(a - ah, bh, ca, cb) + _dot(ah, b - bh, ca, cb)


def _split3(x):
    hi = x.astype(BF16)
    r1 = x - hi.astype(F32)
    mid = r1.astype(BF16)
    return hi, mid, (r1 - mid.astype(F32)).astype(BF16)


def _dot_exact(a, b, ca, cb, exact):
    dims = (((ca,), (cb,)), ((), ()))
    if exact == 0:
        return sum(lax.dot_general(a.astype(BF16), p, dims, preferred_element_type=F32) for p in _split3(b))
    return sum(lax.dot_general(p, b.astype(BF16), dims, preferred_element_type=F32) for p in _split3(a))


def _tri_masks(n, rev):
    r = lax.broadcasted_iota(jnp.int32, (n, n), 0)
    c = lax.broadcasted_iota(jnp.int32, (n, n), 1)
    return ((c >= r), (c > r)) if rev else ((c <= r), (c < r))


@functools.partial(jax.custom_vjp, nondiff_argnums=(1,))
def _cumsum_rows(g, rev):
    incl, _ = _tri_masks(g.shape[0], rev)
    return _dot_exact(incl.astype(F32), g, 1, 0, 0)


_cumsum_rows.defvjp(lambda g, rev: (_cumsum_rows(g, rev), None),
                    lambda rev, _, ct: (_cumsum_rows(ct, not rev),))


def _eye(n):
    return (lax.broadcasted_iota(jnp.int32, (n, n), 0) == lax.broadcasted_iota(jnp.int32, (n, n), 1)).astype(F32)


@jax.custom_vjp
def _to_rows(x):
    return _dot_exact(_eye(x.shape[1]), x, 1, 1, 0)


@jax.custom_vjp
def _to_cols(y):
    return _dot_exact(y, _eye(y.shape[0]), 0, 0, 1)


_to_rows.defvjp(lambda x: (_to_rows(x), None), lambda _, ct: (_to_cols(ct),))
_to_cols.defvjp(lambda y: (_to_cols(y), None), lambda _, ct: (_to_rows(ct),))


def _pick_col(arr, idx):
    return jnp.sum(jnp.where(_lane_iota(arr) == idx, arr, 0.0), axis=1, keepdims=True)


def _pick_row(arr, idx):
    return jnp.sum(jnp.where(_row_iota(arr) == idx, arr, 0.0), axis=0, keepdims=True)


def _chunk_gates(gb, direction, rev):
    n = gb.shape[0]
    incl, strict = _tri_masks(n, rev)
    gc = _cumsum_rows(gb, rev)
    gc_rows = _to_rows(gc)
    out = []
    for h in range(N_HEADS):
        col = _pick_col(gc, direction * N_HEADS + h)
        row = _pick_row(gc_rows, direction * N_HEADS + h)
        beta = _pick_col(gb, 2 * N_HEADS + direction * N_HEADS + h)
        decay = jnp.where(incl, jnp.exp(jnp.where(incl, col - row, 0.0)), 0.0)
        out.append((col, beta, decay, strict))
    return out


def _chunk_lmat(k, gb, direction, rev):
    out = []
    for kh, (col, beta, decay, strict) in zip(_heads(k), _chunk_gates(gb, direction, rev)):
        out.append(jnp.where(strict, _dot(kh * beta, kh, 1, 1) * decay, 0.0))
    return tuple(out)


def _tri_inverse(lmat):
    n = lmat.shape[0]
    eye = _eye(n)
    power = -lmat
    inv = eye + power
    span = 2
    while span < n:
        power = _dot(power, power, 1, 0)
        inv = inv + _dot(power, inv, 1, 0)
        span *= 2
    resid = eye - inv - _dot3(lmat, inv, 1, 0)
    return inv + _dot(inv, resid, 1, 0)


def _chunk_out(q, k, v, gb, tmats, states, direction, rev):
    n = gb.shape[0]
    last_idx = 0 if rev else n - 1
    outs, new_states = [], []
    for qh, kh, vh, tm, st, (col, beta, decay, _) in zip(_heads(q), _heads(k), _heads(v), tmats, states,
                                                         _chunk_gates(gb, direction, rev)):
        last = _pick_row(col, last_idx)
        u = _dot(tm, vh * beta, 1, 0)
        w = _dot(tm, kh * beta * jnp.exp(col), 1, 0)
        attn = _dot(qh, kh, 1, 1) * decay
        v_new = u - _dot(w, st, 1, 0)
        outs.append(_dot(qh * jnp.exp(col), st, 1, 0) + _dot(attn, v_new, 1, 0))
        new_states.append(st * jnp.exp(last) + _dot(kh * jnp.exp(last - col), v_new, 0, 0))
    return jnp.concatenate(outs, axis=1), tuple(new_states)


def _chunk_index(n, backwards):
    return (lambda i: n - 1 - i) if backwards else (lambda i: i)


def _gdn_fwd(qkvn, gb, direction, t):
    n = t // CHUNK
    rev = direction == 1
    idx = _chunk_index(n, rev)

    def body(q_ref, k_ref, v_ref, gb_ref, o_ref, s_ref, t_ref, state):
        @pl.when(pl.program_id(0) == 0)
        def _():
            state[...] = jnp.zeros_like(state)

        k, gbv = k_ref[...], gb_ref[...]
        tmats = tuple(_tri_inverse(lm) for lm in _chunk_lmat(k, gbv, direction, rev))
        states = tuple(state[h] for h in range(N_HEADS))
        o, new_states = _chunk_out(q_ref[...], k, v_ref[...], gbv, tmats, states, direction, rev)
        o_ref[...] = o
        for h in range(N_HEADS):
            s_ref[0, h] = states[h]
            t_ref[0, h] = tmats[h]
            state[h] = new_states[h]

    qkv_specs = [pl.BlockSpec((CHUNK, D_MODEL), lambda i, p=p: (idx(i), p)) for p in range(3)]
    return pl.pallas_call(
        body, name=f"gdn_fwd_{direction}", grid=(n,),
        in_specs=qkv_specs + [pl.BlockSpec((CHUNK, LANES), lambda i: (idx(i), 0))],
        out_specs=[pl.BlockSpec((CHUNK, D_MODEL), lambda i: (idx(i), 0)),
                   pl.BlockSpec((1, N_HEADS, HEAD_DIM, HEAD_DIM), lambda i: (idx(i), 0, 0, 0)),
                   pl.BlockSpec((1, N_HEADS, CHUNK, CHUNK), lambda i: (idx(i), 0, 0, 0))],
        out_shape=[jax.ShapeDtypeStruct((t, D_MODEL), F32),
                   jax.ShapeDtypeStruct((n, N_HEADS, HEAD_DIM, HEAD_DIM), F32),
                   jax.ShapeDtypeStruct((n, N_HEADS, CHUNK, CHUNK), F32)],
        scratch_shapes=[pltpu.VMEM((N_HEADS, HEAD_DIM, HEAD_DIM), F32)],
        compiler_params=_params(("arbitrary",)),
    )(qkvn, qkvn, qkvn, gb)


def _gdn_bwd(qkvn, gb, s_saved, t_saved, do, direction, t):
    n = t // CHUNK
    rev = direction == 1
    idx = _chunk_index(n, not rev)

    def body(q_ref, k_ref, v_ref, gb_ref, s_ref, t_ref, do_ref, dqkv_ref, dgb_ref, dstate):
        @pl.when(pl.program_id(0) == 0)
        def _():
            dstate[...] = jnp.zeros_like(dstate)

        q, k, v, gbv = q_ref[...], k_ref[...], v_ref[...], gb_ref[...]
        tmats = tuple(t_ref[0, h] for h in range(N_HEADS))
        states = tuple(s_ref[0, h] for h in range(N_HEADS))
        _, out_vjp = jax.vjp(lambda *a: _chunk_out(*a, direction, rev), q, k, v, gbv, tmats, states)
        dq, dk, dv, dgb, dtm, dst = out_vjp((do_ref[...], tuple(dstate[h] for h in range(N_HEADS))))
        dlm = tuple(-_dot3(_dot3(tm, d, 0, 0), tm, 1, 1) for tm, d in zip(tmats, dtm))
        _, lmat_vjp = jax.vjp(lambda kk, gg: _chunk_lmat(kk, gg, direction, rev), k, gbv)
        dk2, dgb2 = lmat_vjp(dlm)
        dqkv_ref[...] = jnp.concatenate([dq, dk + dk2, dv], axis=1)
        dgb_ref[...] = dgb + dgb2
        for h in range(N_HEADS):
            dstate[h] = dst[h]

    qkv_specs = [pl.BlockSpec((CHUNK, D_MODEL), lambda i, p=p: (idx(i), p)) for p in range(3)]
    return pl.pallas_call(
        body, name=f"gdn_bwd_{direction}", grid=(n,),
        in_specs=qkv_specs + [pl.BlockSpec((CHUNK, LANES), lambda i: (idx(i), 0)),
                              pl.BlockSpec((1, N_HEADS, HEAD_DIM, HEAD_DIM), lambda i: (idx(i), 0, 0, 0)),
                              pl.BlockSpec((1, N_HEADS, CHUNK, CHUNK), lambda i: (idx(i), 0, 0, 0)),
                              pl.BlockSpec((CHUNK, D_MODEL), lambda i: (idx(i), 0))],
        out_specs=[pl.BlockSpec((CHUNK, 3 * D_MODEL), lambda i: (idx(i), 0)),
                   pl.BlockSpec((CHUNK, LANES), lambda i: (idx(i), 0))],
        out_shape=[jax.ShapeDtypeStruct((t, 3 * D_MODEL), F32), jax.ShapeDtypeStruct((t, LANES), F32)],
        scratch_shapes=[pltpu.VMEM((N_HEADS, HEAD_DIM, HEAD_DIM), F32)],
        compiler_params=_params(("arbitrary",)),
    )(qkvn, qkvn, qkvn, gb, s_saved, t_saved, do)


def _mesh_pos():
    return lax.axis_index("x"), lax.axis_index("y"), lax.axis_index("c")


def _all_gather(shard):
    rows, cols = shard.shape

    def body(x_ref, out_ref, send_sems, recv_sems, local_sem):
        x, y, c = _mesh_pos()
        me, sibling = (x, y, c), (x, y, 1 - c)
        chips = [(1 - x, y), (x, 1 - y), (1 - x, 1 - y)]

        def block(px, py, pc):
            return out_ref.at[4 * px + 2 * py + pc]

        def copy(k, blk, to, src=None):
            return pltpu.make_async_remote_copy(
                src_ref=block(*blk) if src is None else src, dst_ref=block(*blk),
                send_sem=send_sems.at[k], recv_sem=recv_sems.at[k],
                device_id=to, device_id_type=pl.DeviceIdType.MESH)

        mine = pltpu.make_async_copy(x_ref, block(*me), local_sem)
        mine.start()
        first = [copy(0, me, sibling, src=x_ref)]
        first += [copy(1 + j, me, (*chip, c), src=x_ref) for j, chip in enumerate(chips)]
        for cp in first:
            cp.start()
        passed = [copy(4 + j, (*chip, c), sibling) for j, chip in enumerate(chips)]
        for j, chip in enumerate(chips):
            copy(1 + j, (*chip, c), me).wait_recv()
            passed[j].start()
        copy(0, sibling, me).wait_recv()
        for j, chip in enumerate(chips):
            copy(4 + j, (*chip, 1 - c), me).wait_recv()
        for cp in first + passed:
            cp.wait_send()
        mine.wait()

    return pl.pallas_call(
        body, name="weights_all_gather",
        out_shape=jax.ShapeDtypeStruct((N_DEV, rows, cols), shard.dtype),
        in_specs=[pl.BlockSpec(memory_space=pl.ANY)], out_specs=pl.BlockSpec(memory_space=pl.ANY),
        scratch_shapes=[pltpu.SemaphoreType.DMA((N_DEV - 1,)), pltpu.SemaphoreType.DMA((N_DEV - 1,)),
                        pltpu.SemaphoreType.DMA(())],
    )(shard)


def _scatter_blocks(blocks):
    def body(g_ref, land_ref, send_sems, recv_sems, local_sem):
        x, y, c = _mesh_pos()
        mine = 4 * x + 2 * y + c
        local = pltpu.make_async_copy(g_ref.at[mine], land_ref.at[mine], local_sem)
        local.start()
        sends, recvs = [], []
        for k in range(1, N_DEV):
            px = 1 - x if k & 4 else x
            py = 1 - y if k & 2 else y
            pc = 1 - c if k & 1 else c
            peer = 4 * px + 2 * py + pc
            sends.append(pltpu.make_async_remote_copy(
                src_ref=g_ref.at[peer], dst_ref=land_ref.at[mine],
                send_sem=send_sems.at[k - 1], recv_sem=recv_sems.at[k - 1],
                device_id=(px, py, pc), device_id_type=pl.DeviceIdType.MESH))
            recvs.append(pltpu.make_async_remote_copy(
                src_ref=g_ref.at[mine], dst_ref=land_ref.at[peer],
                send_sem=send_sems.at[k - 1], recv_sem=recv_sems.at[k - 1],
                device_id=(x, y, c), device_id_type=pl.DeviceIdType.MESH))
        for cp in sends:
            cp.start()
        for cp in recvs:
            cp.wait_recv()
        for cp in sends:
            cp.wait_send()
        local.wait()

    return pl.pallas_call(
        body, name="grads_scatter",
        out_shape=jax.ShapeDtypeStruct(blocks.shape, blocks.dtype),
        in_specs=[pl.BlockSpec(memory_space=pl.ANY)], out_specs=pl.BlockSpec(memory_space=pl.ANY),
        scratch_shapes=[pltpu.SemaphoreType.DMA((N_DEV - 1,)), pltpu.SemaphoreType.DMA((N_DEV - 1,)),
                        pltpu.SemaphoreType.DMA(())],
    )(blocks)


def _sum_slots(land):
    _, rows, cols = land.shape
    tr = _row_tile(rows, cols * 4 * N_DEV)

    def body(*refs):
        acc = refs[0][0]
        for ref in refs[1:N_DEV]:
            acc = acc + ref[0]
        refs[N_DEV][...] = acc

    return pl.pallas_call(
        body, name="grads_sum", grid=(rows // tr,),
        in_specs=[pl.BlockSpec((1, tr, cols), lambda i, s=s: (s, i, 0)) for s in range(N_DEV)],
        out_specs=pl.BlockSpec((tr, cols), lambda i: (i, 0)),
        out_shape=jax.ShapeDtypeStruct((rows, cols), land.dtype),
        compiler_params=_params(("parallel",)),
    )(*([land] * N_DEV))


def _row_tile(rows, row_bytes, budget=1 << 20):
    if rows * row_bytes <= budget or rows % SUBLANES:
        return rows
    best = SUBLANES
    for tr in range(SUBLANES, rows + 1, SUBLANES):
        if rows % tr == 0 and tr * row_bytes <= budget:
            best = tr
    return best


def _adamw(w, g, m, v, name):
    shape = w.shape
    cols = shape[-1]
    rows = w.size // cols
    tr = _row_tile(rows, cols * 4)
    b1, b2 = ADAM["b1"], ADAM["b2"]

    def body(w_ref, g_ref, m_ref, v_ref, d_ref, nm_ref, nv_ref):
        gv = g_ref[...]
        nm = b1 * m_ref[...] + (1.0 - b1) * gv
        nv = b2 * v_ref[...] + (1.0 - b2) * jnp.square(gv)
        m_hat = nm / (1.0 - b1 ** ADAM["step"])
        v_hat = nv / (1.0 - b2 ** ADAM["step"])
        d_ref[...] = -ADAM["lr"] * (m_hat / (jnp.sqrt(v_hat) + ADAM["eps"]) + ADAM["wd"] * w_ref[...])
        nm_ref[...] = nm
        nv_ref[...] = nv

    spec = pl.BlockSpec((tr, cols), lambda i: (i, 0))
    outs = pl.pallas_call(
        body, name=name, grid=(rows // tr,), in_specs=[spec] * 4, out_specs=[spec] * 3,
        out_shape=[jax.ShapeDtypeStruct((rows, cols), F32)] * 3, compiler_params=_params(("parallel",)),
    )(*[a.reshape(rows, cols) for a in (w, g, m, v)])
    return [o.reshape(shape) for o in outs]


MATRICES = ("w_in", "w_o_gdn", "w_o_sc", "w_out", "w_up", "w_down")
CONVS = ("conv_qkv", "conv_sc")
SHARDED = ("w_in", "conv_qkv", "w_o_gdn", "conv_sc", "w_o_sc", "w_out", "w_up", "w_down")
SMALL = ("a_log", "dt_bias", "gdn_norm_w", "ln1_g", "ln1_b", "b_up", "b_down", "ln2_g", "ln2_b")
COLUMN_SHARDED = ("w_in", "conv_qkv", "conv_sc", "w_up")
PACK_COLS = 1024


def _pack(parts, row_multiple):
    flat = jnp.concatenate(parts, axis=-1)
    unit = PACK_COLS * row_multiple
    pad = -flat.shape[-1] % unit
    flat = jnp.pad(flat, [(0, 0)] * (flat.ndim - 1) + [(0, pad)])
    return flat.reshape(flat.shape[:-1] + (flat.shape[-1] // PACK_COLS, PACK_COLS))


def _unshard(name, blocks):
    _, l, r, c = blocks.shape
    if name in COLUMN_SHARDED:
        return blocks.transpose(1, 2, 0, 3).reshape(l, r, N_DEV * c)
    return blocks.transpose(1, 0, 2, 3).reshape(l, N_DEV * r, c)


def _to_shards(name, full):
    l, r, c = full.shape
    if name in COLUMN_SHARDED:
        return full.reshape(l, r, N_DEV, c // N_DEV).transpose(2, 0, 1, 3).reshape(N_DEV, -1)
    return full.reshape(l, N_DEV, r // N_DEV, c).transpose(1, 0, 2, 3).reshape(N_DEV, -1)


def _gather_weights(shards):
    parts, layout = [], []
    for name in MATRICES:
        parts.append(shards[name].astype(BF16).reshape(-1))
        layout.append((name, shards[name].shape, 1))
    for name in CONVS:
        parts.append(jnp.stack(_split3(shards[name])).reshape(-1))
        layout.append((name, shards[name].shape, 3))
    packed = _pack(parts, 16)
    gathered = _all_gather(packed).reshape(N_DEV, -1)
    full, off = {}, 0
    for name, shape, pieces in layout:
        size = pieces * shape[0] * shape[1] * shape[2]
        blk = gathered[:, off:off + size]
        off += size
        if pieces == 3:
            blk = jnp.sum(blk.reshape(N_DEV, 3, *shape).astype(F32), axis=1)
        full[name] = _unshard(name, blk.reshape(N_DEV, *shape))
    return full


def _reduce_grads(full_grads, small_grads, shard_shapes):
    parts = [_to_shards(name, full_grads[name]) for name in SHARDED]
    small = jnp.concatenate([small_grads[name].reshape(-1) for name in SMALL])
    parts.append(jnp.broadcast_to(small[None, :], (N_DEV, small.shape[0])))
    reduced = _sum_slots(_scatter_blocks(_pack(parts, SUBLANES))).reshape(-1)
    out, off = {}, 0
    for name in SHARDED + SMALL:
        shape = shard_shapes[name]
        size = 1
        for s in shape:
            size *= s
        out[name] = reduced[off:off + size].reshape(shape)
        off += size
    return out


def _lane_row(values):
    flat = values.reshape(1, -1)
    return jnp.pad(flat, ((0, 0), (0, LANES - flat.shape[1])))


def _forward_layer(x, x16, w, t):
    proj = _mm(x16, w["w_main"], "nn", "proj_main")
    proj_ab = _mm(x16, w["w_ab"], "nn", "proj_ab")
    conv_out, qkvn = _qkv_conv_fwd(proj, w["conv_qkv"], t)
    gb = _gating_fwd(proj_ab, w["a_log"], w["dt_bias"], t)
    o_f, s_f, t_f = _gdn_fwd(qkvn, gb, 0, t)
    o_b, s_b, t_b = _gdn_fwd(qkvn, gb, 1, t)
    og = _gate_norm_fwd(o_f, o_b, proj, w["gdn_norm_w"], t)
    s = _sc_fwd(proj, w["conv_sc"], t)
    y_a = _mm(og, w["w_o_gdn"], "nn", "y_gdn")
    y_b = _mm(s, w["w_o_sc"], "nn", "y_sc")
    mixed = _mix_fwd(proj, y_a, y_b, t)
    r1 = _mm(mixed, w["w_out"], "nn", "out_proj")
    x1, x1_16 = _ln1_fwd(x, r1, w["ln1_g"], w["ln1_b"], t)
    hpre = _mm(x1_16, w["w_up"], "nn", "mlp_up")
    h = _relu2_fwd(hpre, w["b_up"], t)
    r2 = _mm(h, w["w_down"], "nn", "mlp_down")
    x2, x2_16 = _ln2_fwd(x1, r2, w["b_down"], w["ln2_g"], w["ln2_b"], t)
    saved = dict(x=x, x16=x16, proj=proj, proj_ab=proj_ab, conv_out=conv_out, qkvn=qkvn, gb=gb, o_f=o_f, o_b=o_b,
                 s_f=s_f, s_b=s_b, t_f=t_f, t_b=t_b, og=og, s=s, y_a=y_a, y_b=y_b, mixed=mixed, r1=r1, x1=x1,
                 x1_16=x1_16, hpre=hpre, h=h, r2=r2)
    return x2, x2_16, saved


def _backward_layer(dx2, w, a, t):
    (dx1_a, dr2), (db_down, dg2, db2) = _ln2_bwd(a["x1"], a["r2"], w["b_down"], w["ln2_g"], w["ln2_b"], dx2, t)
    dh = _mm(dr2, w["w_down"], "nt", "d_h")
    dw_down = _mm(a["h"], dr2, "tn", "dw_down")
    dhpre, db_up = _relu2_bwd(a["hpre"], w["b_up"], dh, t)
    dx1 = _mm(dhpre, w["w_up"], "nt", "d_x1", addends=[(1.0, dx1_a)])
    dw_up = _mm(a["x1_16"], dhpre, "tn", "dw_up")
    (dx_a, dr1), (dg1, db1) = _ln1_bwd(a["x"], a["r1"], w["ln1_g"], w["ln1_b"], dx1, t)
    dmixed = _mm(dr1, w["w_out"], "nt", "d_mixed")
    dw_out = _mm(a["mixed"], dr1, "tn", "dw_out")
    dgates, dy_a, dy_b = _mix_bwd(a["proj"], a["y_a"], a["y_b"], dmixed, t)
    dog = _mm(dy_a, w["w_o_gdn"], "nt", "d_og")
    dw_o_gdn = _mm(a["og"], dy_a, "tn", "dw_o_gdn")
    ds = _mm(dy_b, w["w_o_sc"], "nt", "d_s")
    dw_o_sc = _mm(a["s"], dy_b, "tn", "dw_o_sc")
    dsc, dconv_sc = _sc_bwd(a["proj"], w["conv_sc"], ds, t)
    do, dz, dnorm_w = _gate_norm_bwd(a["o_f"], a["o_b"], a["proj"], w["gdn_norm_w"], dog, t)
    dq_f, dgb_f = _gdn_bwd(a["qkvn"], a["gb"], a["s_f"], a["t_f"], do, 0, t)
    dq_b, dgb_b = _gdn_bwd(a["qkvn"], a["gb"], a["s_b"], a["t_b"], do, 1, t)
    dc = _post_conv_bwd(a["conv_out"], dq_f, dq_b, t)
    dqkv, dconv_qkv = _qkv_conv_bwd(a["proj"], w["conv_qkv"], dc, t)
    dab, da_log, ddt_bias = _gating_bwd(a["proj_ab"], w["a_log"], w["dt_bias"], dgb_f, dgb_b, t)

    pieces = [(dqkv, 0), (dz, 3 * D_MODEL), (dsc, 4 * D_MODEL), (dgates, 7 * D_MODEL)]
    dx = _mm(dab, w["w_ab"], "nt", "dx_ab", addends=[(1.0, dx_a)])
    dw_main = []
    for j, (piece, off) in enumerate(pieces):
        dx = _mm(piece, w["w_main"], "nt", f"dx_{j}", b_off=off, kdim=piece.shape[1], addends=[(1.0, dx)])
        dw_main.append(_mm(a["x16"], piece, "tn", f"dw_in_{j}"))
    dw_ab = _mm(a["x16"], dab, "tn", "dw_ab")
    dw_main = jnp.concatenate(dw_main, axis=1)
    dw_in = jnp.concatenate([dw_main[:, :QKVZ_COLS], dw_ab[:, :AB_COLS], dw_main[:, QKVZ_COLS:]], axis=1)
    grads = dict(w_in=dw_in, conv_qkv=dconv_qkv[:3], a_log=da_log[0, :2 * N_HEADS].reshape(2, N_HEADS),
                 dt_bias=ddt_bias[0, :2 * N_HEADS].reshape(2, N_HEADS), gdn_norm_w=dnorm_w[0], w_o_gdn=dw_o_gdn,
                 conv_sc=dconv_sc[:3], w_o_sc=dw_o_sc, w_out=dw_out, ln1_g=dg1[0], ln1_b=db1[0], w_up=dw_up,
                 b_up=db_up[0], w_down=dw_down, b_down=db_down[0], ln2_g=dg2[0], ln2_b=db2[0])
    return dx, grads


def kernel(x, w_in, conv_qkv, a_log, dt_bias, gdn_norm_w, w_o_gdn, conv_sc, w_o_sc, w_out, ln1_g, ln1_b, w_up, b_up, w_down, b_down, ln2_g, ln2_b, loss_target, m_w_in, m_conv_qkv, m_a_log, m_dt_bias, m_gdn_norm_w, m_w_o_gdn, m_conv_sc, m_w_o_sc, m_w_out, m_ln1_g, m_ln1_b, m_w_up, m_b_up, m_w_down, m_b_down, m_ln2_g, m_ln2_b, v_w_in, v_conv_qkv, v_a_log, v_dt_bias, v_gdn_norm_w, v_w_o_gdn, v_conv_sc, v_w_o_sc, v_w_out, v_ln1_g, v_ln1_b, v_w_up, v_b_up, v_w_down, v_b_down, v_ln2_g, v_ln2_b):
    weights = dict(w_in=w_in, conv_qkv=conv_qkv, a_log=a_log, dt_bias=dt_bias, gdn_norm_w=gdn_norm_w,
                   w_o_gdn=w_o_gdn, conv_sc=conv_sc, w_o_sc=w_o_sc, w_out=w_out, ln1_g=ln1_g, ln1_b=ln1_b,
                   w_up=w_up, b_up=b_up, w_down=w_down, b_down=b_down, ln2_g=ln2_g, ln2_b=ln2_b)
    m_in = dict(w_in=m_w_in, conv_qkv=m_conv_qkv, a_log=m_a_log, dt_bias=m_dt_bias, gdn_norm_w=m_gdn_norm_w,
                w_o_gdn=m_w_o_gdn, conv_sc=m_conv_sc, w_o_sc=m_w_o_sc, w_out=m_w_out, ln1_g=m_ln1_g, ln1_b=m_ln1_b,
                w_up=m_w_up, b_up=m_b_up, w_down=m_w_down, b_down=m_b_down, ln2_g=m_ln2_g, ln2_b=m_ln2_b)
    v_in = dict(w_in=v_w_in, conv_qkv=v_conv_qkv, a_log=v_a_log, dt_bias=v_dt_bias, gdn_norm_w=v_gdn_norm_w,
                w_o_gdn=v_w_o_gdn, conv_sc=v_conv_sc, w_o_sc=v_w_o_sc, w_out=v_w_out, ln1_g=v_ln1_g, ln1_b=v_ln1_b,
                w_up=v_w_up, b_up=v_b_up, w_down=v_w_down, b_down=v_b_down, ln2_g=v_ln2_g, ln2_b=v_ln2_b)
    t = x.shape[1]
    depth = w_in.shape[0]
    full = _gather_weights({name: weights[name] for name in MATRICES + CONVS})

    layers = []
    for l in range(depth):
        w_in_l = full["w_in"][l]
        layers.append(dict(
            w_main=jnp.concatenate([w_in_l[:, :QKVZ_COLS], w_in_l[:, QKVZ_COLS + AB_COLS:]], axis=1),
            w_ab=jnp.pad(w_in_l[:, QKVZ_COLS:QKVZ_COLS + AB_COLS], ((0, 0), (0, LANES - AB_COLS))),
            conv_qkv=jnp.pad(full["conv_qkv"][l], ((0, SUBLANES - 3), (0, 0))),
            conv_sc=jnp.pad(full["conv_sc"][l], ((0, SUBLANES - 3), (0, 0))),
            a_log=_lane_row(a_log[l]), dt_bias=_lane_row(dt_bias[l]), gdn_norm_w=gdn_norm_w[l][None, :],
            w_o_gdn=full["w_o_gdn"][l], w_o_sc=full["w_o_sc"][l], w_out=full["w_out"][l],
            ln1_g=ln1_g[l][None, :], ln1_b=ln1_b[l][None, :], w_up=full["w_up"][l], b_up=b_up[l][None, :],
            w_down=full["w_down"][l], b_down=b_down[l][None, :], ln2_g=ln2_g[l][None, :], ln2_b=ln2_b[l][None, :]))

    h = x.reshape(t, D_MODEL)
    h16 = h.astype(BF16)
    saved = []
    for l in range(depth):
        h, h16, acts = _forward_layer(h, h16, layers[l], t)
        saved.append(acts)
    dh, loss_local = _loss_stage(h, loss_target.reshape(t, D_MODEL), t)
    loss = lax.psum(loss_local, MESH_AXES)

    layer_grads = [None] * depth
    for l in reversed(range(depth)):
        dh, layer_grads[l] = _backward_layer(dh, layers[l], saved[l], t)
    stacked = {name: jnp.stack([g[name] for g in layer_grads]) for name in SHARDED + SMALL}
    shapes = {name: weights[name].shape for name in SHARDED + SMALL}
    grads = _reduce_grads({n: stacked[n] for n in SHARDED}, {n: stacked[n] for n in SMALL}, shapes)

    names = list(weights)
    updates = {n: _adamw(weights[n], grads[n], m_in[n], v_in[n], f"adamw_{n}") for n in names}
    return (loss, dh.reshape(x.shape), *[grads[n] for n in names], *[updates[n][0] for n in names],
            *[updates[n][1] for n in names], *[updates[n][2] for n in names])
```

```python
import functools

import jax
import jax.numpy as jnp
from jax import lax
from jax.experimental import pallas as pl
from jax.experimental.pallas import tpu as pltpu

F32 = jnp.float32
BF16 = jnp.bfloat16

D_MODEL = 1024
N_HEADS = 8
HEAD_DIM = 128
CHUNK = 64
D_FF = 4 * D_MODEL
DEPTH = 4
N_DEV = 8
LN_EPS = 1e-5
RMS_EPS = 1e-6
L2_EPS = 1e-6
ALPHA = (2 * DEPTH) ** 0.25
MAIN_COLS = 9 * D_MODEL
QKVZ_COLS = 4 * D_MODEL
AB_COLS = 4 * N_HEADS
W_IN_COLS = MAIN_COLS + AB_COLS
LANES = 128
SUBLANES = 8
VMEM_LIMIT = 48 * 1024 * 1024
MM_TILE = 1024
ROW_TILE = 256
WIDE_ROW_TILE = 128
ADAM = dict(lr=0.001, b1=0.9, b2=0.999, eps=1e-08, wd=0.01, step=10)
MESH_AXES = ("x", "y", "c")


def _params(sem=None):
    return pltpu.CompilerParams(dimension_semantics=sem, vmem_limit_bytes=VMEM_LIMIT)


_DIMS = {"nn": (1, 0), "nt": (1, 1), "tn": (0, 0)}


def _mm(a, b, mode, name, *, out_dtype=F32, b_off=0, kdim=None, addends=()):
    if mode == "nn":
        (m, k), n = a.shape, b.shape[1]
    elif mode == "nt":
        (m, k), n = a.shape, b.shape[0]
        k = kdim or k
    else:
        (k, m), n = a.shape, b.shape[1]
    tm, tn, tk = min(m, MM_TILE), min(n, MM_TILE), min(k, MM_TILE)
    assert m % tm == 0 and n % tn == 0 and k % tk == 0 and b_off % tk == 0
    nk = k // tk
    koff = b_off // tk
    ca, cb = _DIMS[mode]
    scales = tuple(s for s, _ in addends)
    na = len(addends)

    def body(a_ref, b_ref, *rest):
        add_refs, o_ref = rest[:na], rest[na]
        kk = pl.program_id(2)
        p = lax.dot_general(a_ref[...].astype(BF16), b_ref[...].astype(BF16), (((ca,), (cb,)), ((), ())),
                            preferred_element_type=F32)

        def finish(r):
            for s, ref in zip(scales, add_refs):
                r = r + s * ref[...].astype(F32)
            o_ref[...] = r.astype(o_ref.dtype)

        if nk == 1:
            finish(p)
        else:
            acc = rest[na + 1]

            @pl.when(kk == 0)
            def _():
                acc[...] = p

            @pl.when(kk > 0)
            def _():
                acc[...] += p

            @pl.when(kk == nk - 1)
            def _():
                finish(acc[...])

    if mode == "nn":
        a_spec = pl.BlockSpec((tm, tk), lambda i, j, kk: (i, kk))
        b_spec = pl.BlockSpec((tk, tn), lambda i, j, kk: (kk, j))
    elif mode == "nt":
        a_spec = pl.BlockSpec((tm, tk), lambda i, j, kk: (i, kk))
        b_spec = pl.BlockSpec((tn, tk), lambda i, j, kk: (j, kk + koff))
    else:
        a_spec = pl.BlockSpec((tk, tm), lambda i, j, kk: (kk, i))
        b_spec = pl.BlockSpec((tk, tn), lambda i, j, kk: (kk, j))
    o_spec = pl.BlockSpec((tm, tn), lambda i, j, kk: (i, j))
    return pl.pallas_call(
        body, name=name, grid=(m // tm, n // tn, nk),
        in_specs=[a_spec, b_spec] + [o_spec] * na, out_specs=o_spec,
        out_shape=jax.ShapeDtypeStruct((m, n), out_dtype),
        scratch_shapes=[pltpu.VMEM((tm, tn), F32)] if nk > 1 else [],
        compiler_params=_params(("parallel", "parallel", "arbitrary")),
    )(a, b, *[arr for _, arr in addends])


def _tile_call(name, body, t, tm, tiled, halo, params, outs, accs):
    tm = min(tm, t)
    assert t % tm == 0 and tm % SUBLANES == 0
    steps = t // tm
    hb = tm // SUBLANES
    nt, nh, npar, no = len(tiled), len(halo), len(params), len(outs)

    def kern(*refs):
        i = pl.program_id(0)
        t_refs = refs[:nt + nh]
        h_refs = refs[nt + nh:nt + 3 * nh]
        p_refs = refs[nt + 3 * nh:nt + 3 * nh + npar]
        o_refs = refs[nt + 3 * nh + npar:nt + 3 * nh + npar + no]
        a_refs = refs[nt + 3 * nh + npar + no:]
        tiles = [r[...] for r in t_refs]
        halos = []
        for j in range(nh):
            prev = h_refs[2 * j][SUBLANES - 1:SUBLANES, :].astype(F32)
            nxt = h_refs[2 * j + 1][0:1, :].astype(F32)
            halos.append((jnp.where(i > 0, prev, 0.0), jnp.where(i < steps - 1, nxt, 0.0)))
        o_vals, a_vals = body(tiles, halos, [r[...] for r in p_refs])
        for ref, val in zip(o_refs, o_vals):
            ref[...] = val.astype(ref.dtype)
        for ref, val in zip(a_refs, a_vals):
            @pl.when(i == 0)
            def _(ref=ref, val=val):
                ref[...] = val

            @pl.when(i > 0)
            def _(ref=ref, val=val):
                ref[...] += val

    in_specs, args = [], []
    for arr, nc, cb in list(tiled) + list(halo):
        in_specs.append(pl.BlockSpec((tm, nc), lambda i, cb=cb: (i, cb)))
        args.append(arr)
    last = t // SUBLANES - 1
    for arr, nc, cb in halo:
        in_specs.append(pl.BlockSpec((SUBLANES, nc), lambda i, cb=cb: (jnp.maximum(i * hb - 1, 0), cb)))
        in_specs.append(pl.BlockSpec((SUBLANES, nc), lambda i, cb=cb: (jnp.minimum((i + 1) * hb, last), cb)))
        args += [arr, arr]
    for arr in params:
        in_specs.append(pl.BlockSpec(arr.shape, lambda i: (0, 0)))
        args.append(arr)
    out_specs = [pl.BlockSpec((tm, nc), lambda i: (i, 0)) for nc, _ in outs]
    out_specs += [pl.BlockSpec(shape, lambda i: (0, 0)) for shape in accs]
    out_shape = [jax.ShapeDtypeStruct((t, nc), dt) for nc, dt in outs]
    out_shape += [jax.ShapeDtypeStruct(shape, F32) for shape in accs]
    res = pl.pallas_call(kern, name=name, grid=(steps,), in_specs=in_specs, out_specs=out_specs,
                         out_shape=out_shape, compiler_params=_params(("arbitrary",)))(*args)
    return res[:no], res[no:]


def _row_iota(x):
    return lax.broadcasted_iota(jnp.int32, x.shape, 0)


def _lane_iota(x):
    return lax.broadcasted_iota(jnp.int32, x.shape, 1)


def _shift_down(x, first_row):
    return jnp.where(_row_iota(x) == 0, first_row, pltpu.roll(x, 1, 0))


def _shift_up(x, last_row):
    n = x.shape[0]
    return jnp.where(_row_iota(x) == n - 1, last_row, pltpu.roll(x, n - 1, 0))


def _taps(w):
    return w[0:1, :], w[1:2, :], w[2:3, :]


def _tap_rows(d0, d1, d2, rows=SUBLANES):
    r = lax.broadcasted_iota(jnp.int32, (rows, d0.shape[1]), 0)
    return jnp.where(r == 0, d0, jnp.where(r == 1, d1, jnp.where(r == 2, d2, 0.0)))


def _colsum(x):
    return jnp.sum(x, axis=0, keepdims=True)


def _silu(x):
    return x * jax.nn.sigmoid(x)


def _softplus(x):
    return jnp.maximum(x, 0.0) + jnp.log(1.0 + jnp.exp(-jnp.abs(x)))


def _heads(x):
    return [x[:, h * HEAD_DIM:(h + 1) * HEAD_DIM] for h in range(x.shape[1] // HEAD_DIM)]


def _post_conv(c):
    blocks = _heads(_silu(c))
    out = []
    for j, blk in enumerate(blocks):
        if j < 2 * N_HEADS:
            blk = blk * lax.rsqrt(jnp.sum(blk * blk, axis=-1, keepdims=True) + L2_EPS)
        if j < N_HEADS:
            blk = blk * (HEAD_DIM ** -0.5)
        out.append(blk)
    return jnp.concatenate(out, axis=1)


def _gating(ab, a_log, dt_bias):
    lane = _lane_iota(ab)
    g = -jnp.exp(a_log) * _softplus(ab + dt_bias)
    return jnp.where(lane < 2 * N_HEADS, g, jnp.where(lane < AB_COLS, jax.nn.sigmoid(ab), 0.0))


def _gate_norm(o_f, o_b, z, norm_w):
    out = []
    for oh, zh in zip(_heads(o_f + o_b), _heads(z)):
        out.append(oh * lax.rsqrt(jnp.mean(oh * oh, axis=-1, keepdims=True) + RMS_EPS) * norm_w * _silu(zh))
    return jnp.concatenate(out, axis=1)


def _mix(gate_a, gate_b, y_a, y_b):
    return jax.nn.sigmoid(gate_a) * y_a + jax.nn.sigmoid(gate_b) * y_b


def _layer_norm(u, g, b):
    mu = jnp.mean(u, axis=-1, keepdims=True)
    var = jnp.mean(jnp.square(u - mu), axis=-1, keepdims=True)
    return (u - mu) * lax.rsqrt(var + LN_EPS) * g + b


def _ln1(x, r, g, b):
    return _layer_norm(ALPHA * x + r, g, b)


def _ln2(x, r, bias, g, b):
    return _layer_norm(ALPHA * x + r + bias, g, b)


def _relu2(hpre, bias):
    return jnp.square(jnp.maximum(hpre + bias, 0.0))


def _qkv_conv_fwd(proj, conv_w, t):
    def body(tiles, halos, params):
        (x,), ((xp, xn),), (w,) = tiles, halos, params
        w0, w1, w2 = _taps(w)
        c = w0 * _shift_down(x, xp) + w1 * x + w2 * _shift_up(x, xn)
        return [c, _post_conv(c)], []

    (c, qkvn), _ = _tile_call("qkv_conv_fwd", body, t, WIDE_ROW_TILE, [], [(proj, 3 * D_MODEL, 0)], [conv_w],
                              [(3 * D_MODEL, F32), (3 * D_MODEL, F32)], [])
    return c, qkvn


def _gating_fwd(proj_ab, a_log, dt_bias, t):
    def body(tiles, halos, params):
        return [_gating(tiles[0], params[0], params[1])], []

    (gb,), _ = _tile_call("gating_fwd", body, t, ROW_TILE, [(proj_ab, LANES, 0)], [], [a_log, dt_bias],
                          [(LANES, F32)], [])
    return gb


def _gate_norm_fwd(o_f, o_b, proj, norm_w, t):
    def body(tiles, halos, params):
        return [_gate_norm(tiles[0], tiles[1], tiles[2], params[0])], []

    (og,), _ = _tile_call("gate_norm_fwd", body, t, ROW_TILE,
                          [(o_f, D_MODEL, 0), (o_b, D_MODEL, 0), (proj, D_MODEL, 3)], [], [norm_w],
                          [(D_MODEL, BF16)], [])
    return og


def _sc_fwd(proj, conv_w, t):
    def body(tiles, halos, params):
        (sb,), ((cp, cn), (xp, xn)), (w,) = tiles[:1], halos, params
        sc, sx = tiles[1], tiles[2]
        w0, w1, w2 = _taps(w)
        u = sc * sx
        return [sb * (w0 * _shift_down(u, cp * xp) + w1 * u + w2 * _shift_up(u, cn * xn))], []

    (s,), _ = _tile_call("sc_fwd", body, t, ROW_TILE, [(proj, D_MODEL, 4)],
                         [(proj, D_MODEL, 5), (proj, D_MODEL, 6)], [conv_w], [(D_MODEL, BF16)], [])
    return s


def _mix_fwd(proj, y_a, y_b, t):
    def body(tiles, halos, params):
        return [_mix(*tiles)], []

    (mixed,), _ = _tile_call("mix_fwd", body, t, ROW_TILE,
                             [(proj, D_MODEL, 7), (proj, D_MODEL, 8), (y_a, D_MODEL, 0), (y_b, D_MODEL, 0)], [], [],
                             [(D_MODEL, BF16)], [])
    return mixed


def _ln1_fwd(x, r, g, b, t):
    def body(tiles, halos, params):
        y = _ln1(tiles[0], tiles[1], params[0], params[1])
        return [y, y], []

    (y, y16), _ = _tile_call("ln1_fwd", body, t, ROW_TILE, [(x, D_MODEL, 0), (r, D_MODEL, 0)], [], [g, b],
                             [(D_MODEL, F32), (D_MODEL, BF16)], [])
    return y, y16


def _ln2_fwd(x, r, bias, g, b, t):
    def body(tiles, halos, params):
        y = _ln2(tiles[0], tiles[1], params[0], params[1], params[2])
        return [y, y], []

    (y, y16), _ = _tile_call("ln2_fwd", body, t, ROW_TILE, [(x, D_MODEL, 0), (r, D_MODEL, 0)], [], [bias, g, b],
                             [(D_MODEL, F32), (D_MODEL, BF16)], [])
    return y, y16


def _relu2_fwd(hpre, bias, t):
    def body(tiles, halos, params):
        return [_relu2(tiles[0], params[0])], []

    (h,), _ = _tile_call("relu2_fwd", body, t, ROW_TILE, [(hpre, D_FF, 0)], [], [bias], [(D_FF, BF16)], [])
    return h


def _loss_stage(y, target, t):
    def body(tiles, halos, params):
        d = tiles[0] - tiles[1]
        part = 0.5 * jnp.sum(jnp.mean(d * d, axis=-1, keepdims=True), axis=0, keepdims=True)
        return [d * (1.0 / D_MODEL)], [jnp.broadcast_to(part, (1, LANES))]

    (dy,), (loss,) = _tile_call("loss", body, t, ROW_TILE, [(y, D_MODEL, 0), (target, D_MODEL, 0)], [], [],
                                [(D_MODEL, F32)], [(1, LANES)])
    return dy, loss[0, 0]


def _ln2_bwd(x, r, bias, g, b, dy, t):
    def body(tiles, halos, params):
        _, vjp = jax.vjp(_ln2, tiles[0], tiles[1], params[0], params[1], params[2])
        dx, dr, dbias, dg, db = vjp(tiles[2])
        return [dx, dr], [dbias, dg, db]

    return _tile_call("ln2_bwd", body, t, ROW_TILE, [(x, D_MODEL, 0), (r, D_MODEL, 0), (dy, D_MODEL, 0)], [],
                      [bias, g, b], [(D_MODEL, F32), (D_MODEL, BF16)], [(1, D_MODEL)] * 3)


def _ln1_bwd(x, r, g, b, dy, t):
    def body(tiles, halos, params):
        _, vjp = jax.vjp(_ln1, tiles[0], tiles[1], params[0], params[1])
        dx, dr, dg, db = vjp(tiles[2])
        return [dx, dr], [dg, db]

    return _tile_call("ln1_bwd", body, t, ROW_TILE, [(x, D_MODEL, 0), (r, D_MODEL, 0), (dy, D_MODEL, 0)], [],
                      [g, b], [(D_MODEL, F32), (D_MODEL, BF16)], [(1, D_MODEL)] * 2)


def _relu2_bwd(hpre, bias, dh, t):
    def body(tiles, halos, params):
        _, vjp = jax.vjp(_relu2, tiles[0], params[0])
        dhpre, dbias = vjp(tiles[1])
        return [dhpre], [dbias]

    (dhpre,), (dbias,) = _tile_call("relu2_bwd", body, t, WIDE_ROW_TILE, [(hpre, D_FF, 0), (dh, D_FF, 0)], [],
                                    [bias], [(D_FF, BF16)], [(1, D_FF)])
    return dhpre, dbias


def _mix_bwd(proj, y_a, y_b, dmixed, t):
    def body(tiles, halos, params):
        _, vjp = jax.vjp(_mix, *tiles[:4])
        dga, dgb, dya, dyb = vjp(tiles[4])
        return [jnp.concatenate([dga, dgb], axis=1), dya, dyb], []

    (dgates, dya, dyb), _ = _tile_call(
        "mix_bwd", body, t, ROW_TILE,
        [(proj, D_MODEL, 7), (proj, D_MODEL, 8), (y_a, D_MODEL, 0), (y_b, D_MODEL, 0), (dmixed, D_MODEL, 0)], [], [],
        [(2 * D_MODEL, BF16), (D_MODEL, BF16), (D_MODEL, BF16)], [])
    return dgates, dya, dyb


def _sc_bwd(proj, conv_w, ds, t):
    def body(tiles, halos, params):
        ds_, sb, sc, sx = tiles
        (dsp, dsn), (sbp, sbn), (scp, scn), (sxp, sxn) = halos
        w0, w1, w2 = _taps(params[0])
        u = sc * sx
        u_prev, u_next = _shift_down(u, scp * sxp), _shift_up(u, scn * sxn)
        dconv = ds_ * sb
        du = w0 * _shift_up(dconv, dsn * sbn) + w1 * dconv + w2 * _shift_down(dconv, dsp * sbp)
        dsb = ds_ * (w0 * u_prev + w1 * u + w2 * u_next)
        dw = _tap_rows(_colsum(dconv * u_prev), _colsum(dconv * u), _colsum(dconv * u_next))
        return [jnp.concatenate([dsb, du * sx, du * sc], axis=1)], [dw]

    (dsc,), (dw,) = _tile_call("sc_bwd", body, t, ROW_TILE, [],
                               [(ds, D_MODEL, 0), (proj, D_MODEL, 4), (proj, D_MODEL, 5), (proj, D_MODEL, 6)],
                               [conv_w], [(3 * D_MODEL, BF16)], [(SUBLANES, D_MODEL)])
    return dsc, dw


def _gate_norm_bwd(o_f, o_b, proj, norm_w, dog, t):
    def body(tiles, halos, params):
        _, vjp = jax.vjp(_gate_norm, tiles[0], tiles[1], tiles[2], params[0])
        do, _, dz, dnw = vjp(tiles[3])
        return [do, dz], [dnw]

    (do, dz), (dnw,) = _tile_call(
        "gate_norm_bwd", body, t, ROW_TILE,
        [(o_f, D_MODEL, 0), (o_b, D_MODEL, 0), (proj, D_MODEL, 3), (dog, D_MODEL, 0)], [], [norm_w],
        [(D_MODEL, F32), (D_MODEL, BF16)], [(1, HEAD_DIM)])
    return do, dz, dnw


def _post_conv_bwd(c, dq_f, dq_b, t):
    def body(tiles, halos, params):
        _, vjp = jax.vjp(_post_conv, tiles[0])
        return [vjp(tiles[1] + tiles[2])[0]], []

    (dc,), _ = _tile_call("post_conv_bwd", body, t, WIDE_ROW_TILE,
                          [(c, 3 * D_MODEL, 0), (dq_f, 3 * D_MODEL, 0), (dq_b, 3 * D_MODEL, 0)], [], [],
                          [(3 * D_MODEL, F32)], [])
    return dc


def _qkv_conv_bwd(proj, conv_w, dc, t):
    def body(tiles, halos, params):
        (dcp, dcn), (xp, xn) = halos
        d, x = tiles
        w0, w1, w2 = _taps(params[0])
        dx = w0 * _shift_up(d, dcn) + w1 * d + w2 * _shift_down(d, dcp)
        dw = _tap_rows(_colsum(d * _shift_down(x, xp)), _colsum(d * x), _colsum(d * _shift_up(x, xn)))
        return [dx], [dw]

    (dqkv,), (dw,) = _tile_call("qkv_conv_bwd", body, t, WIDE_ROW_TILE, [],
                                [(dc, 3 * D_MODEL, 0), (proj, 3 * D_MODEL, 0)], [conv_w],
                                [(3 * D_MODEL, BF16)], [(SUBLANES, 3 * D_MODEL)])
    return dqkv, dw


def _gating_bwd(proj_ab, a_log, dt_bias, dgb_f, dgb_b, t):
    def body(tiles, halos, params):
        _, vjp = jax.vjp(_gating, tiles[0], params[0], params[1])
        dab, dal, ddt = vjp(tiles[1] + tiles[2])
        return [dab], [dal, ddt]

    (dab,), (dal, ddt) = _tile_call("gating_bwd", body, t, ROW_TILE,
                                    [(proj_ab, LANES, 0), (dgb_f, LANES, 0), (dgb_b, LANES, 0)], [],
                                    [a_log, dt_bias], [(LANES, BF16)], [(1, LANES)] * 2)
    return dab, dal, ddt


@functools.partial(jax.custom_vjp, nondiff_argnums=(2, 3))
def _dot(a, b, ca, cb):
    return lax.dot_general(a.astype(BF16), b.astype(BF16), (((ca,), (cb,)), ((), ())), preferred_element_type=F32)


def _dot_fwd(a, b, ca, cb):
    return _dot(a, b, ca, cb), (a, b)


def _dot_bwd(ca, cb, res, ct):
    a, b = res
    fa, fb = 1 - ca, 1 - cb
    da = _dot(ct, b, 1, fb) if ca == 1 else _dot(b, ct, fb, 1)
    db = _dot(a, ct, fa, 0) if cb == 0 else _dot(ct, a, 0, fa)
    return da, db


_dot.defvjp(_dot_fwd, _dot_bwd)


def _dot3(a, b, ca, cb):
    ah, bh = a.astype(BF16).astype(F32), b.astype(BF16).astype(F32)
    return _dot(ah, bh, ca, cb) + _dot(a - ah, bh, ca, cb) + _dot(ah, b - bh, ca, cb)


def _split3(x):
    hi = x.astype(BF16)
    r1 = x - hi.astype(F32)
    mid = r1.astype(BF16)
    return hi, mid, (r1 - mid.astype(F32)).astype(BF16)


def _dot_exact(a, b, ca, cb, exact):
    dims = (((ca,), (cb,)), ((), ()))
    if exact == 0:
        return sum(lax.dot_general(a.astype(BF16), p, dims, preferred_element_type=F32) for p in _split3(b))
    return sum(lax.dot_general(p, b.astype(BF16), dims, preferred_element_type=F32) for p in _split3(a))


def _tri_masks(n, rev):
    r = lax.broadcasted_iota(jnp.int32, (n, n), 0)
    c = lax.broadcasted_iota(jnp.int32, (n, n), 1)
    return ((c >= r), (c > r)) if rev else ((c <= r), (c < r))


@functools.partial(jax.custom_vjp, nondiff_argnums=(1,))
def _cumsum_rows(g, rev):
    incl, _ = _tri_masks(g.shape[0], rev)
    return _dot_exact(incl.astype(F32), g, 1, 0, 0)


_cumsum_rows.defvjp(lambda g, rev: (_cumsum_rows(g, rev), None),
                    lambda rev, _, ct: (_cumsum_rows(ct, not rev),))


def _eye(n):
    return (lax.broadcasted_iota(jnp.int32, (n, n), 0) == lax.broadcasted_iota(jnp.int32, (n, n), 1)).astype(F32)


@jax.custom_vjp
def _to_rows(x):
    return _dot_exact(_eye(x.shape[1]), x, 1, 1, 0)


@jax.custom_vjp
def _to_cols(y):
    return _dot_exact(y, _eye(y.shape[0]), 0, 0, 1)


_to_rows.defvjp(lambda x: (_to_rows(x), None), lambda _, ct: (_to_cols(ct),))
_to_cols.defvjp(lambda y: (_to_cols(y), None), lambda _, ct: (_to_rows(ct),))


def _pick_col(arr, idx):
    return jnp.sum(jnp.where(_lane_iota(arr) == idx, arr, 0.0), axis=1, keepdims=True)


def _pick_row(arr, idx):
    return jnp.sum(jnp.where(_row_iota(arr) == idx, arr, 0.0), axis=0, keepdims=True)


def _chunk_gates(gb, direction, rev):
    n = gb.shape[0]
    incl, strict = _tri_masks(n, rev)
    gc = _cumsum_rows(gb, rev)
    gc_rows = _to_rows(gc)
    lanes = [direction * N_HEADS + h for h in range(N_HEADS)]
    cols = [_pick_col(gc, ln) for ln in lanes]
    rows = [_pick_row(gc_rows, ln) for ln in lanes]
    betas = [_pick_col(gb, 2 * N_HEADS + ln) for ln in lanes]
    decays = [jnp.where(incl, jnp.exp(jnp.where(incl, c - r, 0.0)), 0.0) for c, r in zip(cols, rows)]
    return cols, betas, decays, strict


def _chunk_lmat(k, gb, direction, rev):
    _, betas, decays, strict = _chunk_gates(gb, direction, rev)
    ks = _heads(k)
    kk = [_dot(kh * b, kh, 1, 1) for kh, b in zip(ks, betas)]
    return tuple(jnp.where(strict, x * d, 0.0) for x, d in zip(kk, decays))


def _tri_inverse(lmats):
    n = lmats[0].shape[0]
    eye = _eye(n)
    powers = [-lm for lm in lmats]
    invs = [eye + p for p in powers]
    span = 2
    while span < n:
        powers = [_dot(p, p, 1, 0) for p in powers]
        steps = [_dot(p, i, 1, 0) for p, i in zip(powers, invs)]
        invs = [i + s for i, s in zip(invs, steps)]
        span *= 2
    return tuple(invs)


def _chunk_out(q, k, v, gb, tmats, states, direction, rev):
    n = gb.shape[0]
    last_idx = 0 if rev else n - 1
    cols, betas, decays, _ = _chunk_gates(gb, direction, rev)
    qs, ks, vs = _heads(q), _heads(k), _heads(v)
    lasts = [_pick_row(c, last_idx) for c in cols]
    us = [_dot(tm, vh * b, 1, 0) for tm, vh, b in zip(tmats, vs, betas)]
    ws = [_dot(tm, kh * b * jnp.exp(c), 1, 0) for tm, kh, b, c in zip(tmats, ks, betas, cols)]
    attns = [_dot(qh, kh, 1, 1) * d for qh, kh, d in zip(qs, ks, decays)]
    wss = [_dot(w, st, 1, 0) for w, st in zip(ws, states)]
    v_news = [u - x for u, x in zip(us, wss)]
    inter = [_dot(qh * jnp.exp(c), st, 1, 0) for qh, c, st in zip(qs, cols, states)]
    intra = [_dot(a, vn, 1, 0) for a, vn in zip(attns, v_news)]
    adds = [_dot(kh * jnp.exp(l - c), vn, 0, 0) for kh, l, c, vn in zip(ks, lasts, cols, v_news)]
    new_states = tuple(st * jnp.exp(l) + a for st, l, a in zip(states, lasts, adds))
    return jnp.concatenate([x + y for x, y in zip(inter, intra)], axis=1), new_states


def _chunk_index(n, backwards):
    return (lambda i: n - 1 - i) if backwards else (lambda i: i)


def _gdn_fwd(qkvn, gb, direction, t):
    n = t // CHUNK
    rev = direction == 1
    idx = _chunk_index(n, rev)

    def body(q_ref, k_ref, v_ref, gb_ref, o_ref, s_ref, t_ref, state):
        @pl.when(pl.program_id(0) == 0)
        def _():
            state[...] = jnp.zeros_like(state)

        k, gbv = k_ref[...], gb_ref[...]
        tmats = _tri_inverse(_chunk_lmat(k, gbv, direction, rev))
        states = tuple(state[h] for h in range(N_HEADS))
        o, new_states = _chunk_out(q_ref[...], k, v_ref[...], gbv, tmats, states, direction, rev)
        o_ref[...] = o
        for h in range(N_HEADS):
            s_ref[0, h] = states[h]
            t_ref[0, h] = tmats[h]
            state[h] = new_states[h]

    qkv_specs = [pl.BlockSpec((CHUNK, D_MODEL), lambda i, p=p: (idx(i), p)) for p in range(3)]
    return pl.pallas_call(
        body, name=f"gdn_fwd_{direction}", grid=(n,),
        in_specs=qkv_specs + [pl.BlockSpec((CHUNK, LANES), lambda i: (idx(i), 0))],
        out_specs=[pl.BlockSpec((CHUNK, D_MODEL), lambda i: (idx(i), 0)),
                   pl.BlockSpec((1, N_HEADS, HEAD_DIM, HEAD_DIM), lambda i: (idx(i), 0, 0, 0)),
                   pl.BlockSpec((1, N_HEADS, CHUNK, CHUNK), lambda i: (idx(i), 0, 0, 0))],
        out_shape=[jax.ShapeDtypeStruct((t, D_MODEL), F32),
                   jax.ShapeDtypeStruct((n, N_HEADS, HEAD_DIM, HEAD_DIM), F32),
                   jax.ShapeDtypeStruct((n, N_HEADS, CHUNK, CHUNK), F32)],
        scratch_shapes=[pltpu.VMEM((N_HEADS, HEAD_DIM, HEAD_DIM), F32)],
        compiler_params=_params(("arbitrary",)),
    )(qkvn, qkvn, qkvn, gb)


def _gdn_bwd(qkvn, gb, s_saved, t_saved, do, direction, t):
    n = t // CHUNK
    rev = direction == 1
    idx = _chunk_index(n, not rev)

    def body(q_ref, k_ref, v_ref, gb_ref, s_ref, t_ref, do_ref, dqkv_ref, dgb_ref, dstate):
        @pl.when(pl.program_id(0) == 0)
        def _():
            dstate[...] = jnp.zeros_like(dstate)

        q, k, v, gbv = q_ref[...], k_ref[...], v_ref[...], gb_ref[...]
        tmats = tuple(t_ref[0, h] for h in range(N_HEADS))
        states = tuple(s_ref[0, h] for h in range(N_HEADS))
        _, out_vjp = jax.vjp(lambda *a: _chunk_out(*a, direction, rev), q, k, v, gbv, tmats, states)
        dq, dk, dv, dgb, dtm, dst = out_vjp((do_ref[...], tuple(dstate[h] for h in range(N_HEADS))))
        dlm = tuple(-_dot3(_dot3(tm, d, 0, 0), tm, 1, 1) for tm, d in zip(tmats, dtm))
        _, lmat_vjp = jax.vjp(lambda kk, gg: _chunk_lmat(kk, gg, direction, rev), k, gbv)
        dk2, dgb2 = lmat_vjp(dlm)
        dqkv_ref[...] = jnp.concatenate([dq, dk + dk2, dv], axis=1)
        dgb_ref[...] = dgb + dgb2
        for h in range(N_HEADS):
            dstate[h] = dst[h]

    qkv_specs = [pl.BlockSpec((CHUNK, D_MODEL), lambda i, p=p: (idx(i), p)) for p in range(3)]
    return pl.pallas_call(
        body, name=f"gdn_bwd_{direction}", grid=(n,),
        in_specs=qkv_specs + [pl.BlockSpec((CHUNK, LANES), lambda i: (idx(i), 0)),
                              pl.BlockSpec((1, N_HEADS, HEAD_DIM, HEAD_DIM), lambda i: (idx(i), 0, 0, 0)),
                              pl.BlockSpec((1, N_HEADS, CHUNK, CHUNK), lambda i: (idx(i), 0, 0, 0)),
                              pl.BlockSpec((CHUNK, D_MODEL), lambda i: (idx(i), 0))],
        out_specs=[pl.BlockSpec((CHUNK, 3 * D_MODEL), lambda i: (idx(i), 0)),
                   pl.BlockSpec((CHUNK, LANES), lambda i: (idx(i), 0))],
        out_shape=[jax.ShapeDtypeStruct((t, 3 * D_MODEL), F32), jax.ShapeDtypeStruct((t, LANES), F32)],
        scratch_shapes=[pltpu.VMEM((N_HEADS, HEAD_DIM, HEAD_DIM), F32)],
        compiler_params=_params(("arbitrary",)),
    )(qkvn, qkvn, qkvn, gb, s_saved, t_saved, do)


def _mesh_pos():
    return lax.axis_index("x"), lax.axis_index("y"), lax.axis_index("c")


def _all_gather(shards):
    na = len(shards)

    def body(*refs):
        x_refs, out_refs = refs[:na], refs[na:2 * na]
        send_sems, recv_sems, local_sems = refs[2 * na:]
        x, y, c = _mesh_pos()
        me, sibling = (x, y, c), (x, y, 1 - c)
        chips = [(1 - x, y), (x, 1 - y), (1 - x, 1 - y)]

        def block(a, px, py, pc):
            return out_refs[a].at[4 * px + 2 * py + pc]

        def copy(a, k, blk, to, src=None):
            return pltpu.make_async_remote_copy(
                src_ref=block(a, *blk) if src is None else src, dst_ref=block(a, *blk),
                send_sem=send_sems.at[a, k], recv_sem=recv_sems.at[a, k],
                device_id=to, device_id_type=pl.DeviceIdType.MESH)

        mine = [pltpu.make_async_copy(x_refs[a], block(a, *me), local_sems.at[a]) for a in range(na)]
        first, passed = [], []
        for a in range(na):
            mine[a].start()
            first.append(copy(a, 0, me, sibling, src=x_refs[a]))
            first += [copy(a, 1 + j, me, (*chip, c), src=x_refs[a]) for j, chip in enumerate(chips)]
        for cp in first:
            cp.start()
        for j, chip in enumerate(chips):
            for a in range(na):
                copy(a, 1 + j, (*chip, c), me).wait_recv()
                passed.append(copy(a, 4 + j, (*chip, c), sibling))
                passed[-1].start()
        for a in range(na):
            copy(a, 0, sibling, me).wait_recv()
            for j, chip in enumerate(chips):
                copy(a, 4 + j, (*chip, 1 - c), me).wait_recv()
        for cp in first + passed:
            cp.wait_send()
        for cp in mine:
            cp.wait()

    any_spec = pl.BlockSpec(memory_space=pl.ANY)
    return pl.pallas_call(
        body, name="weights_all_gather",
        out_shape=[jax.ShapeDtypeStruct((N_DEV,) + s.shape, s.dtype) for s in shards],
        in_specs=[any_spec] * na, out_specs=[any_spec] * na,
        scratch_shapes=[pltpu.SemaphoreType.DMA((na, N_DEV - 1)), pltpu.SemaphoreType.DMA((na, N_DEV - 1)),
                        pltpu.SemaphoreType.DMA((na,))],
    )(*shards)


def _scatter_blocks(blocks):
    na = len(blocks)

    def body(*refs):
        g_refs, land_refs = refs[:na], refs[na:2 * na]
        send_sems, recv_sems, local_sems = refs[2 * na:]
        x, y, c = _mesh_pos()
        mine = 4 * x + 2 * y + c
        local = [pltpu.make_async_copy(g_refs[a].at[mine], land_refs[a].at[mine], local_sems.at[a])
                 for a in range(na)]
        for cp in local:
            cp.start()
        sends, recvs = [], []
        for k in range(1, N_DEV):
            px = 1 - x if k & 4 else x
            py = 1 - y if k & 2 else y
            pc = 1 - c if k & 1 else c
            peer = 4 * px + 2 * py + pc
            for a in range(na):
                sends.append(pltpu.make_async_remote_copy(
                    src_ref=g_refs[a].at[peer], dst_ref=land_refs[a].at[mine],
                    send_sem=send_sems.at[a, k - 1], recv_sem=recv_sems.at[a, k - 1],
                    device_id=(px, py, pc), device_id_type=pl.DeviceIdType.MESH))
                recvs.append(pltpu.make_async_remote_copy(
                    src_ref=g_refs[a].at[mine], dst_ref=land_refs[a].at[peer],
                    send_sem=send_sems.at[a, k - 1], recv_sem=recv_sems.at[a, k - 1],
                    device_id=(x, y, c), device_id_type=pl.DeviceIdType.MESH))
        for cp in sends:
            cp.start()
        for cp in recvs:
            cp.wait_recv()
        for cp in sends:
            cp.wait_send()
        for cp in local:
            cp.wait()

    any_spec = pl.BlockSpec(memory_space=pl.ANY)
    return pl.pallas_call(
        body, name="grads_scatter",
        out_shape=[jax.ShapeDtypeStruct(b.shape, b.dtype) for b in blocks],
        in_specs=[any_spec] * na, out_specs=[any_spec] * na,
        scratch_shapes=[pltpu.SemaphoreType.DMA((na, N_DEV - 1)), pltpu.SemaphoreType.DMA((na, N_DEV - 1)),
                        pltpu.SemaphoreType.DMA((na,))],
    )(*blocks)


def _sum_slots(land, name):
    _, rows, cols = land.shape
    tr = _row_tile(rows, cols * 4 * N_DEV, budget=4 << 20)

    def body(*refs):
        acc = refs[0][0]
        for ref in refs[1:N_DEV]:
            acc = acc + ref[0]
        refs[N_DEV][...] = acc

    return pl.pallas_call(
        body, name=name, grid=(rows // tr,),
        in_specs=[pl.BlockSpec((1, tr, cols), lambda i, s=s: (s, i, 0)) for s in range(N_DEV)],
        out_specs=pl.BlockSpec((tr, cols), lambda i: (i, 0)),
        out_shape=jax.ShapeDtypeStruct((rows, cols), land.dtype),
        compiler_params=_params(("parallel",)),
    )(*([land] * N_DEV))


def _row_tile(rows, row_bytes, budget=1 << 20):
    if rows * row_bytes <= budget or rows % SUBLANES:
        return rows
    best = SUBLANES
    for tr in range(SUBLANES, rows + 1, SUBLANES):
        if rows % tr == 0 and tr * row_bytes <= budget:
            best = tr
    return best


def _adamw(w, g, m, v, name):
    shape = w.shape
    cols = shape[-1]
    rows = w.size // cols
    tr = _row_tile(rows, cols * 4)
    b1, b2 = ADAM["b1"], ADAM["b2"]

    def body(w_ref, g_ref, m_ref, v_ref, d_ref, nm_ref, nv_ref):
        gv = g_ref[...]
        nm = b1 * m_ref[...] + (1.0 - b1) * gv
        nv = b2 * v_ref[...] + (1.0 - b2) * jnp.square(gv)
        m_hat = nm / (1.0 - b1 ** ADAM["step"])
        v_hat = nv / (1.0 - b2 ** ADAM["step"])
        d_ref[...] = -ADAM["lr"] * (m_hat / (jnp.sqrt(v_hat) + ADAM["eps"]) + ADAM["wd"] * w_ref[...])
        nm_ref[...] = nm
        nv_ref[...] = nv

    spec = pl.BlockSpec((tr, cols), lambda i: (i, 0))
    outs = pl.pallas_call(
        body, name=name, grid=(rows // tr,), in_specs=[spec] * 4, out_specs=[spec] * 3,
        out_shape=[jax.ShapeDtypeStruct((rows, cols), F32)] * 3, compiler_params=_params(("parallel",)),
    )(*[a.reshape(rows, cols) for a in (w, g, m, v)])
    return [o.reshape(shape) for o in outs]


MATRICES = ("w_in", "w_o_gdn", "w_o_sc", "w_out", "w_up", "w_down")
CONVS = ("conv_qkv", "conv_sc")
SHARDED = ("w_in", "conv_qkv", "w_o_gdn", "conv_sc", "w_o_sc", "w_out", "w_up", "w_down")
SMALL = ("a_log", "dt_bias", "gdn_norm_w", "ln1_g", "ln1_b", "b_up", "b_down", "ln2_g", "ln2_b")
COLUMN_SHARDED = ("w_in", "conv_qkv", "conv_sc", "w_up")
PACK_COLS = 1024


def _pack(parts, row_multiple):
    flat = jnp.concatenate(parts, axis=-1)
    unit = PACK_COLS * row_multiple
    pad = -flat.shape[-1] % unit
    flat = jnp.pad(flat, [(0, 0)] * (flat.ndim - 1) + [(0, pad)])
    return flat.reshape(flat.shape[:-1] + (flat.shape[-1] // PACK_COLS, PACK_COLS))


def _unshard(name, blocks):
    _, l, r, c = blocks.shape
    if name in COLUMN_SHARDED:
        return blocks.transpose(1, 2, 0, 3).reshape(l, r, N_DEV * c)
    return blocks.transpose(1, 0, 2, 3).reshape(l, N_DEV * r, c)


def _to_shards(name, full):
    l, r, c = full.shape
    if name in COLUMN_SHARDED:
        return full.reshape(l, r, N_DEV, c // N_DEV).transpose(2, 0, 1, 3).reshape(N_DEV, -1)
    return full.reshape(l, N_DEV, r // N_DEV, c).transpose(1, 0, 2, 3).reshape(N_DEV, -1)


def _gather_weights(shards):
    parts, layout = [], []
    for name in MATRICES[1:]:
        parts.append(shards[name].astype(BF16).reshape(-1))
        layout.append((name, shards[name].shape, 1))
    for name in CONVS:
        parts.append(jnp.stack(_split3(shards[name])).reshape(-1))
        layout.append((name, shards[name].shape, 3))
    w_in = shards["w_in"]
    w_in_all, rest_all = _all_gather([w_in.astype(BF16).reshape(-1, w_in.shape[-1]), _pack(parts, 16)])
    full = {"w_in": _unshard("w_in", w_in_all.reshape(N_DEV, *w_in.shape))}
    rest_all, off = rest_all.reshape(N_DEV, -1), 0
    for name, shape, pieces in layout:
        size = pieces * shape[0] * shape[1] * shape[2]
        blk = rest_all[:, off:off + size]
        off += size
        if pieces == 3:
            blk = jnp.sum(blk.reshape(N_DEV, 3, *shape).astype(F32), axis=1)
        full[name] = _unshard(name, blk.reshape(N_DEV, *shape))
    return full


def _unpack(flat, names, shapes):
    out, off = {}, 0
    for name in names:
        size = 1
        for s in shapes[name]:
            size *= s
        out[name] = flat[off:off + size].reshape(shapes[name])
        off += size
    return out


def _reduce_grads(full_grads, small_grads, shard_shapes):
    l, r, c = full_grads["w_in"].shape
    w_in = full_grads["w_in"].reshape(l, r, N_DEV, c // N_DEV).transpose(2, 0, 1, 3).reshape(N_DEV, l * r, c // N_DEV)
    rest = _pack([_to_shards(name, full_grads[name]) for name in SHARDED[1:]], SUBLANES)
    small = jnp.concatenate([small_grads[name].reshape(-1) for name in SMALL])
    small = _pack([jnp.broadcast_to(small[None, :], (N_DEV, small.shape[0]))], SUBLANES)
    landed = _scatter_blocks([w_in, rest, small])
    out = {"w_in": _sum_slots(landed[0], "grads_sum_w_in").reshape(shard_shapes["w_in"])}
    out.update(_unpack(_sum_slots(landed[1], "grads_sum_rest").reshape(-1), SHARDED[1:], shard_shapes))
    out.update(_unpack(_sum_slots(landed[2], "grads_sum_small").reshape(-1), SMALL, shard_shapes))
    return out


def _lane_row(values):
    flat = values.reshape(1, -1)
    return jnp.pad(flat, ((0, 0), (0, LANES - flat.shape[1])))


def _forward_layer(x, x16, w, t):
    proj = _mm(x16, w["w_main"], "nn", "proj_main")
    proj_ab = _mm(x16, w["w_ab"], "nn", "proj_ab")
    conv_out, qkvn = _qkv_conv_fwd(proj, w["conv_qkv"], t)
    gb = _gating_fwd(proj_ab, w["a_log"], w["dt_bias"], t)
    o_f, s_f, t_f = _gdn_fwd(qkvn, gb, 0, t)
    o_b, s_b, t_b = _gdn_fwd(qkvn, gb, 1, t)
    og = _gate_norm_fwd(o_f, o_b, proj, w["gdn_norm_w"], t)
    s = _sc_fwd(proj, w["conv_sc"], t)
    y_a = _mm(og, w["w_o_gdn"], "nn", "y_gdn")
    y_b = _mm(s, w["w_o_sc"], "nn", "y_sc")
    mixed = _mix_fwd(proj, y_a, y_b, t)
    r1 = _mm(mixed, w["w_out"], "nn", "out_proj")
    x1, x1_16 = _ln1_fwd(x, r1, w["ln1_g"], w["ln1_b"], t)
    hpre = _mm(x1_16, w["w_up"], "nn", "mlp_up")
    h = _relu2_fwd(hpre, w["b_up"], t)
    r2 = _mm(h, w["w_down"], "nn", "mlp_down")
    x2, x2_16 = _ln2_fwd(x1, r2, w["b_down"], w["ln2_g"], w["ln2_b"], t)
    saved = dict(x=x, x16=x16, proj=proj, proj_ab=proj_ab, conv_out=conv_out, qkvn=qkvn, gb=gb, o_f=o_f, o_b=o_b,
                 s_f=s_f, s_b=s_b, t_f=t_f, t_b=t_b, og=og, s=s, y_a=y_a, y_b=y_b, mixed=mixed, r1=r1, x1=x1,
                 x1_16=x1_16, hpre=hpre, h=h, r2=r2)
    return x2, x2_16, saved


def _backward_layer(dx2, w, a, t):
    (dx1_a, dr2), (db_down, dg2, db2) = _ln2_bwd(a["x1"], a["r2"], w["b_down"], w["ln2_g"], w["ln2_b"], dx2, t)
    dh = _mm(dr2, w["w_down"], "nt", "d_h")
    dw_down = _mm(a["h"], dr2, "tn", "dw_down")
    dhpre, db_up = _relu2_bwd(a["hpre"], w["b_up"], dh, t)
    dx1 = _mm(dhpre, w["w_up"], "nt", "d_x1", addends=[(1.0, dx1_a)])
    dw_up = _mm(a["x1_16"], dhpre, "tn", "dw_up")
    (dx_a, dr1), (dg1, db1) = _ln1_bwd(a["x"], a["r1"], w["ln1_g"], w["ln1_b"], dx1, t)
    dmixed = _mm(dr1, w["w_out"], "nt", "d_mixed")
    dw_out = _mm(a["mixed"], dr1, "tn", "dw_out")
    dgates, dy_a, dy_b = _mix_bwd(a["proj"], a["y_a"], a["y_b"], dmixed, t)
    dog = _mm(dy_a, w["w_o_gdn"], "nt", "d_og")
    dw_o_gdn = _mm(a["og"], dy_a, "tn", "dw_o_gdn")
    ds = _mm(dy_b, w["w_o_sc"], "nt", "d_s")
    dw_o_sc = _mm(a["s"], dy_b, "tn", "dw_o_sc")
    dsc, dconv_sc = _sc_bwd(a["proj"], w["conv_sc"], ds, t)
    do, dz, dnorm_w = _gate_norm_bwd(a["o_f"], a["o_b"], a["proj"], w["gdn_norm_w"], dog, t)
    dq_f, dgb_f = _gdn_bwd(a["qkvn"], a["gb"], a["s_f"], a["t_f"], do, 0, t)
    dq_b, dgb_b = _gdn_bwd(a["qkvn"], a["gb"], a["s_b"], a["t_b"], do, 1, t)
    dc = _post_conv_bwd(a["conv_out"], dq_f, dq_b, t)
    dqkv, dconv_qkv = _qkv_conv_bwd(a["proj"], w["conv_qkv"], dc, t)
    dab, da_log, ddt_bias = _gating_bwd(a["proj_ab"], w["a_log"], w["dt_bias"], dgb_f, dgb_b, t)

    pieces = [(dqkv, 0), (dz, 3 * D_MODEL), (dsc, 4 * D_MODEL), (dgates, 7 * D_MODEL)]
    dx = _mm(dab, w["w_ab"], "nt", "dx_ab", addends=[(1.0, dx_a)])
    dw_main = []
    for j, (piece, off) in enumerate(pieces):
        dx = _mm(piece, w["w_main"], "nt", f"dx_{j}", b_off=off, kdim=piece.shape[1], addends=[(1.0, dx)])
        dw_main.append(_mm(a["x16"], piece, "tn", f"dw_in_{j}"))
    dw_ab = _mm(a["x16"], dab, "tn", "dw_ab")
    dw_main = jnp.concatenate(dw_main, axis=1)
    dw_in = jnp.concatenate([dw_main[:, :QKVZ_COLS], dw_ab[:, :AB_COLS], dw_main[:, QKVZ_COLS:]], axis=1)
    grads = dict(w_in=dw_in, conv_qkv=dconv_qkv[:3], a_log=da_log[0, :2 * N_HEADS].reshape(2, N_HEADS),
                 dt_bias=ddt_bias[0, :2 * N_HEADS].reshape(2, N_HEADS), gdn_norm_w=dnorm_w[0], w_o_gdn=dw_o_gdn,
                 conv_sc=dconv_sc[:3], w_o_sc=dw_o_sc, w_out=dw_out, ln1_g=dg1[0], ln1_b=db1[0], w_up=dw_up,
                 b_up=db_up[0], w_down=dw_down, b_down=db_down[0], ln2_g=dg2[0], ln2_b=db2[0])
    return dx, grads


def kernel(x, w_in, conv_qkv, a_log, dt_bias, gdn_norm_w, w_o_gdn, conv_sc, w_o_sc, w_out, ln1_g, ln1_b, w_up, b_up, w_down, b_down, ln2_g, ln2_b, loss_target, m_w_in, m_conv_qkv, m_a_log, m_dt_bias, m_gdn_norm_w, m_w_o_gdn, m_conv_sc, m_w_o_sc, m_w_out, m_ln1_g, m_ln1_b, m_w_up, m_b_up, m_w_down, m_b_down, m_ln2_g, m_ln2_b, v_w_in, v_conv_qkv, v_a_log, v_dt_bias, v_gdn_norm_w, v_w_o_gdn, v_conv_sc, v_w_o_sc, v_w_out, v_ln1_g, v_ln1_b, v_w_up, v_b_up, v_w_down, v_b_down, v_ln2_g, v_ln2_b):
    weights = dict(w_in=w_in, conv_qkv=conv_qkv, a_log=a_log, dt_bias=dt_bias, gdn_norm_w=gdn_norm_w,
                   w_o_gdn=w_o_gdn, conv_sc=conv_sc, w_o_sc=w_o_sc, w_out=w_out, ln1_g=ln1_g, ln1_b=ln1_b,
                   w_up=w_up, b_up=b_up, w_down=w_down, b_down=b_down, ln2_g=ln2_g, ln2_b=ln2_b)
    m_in = dict(w_in=m_w_in, conv_qkv=m_conv_qkv, a_log=m_a_log, dt_bias=m_dt_bias, gdn_norm_w=m_gdn_norm_w,
                w_o_gdn=m_w_o_gdn, conv_sc=m_conv_sc, w_o_sc=m_w_o_sc, w_out=m_w_out, ln1_g=m_ln1_g, ln1_b=m_ln1_b,
                w_up=m_w_up, b_up=m_b_up, w_down=m_w_down, b_down=m_b_down, ln2_g=m_ln2_g, ln2_b=m_ln2_b)
    v_in = dict(w_in=v_w_in, conv_qkv=v_conv_qkv, a_log=v_a_log, dt_bias=v_dt_bias, gdn_norm_w=v_gdn_norm_w,
                w_o_gdn=v_w_o_gdn, conv_sc=v_conv_sc, w_o_sc=v_w_o_sc, w_out=v_w_out, ln1_g=v_ln1_g, ln1_b=v_ln1_b,
                w_up=v_w_up, b_up=v_b_up, w_down=v_w_down, b_down=v_b_down, ln2_g=v_ln2_g, ln2_b=v_ln2_b)
    t = x.shape[1]
    depth = w_in.shape[0]
    full = _gather_weights({name: weights[name] for name in MATRICES + CONVS})

    layers = []
    for l in range(depth):
        w_in_l = full["w_in"][l]
        layers.append(dict(
            w_main=jnp.concatenate([w_in_l[:, :QKVZ_COLS], w_in_l[:, QKVZ_COLS + AB_COLS:]], axis=1),
            w_ab=jnp.pad(w_in_l[:, QKVZ_COLS:QKVZ_COLS + AB_COLS], ((0, 0), (0, LANES - AB_COLS))),
            conv_qkv=jnp.pad(full["conv_qkv"][l], ((0, SUBLANES - 3), (0, 0))),
            conv_sc=jnp.pad(full["conv_sc"][l], ((0, SUBLANES - 3), (0, 0))),
            a_log=_lane_row(a_log[l]), dt_bias=_lane_row(dt_bias[l]), gdn_norm_w=gdn_norm_w[l][None, :],
            w_o_gdn=full["w_o_gdn"][l], w_o_sc=full["w_o_sc"][l], w_out=full["w_out"][l],
            ln1_g=ln1_g[l][None, :], ln1_b=ln1_b[l][None, :], w_up=full["w_up"][l], b_up=b_up[l][None, :],
            w_down=full["w_down"][l], b_down=b_down[l][None, :], ln2_g=ln2_g[l][None, :], ln2_b=ln2_b[l][None, :]))

    h = x.reshape(t, D_MODEL)
    h16 = h.astype(BF16)
    saved = []
    for l in range(depth):
        h, h16, acts = _forward_layer(h, h16, layers[l], t)
        saved.append(acts)
    dh, loss_local = _loss_stage(h, loss_target.reshape(t, D_MODEL), t)
    loss = lax.psum(loss_local, MESH_AXES)

    layer_grads = [None] * depth
    for l in reversed(range(depth)):
        dh, layer_grads[l] = _backward_layer(dh, layers[l], saved[l], t)
    stacked = {name: jnp.stack([g[name] for g in layer_grads]) for name in SHARDED + SMALL}
    shapes = {name: weights[name].shape for name in SHARDED + SMALL}
    grads = _reduce_grads({n: stacked[n] for n in SHARDED}, {n: stacked[n] for n in SMALL}, shapes)

    names = list(weights)
    updates = {n: _adamw(weights[n], grads[n], m_in[n], v_in[n], f"adamw_{n}") for n in names}
    return (loss, dh.reshape(x.shape), *[grads[n] for n in names], *[updates[n][0] for n in names],
            *[updates[n][1] for n in names], *[updates[n][2] for n in names])
```

```python
import functools

import jax
import jax.numpy as jnp
from jax import lax
from jax.experimental import pallas as pl
from jax.experimental.pallas import tpu as pltpu

F32 = jnp.float32
BF16 = jnp.bfloat16

D_MODEL = 1024
N_HEADS = 8
HEAD_DIM = 128
CHUNK = 64
D_FF = 4 * D_MODEL
DEPTH = 4
N_DEV = 8
LN_EPS = 1e-5
RMS_EPS = 1e-6
L2_EPS = 1e-6
ALPHA = (2 * DEPTH) ** 0.25
MAIN_COLS = 9 * D_MODEL
QKVZ_COLS = 4 * D_MODEL
AB_COLS = 4 * N_HEADS
W_IN_COLS = MAIN_COLS + AB_COLS
LANES = 128
SUBLANES = 8
VMEM_LIMIT = 48 * 1024 * 1024
MM_TILE = 1024
ROW_TILE = 256
WIDE_ROW_TILE = 128
ADAM = dict(lr=0.001, b1=0.9, b2=0.999, eps=1e-08, wd=0.01, step=10)
MESH_AXES = ("x", "y", "c")


def _params(sem=None):
    return pltpu.CompilerParams(dimension_semantics=sem, vmem_limit_bytes=VMEM_LIMIT)


_DIMS = {"nn": (1, 0), "nt": (1, 1), "tn": (0, 0)}


def _mm(a, b, mode, name, *, out_dtype=F32, b_off=0, kdim=None, addends=()):
    if mode == "nn":
        (m, k), n = a.shape, b.shape[1]
    elif mode == "nt":
        (m, k), n = a.shape, b.shape[0]
        k = kdim or k
    else:
        (k, m), n = a.shape, b.shape[1]
    tm, tn, tk = min(m, MM_TILE), min(n, MM_TILE), min(k, MM_TILE)
    assert m % tm == 0 and n % tn == 0 and k % tk == 0 and b_off % tk == 0
    nk = k // tk
    koff = b_off // tk
    ca, cb = _DIMS[mode]
    scales = tuple(s for s, _ in addends)
    na = len(addends)

    def body(a_ref, b_ref, *rest):
        add_refs, o_ref = rest[:na], rest[na]
        kk = pl.program_id(2)
        p = lax.dot_general(a_ref[...].astype(BF16), b_ref[...].astype(BF16), (((ca,), (cb,)), ((), ())),
                            preferred_element_type=F32)

        def finish(r):
            for s, ref in zip(scales, add_refs):
                r = r + s * ref[...].astype(F32)
            o_ref[...] = r.astype(o_ref.dtype)

        if nk == 1:
            finish(p)
        else:
            acc = rest[na + 1]

            @pl.when(kk == 0)
            def _():
                acc[...] = p

            @pl.when(kk > 0)
            def _():
                acc[...] += p

            @pl.when(kk == nk - 1)
            def _():
                finish(acc[...])

    if mode == "nn":
        a_spec = pl.BlockSpec((tm, tk), lambda i, j, kk: (i, kk))
        b_spec = pl.BlockSpec((tk, tn), lambda i, j, kk: (kk, j))
    elif mode == "nt":
        a_spec = pl.BlockSpec((tm, tk), lambda i, j, kk: (i, kk))
        b_spec = pl.BlockSpec((tn, tk), lambda i, j, kk: (j, kk + koff))
    else:
        a_spec = pl.BlockSpec((tk, tm), lambda i, j, kk: (kk, i))
        b_spec = pl.BlockSpec((tk, tn), lambda i, j, kk: (kk, j))
    o_spec = pl.BlockSpec((tm, tn), lambda i, j, kk: (i, j))
    return pl.pallas_call(
        body, name=name, grid=(m // tm, n // tn, nk),
        in_specs=[a_spec, b_spec] + [o_spec] * na, out_specs=o_spec,
        out_shape=jax.ShapeDtypeStruct((m, n), out_dtype),
        scratch_shapes=[pltpu.VMEM((tm, tn), F32)] if nk > 1 else [],
        compiler_params=_params(("parallel", "parallel", "arbitrary")),
    )(a, b, *[arr for _, arr in addends])


def _tile_call(name, body, t, tm, tiled, halo, params, outs, accs):
    tm = min(tm, t)
    assert t % tm == 0 and tm % SUBLANES == 0
    steps = t // tm
    hb = tm // SUBLANES
    nt, nh, npar, no = len(tiled), len(halo), len(params), len(outs)

    def kern(*refs):
        i = pl.program_id(0)
        t_refs = refs[:nt + nh]
        h_refs = refs[nt + nh:nt + 3 * nh]
        p_refs = refs[nt + 3 * nh:nt + 3 * nh + npar]
        o_refs = refs[nt + 3 * nh + npar:nt + 3 * nh + npar + no]
        a_refs = refs[nt + 3 * nh + npar + no:]
        tiles = [r[...] for r in t_refs]
        halos = []
        for j in range(nh):
            prev = h_refs[2 * j][SUBLANES - 1:SUBLANES, :].astype(F32)
            nxt = h_refs[2 * j + 1][0:1, :].astype(F32)
            halos.append((jnp.where(i > 0, prev, 0.0), jnp.where(i < steps - 1, nxt, 0.0)))
        o_vals, a_vals = body(tiles, halos, [r[...] for r in p_refs])
        for ref, val in zip(o_refs, o_vals):
            ref[...] = val.astype(ref.dtype)
        for ref, val in zip(a_refs, a_vals):
            @pl.when(i == 0)
            def _(ref=ref, val=val):
                ref[...] = val

            @pl.when(i > 0)
            def _(ref=ref, val=val):
                ref[...] += val

    in_specs, args = [], []
    for arr, nc, cb in list(tiled) + list(halo):
        in_specs.append(pl.BlockSpec((tm, nc), lambda i, cb=cb: (i, cb)))
        args.append(arr)
    last = t // SUBLANES - 1
    for arr, nc, cb in halo:
        in_specs.append(pl.BlockSpec((SUBLANES, nc), lambda i, cb=cb: (jnp.maximum(i * hb - 1, 0), cb)))
        in_specs.append(pl.BlockSpec((SUBLANES, nc), lambda i, cb=cb: (jnp.minimum((i + 1) * hb, last), cb)))
        args += [arr, arr]
    for arr in params:
        in_specs.append(pl.BlockSpec(arr.shape, lambda i: (0, 0)))
        args.append(arr)
    out_specs = [pl.BlockSpec((tm, nc), lambda i: (i, 0)) for nc, _ in outs]
    out_specs += [pl.BlockSpec(shape, lambda i: (0, 0)) for shape in accs]
    out_shape = [jax.ShapeDtypeStruct((t, nc), dt) for nc, dt in outs]
    out_shape += [jax.ShapeDtypeStruct(shape, F32) for shape in accs]
    res = pl.pallas_call(kern, name=name, grid=(steps,), in_specs=in_specs, out_specs=out_specs,
                         out_shape=out_shape, compiler_params=_params(("arbitrary",)))(*args)
    return res[:no], res[no:]


def _row_iota(x):
    return lax.broadcasted_iota(jnp.int32, x.shape, 0)


def _lane_iota(x):
    return lax.broadcasted_iota(jnp.int32, x.shape, 1)


def _shift_down(x, first_row):
    return jnp.where(_row_iota(x) == 0, first_row, pltpu.roll(x, 1, 0))


def _shift_up(x, last_row):
    n = x.shape[0]
    return jnp.where(_row_iota(x) == n - 1, last_row, pltpu.roll(x, n - 1, 0))


def _taps(w):
    return w[0:1, :], w[1:2, :], w[2:3, :]


def _tap_rows(d0, d1, d2, rows=SUBLANES):
    r = lax.broadcasted_iota(jnp.int32, (rows, d0.shape[1]), 0)
    return jnp.where(r == 0, d0, jnp.where(r == 1, d1, jnp.where(r == 2, d2, 0.0)))


def _colsum(x):
    return jnp.sum(x, axis=0, keepdims=True)


def _silu(x):
    return x * jax.nn.sigmoid(x)


def _softplus(x):
    return jnp.maximum(x, 0.0) + jnp.log(1.0 + jnp.exp(-jnp.abs(x)))


def _heads(x):
    return [x[:, h * HEAD_DIM:(h + 1) * HEAD_DIM] for h in range(x.shape[1] // HEAD_DIM)]


def _post_conv(c):
    blocks = _heads(_silu(c))
    out = []
    for j, blk in enumerate(blocks):
        if j < 2 * N_HEADS:
            blk = blk * lax.rsqrt(jnp.sum(blk * blk, axis=-1, keepdims=True) + L2_EPS)
        if j < N_HEADS:
            blk = blk * (HEAD_DIM ** -0.5)
        out.append(blk)
    return jnp.concatenate(out, axis=1)


def _gating(ab, a_log, dt_bias):
    lane = _lane_iota(ab)
    g = -jnp.exp(a_log) * _softplus(ab + dt_bias)
    return jnp.where(lane < 2 * N_HEADS, g, jnp.where(lane < AB_COLS, jax.nn.sigmoid(ab), 0.0))


def _gate_norm(o_f, o_b, z, norm_w):
    out = []
    for oh, zh in zip(_heads(o_f + o_b), _heads(z)):
        out.append(oh * lax.rsqrt(jnp.mean(oh * oh, axis=-1, keepdims=True) + RMS_EPS) * norm_w * _silu(zh))
    return jnp.concatenate(out, axis=1)


def _mix(gate_a, gate_b, y_a, y_b):
    return jax.nn.sigmoid(gate_a) * y_a + jax.nn.sigmoid(gate_b) * y_b


def _layer_norm(u, g, b):
    mu = jnp.mean(u, axis=-1, keepdims=True)
    var = jnp.mean(jnp.square(u - mu), axis=-1, keepdims=True)
    return (u - mu) * lax.rsqrt(var + LN_EPS) * g + b


def _ln1(x, r, g, b):
    return _layer_norm(ALPHA * x + r, g, b)


def _ln2(x, r, bias, g, b):
    return _layer_norm(ALPHA * x + r + bias, g, b)


def _relu2(hpre, bias):
    return jnp.square(jnp.maximum(hpre + bias, 0.0))


def _qkv_conv_fwd(proj, conv_w, t):
    def body(tiles, halos, params):
        (x,), ((xp, xn),), (w,) = tiles, halos, params
        w0, w1, w2 = _taps(w)
        c = w0 * _shift_down(x, xp) + w1 * x + w2 * _shift_up(x, xn)
        return [c, _post_conv(c)], []

    (c, qkvn), _ = _tile_call("qkv_conv_fwd", body, t, WIDE_ROW_TILE, [], [(proj, 3 * D_MODEL, 0)], [conv_w],
                              [(3 * D_MODEL, F32), (3 * D_MODEL, F32)], [])
    return c, qkvn


def _gating_fwd(proj_ab, a_log, dt_bias, t):
    def body(tiles, halos, params):
        return [_gating(tiles[0], params[0], params[1])], []

    (gb,), _ = _tile_call("gating_fwd", body, t, ROW_TILE, [(proj_ab, LANES, 0)], [], [a_log, dt_bias],
                          [(LANES, F32)], [])
    return gb


def _gate_norm_fwd(o_f, o_b, proj, norm_w, t):
    def body(tiles, halos, params):
        return [_gate_norm(tiles[0], tiles[1], tiles[2], params[0])], []

    (og,), _ = _tile_call("gate_norm_fwd", body, t, ROW_TILE,
                          [(o_f, D_MODEL, 0), (o_b, D_MODEL, 0), (proj, D_MODEL, 3)], [], [norm_w],
                          [(D_MODEL, BF16)], [])
    return og


def _sc_fwd(proj, conv_w, t):
    def body(tiles, halos, params):
        (sb,), ((cp, cn), (xp, xn)), (w,) = tiles[:1], halos, params
        sc, sx = tiles[1], tiles[2]
        w0, w1, w2 = _taps(w)
        u = sc * sx
        return [sb * (w0 * _shift_down(u, cp * xp) + w1 * u + w2 * _shift_up(u, cn * xn))], []

    (s,), _ = _tile_call("sc_fwd", body, t, ROW_TILE, [(proj, D_MODEL, 4)],
                         [(proj, D_MODEL, 5), (proj, D_MODEL, 6)], [conv_w], [(D_MODEL, BF16)], [])
    return s


def _mix_fwd(proj, y_a, y_b, t):
    def body(tiles, halos, params):
        return [_mix(*tiles)], []

    (mixed,), _ = _tile_call("mix_fwd", body, t, ROW_TILE,
                             [(proj, D_MODEL, 7), (proj, D_MODEL, 8), (y_a, D_MODEL, 0), (y_b, D_MODEL, 0)], [], [],
                             [(D_MODEL, BF16)], [])
    return mixed


def _ln1_fwd(x, r, g, b, t):
    def body(tiles, halos, params):
        y = _ln1(tiles[0], tiles[1], params[0], params[1])
        return [y, y], []

    (y, y16), _ = _tile_call("ln1_fwd", body, t, ROW_TILE, [(x, D_MODEL, 0), (r, D_MODEL, 0)], [], [g, b],
                             [(D_MODEL, F32), (D_MODEL, BF16)], [])
    return y, y16


def _ln2_fwd(x, r, bias, g, b, t):
    def body(tiles, halos, params):
        y = _ln2(tiles[0], tiles[1], params[0], params[1], params[2])
        return [y, y], []

    (y, y16), _ = _tile_call("ln2_fwd", body, t, ROW_TILE, [(x, D_MODEL, 0), (r, D_MODEL, 0)], [], [bias, g, b],
                             [(D_MODEL, F32), (D_MODEL, BF16)], [])
    return y, y16


def _relu2_fwd(hpre, bias, t):
    def body(tiles, halos, params):
        return [_relu2(tiles[0], params[0])], []

    (h,), _ = _tile_call("relu2_fwd", body, t, ROW_TILE, [(hpre, D_FF, 0)], [], [bias], [(D_FF, BF16)], [])
    return h


def _loss_stage(y, target, t):
    def body(tiles, halos, params):
        d = tiles[0] - tiles[1]
        part = 0.5 * jnp.sum(jnp.mean(d * d, axis=-1, keepdims=True), axis=0, keepdims=True)
        return [d * (1.0 / D_MODEL)], [jnp.broadcast_to(part, (1, LANES))]

    (dy,), (loss,) = _tile_call("loss", body, t, ROW_TILE, [(y, D_MODEL, 0), (target, D_MODEL, 0)], [], [],
                                [(D_MODEL, F32)], [(1, LANES)])
    return dy, loss[0, 0]


def _ln2_bwd(x, r, bias, g, b, dy, t):
    def body(tiles, halos, params):
        _, vjp = jax.vjp(_ln2, tiles[0], tiles[1], params[0], params[1], params[2])
        dx, dr, dbias, dg, db = vjp(tiles[2])
        return [dx, dr], [dbias, dg, db]

    return _tile_call("ln2_bwd", body, t, ROW_TILE, [(x, D_MODEL, 0), (r, D_MODEL, 0), (dy, D_MODEL, 0)], [],
                      [bias, g, b], [(D_MODEL, F32), (D_MODEL, BF16)], [(1, D_MODEL)] * 3)


def _ln1_bwd(x, r, g, b, dy, t):
    def body(tiles, halos, params):
        _, vjp = jax.vjp(_ln1, tiles[0], tiles[1], params[0], params[1])
        dx, dr, dg, db = vjp(tiles[2])
        return [dx, dr], [dg, db]

    return _tile_call("ln1_bwd", body, t, ROW_TILE, [(x, D_MODEL, 0), (r, D_MODEL, 0), (dy, D_MODEL, 0)], [],
                      [g, b], [(D_MODEL, F32), (D_MODEL, BF16)], [(1, D_MODEL)] * 2)


def _relu2_bwd(hpre, bias, dh, t):
    def body(tiles, halos, params):
        _, vjp = jax.vjp(_relu2, tiles[0], params[0])
        dhpre, dbias = vjp(tiles[1])
        return [dhpre], [dbias]

    (dhpre,), (dbias,) = _tile_call("relu2_bwd", body, t, WIDE_ROW_TILE, [(hpre, D_FF, 0), (dh, D_FF, 0)], [],
                                    [bias], [(D_FF, BF16)], [(1, D_FF)])
    return dhpre, dbias


def _mix_bwd(proj, y_a, y_b, dmixed, t):
    def body(tiles, halos, params):
        _, vjp = jax.vjp(_mix, *tiles[:4])
        dga, dgb, dya, dyb = vjp(tiles[4])
        return [jnp.concatenate([dga, dgb], axis=1), dya, dyb], []

    (dgates, dya, dyb), _ = _tile_call(
        "mix_bwd", body, t, ROW_TILE,
        [(proj, D_MODEL, 7), (proj, D_MODEL, 8), (y_a, D_MODEL, 0), (y_b, D_MODEL, 0), (dmixed, D_MODEL, 0)], [], [],
        [(2 * D_MODEL, BF16), (D_MODEL, BF16), (D_MODEL, BF16)], [])
    return dgates, dya, dyb


def _sc_bwd(proj, conv_w, ds, t):
    def body(tiles, halos, params):
        ds_, sb, sc, sx = tiles
        (dsp, dsn), (sbp, sbn), (scp, scn), (sxp, sxn) = halos
        w0, w1, w2 = _taps(params[0])
        u = sc * sx
        u_prev, u_next = _shift_down(u, scp * sxp), _shift_up(u, scn * sxn)
        dconv = ds_ * sb
        du = w0 * _shift_up(dconv, dsn * sbn) + w1 * dconv + w2 * _shift_down(dconv, dsp * sbp)
        dsb = ds_ * (w0 * u_prev + w1 * u + w2 * u_next)
        dw = _tap_rows(_colsum(dconv * u_prev), _colsum(dconv * u), _colsum(dconv * u_next))
        return [jnp.concatenate([dsb, du * sx, du * sc], axis=1)], [dw]

    (dsc,), (dw,) = _tile_call("sc_bwd", body, t, ROW_TILE, [],
                               [(ds, D_MODEL, 0), (proj, D_MODEL, 4), (proj, D_MODEL, 5), (proj, D_MODEL, 6)],
                               [conv_w], [(3 * D_MODEL, BF16)], [(SUBLANES, D_MODEL)])
    return dsc, dw


def _gate_norm_bwd(o_f, o_b, proj, norm_w, dog, t):
    def body(tiles, halos, params):
        _, vjp = jax.vjp(_gate_norm, tiles[0], tiles[1], tiles[2], params[0])
        do, _, dz, dnw = vjp(tiles[3])
        return [do, dz], [dnw]

    (do, dz), (dnw,) = _tile_call(
        "gate_norm_bwd", body, t, ROW_TILE,
        [(o_f, D_MODEL, 0), (o_b, D_MODEL, 0), (proj, D_MODEL, 3), (dog, D_MODEL, 0)], [], [norm_w],
        [(D_MODEL, F32), (D_MODEL, BF16)], [(1, HEAD_DIM)])
    return do, dz, dnw


def _post_conv_bwd(c, dq_f, dq_b, t):
    def body(tiles, halos, params):
        _, vjp = jax.vjp(_post_conv, tiles[0])
        return [vjp(tiles[1] + tiles[2])[0]], []

    (dc,), _ = _tile_call("post_conv_bwd", body, t, WIDE_ROW_TILE,
                          [(c, 3 * D_MODEL, 0), (dq_f, 3 * D_MODEL, 0), (dq_b, 3 * D_MODEL, 0)], [], [],
                          [(3 * D_MODEL, F32)], [])
    return dc


def _qkv_conv_bwd(proj, conv_w, dc, t):
    def body(tiles, halos, params):
        (dcp, dcn), (xp, xn) = halos
        d, x = tiles
        w0, w1, w2 = _taps(params[0])
        dx = w0 * _shift_up(d, dcn) + w1 * d + w2 * _shift_down(d, dcp)
        dw = _tap_rows(_colsum(d * _shift_down(x, xp)), _colsum(d * x), _colsum(d * _shift_up(x, xn)))
        return [dx], [dw]

    (dqkv,), (dw,) = _tile_call("qkv_conv_bwd", body, t, WIDE_ROW_TILE, [],
                                [(dc, 3 * D_MODEL, 0), (proj, 3 * D_MODEL, 0)], [conv_w],
                                [(3 * D_MODEL, BF16)], [(SUBLANES, 3 * D_MODEL)])
    return dqkv, dw


def _gating_bwd(proj_ab, a_log, dt_bias, dgb_f, dgb_b, t):
    def body(tiles, halos, params):
        _, vjp = jax.vjp(_gating, tiles[0], params[0], params[1])
        dab, dal, ddt = vjp(tiles[1] + tiles[2])
        return [dab], [dal, ddt]

    (dab,), (dal, ddt) = _tile_call("gating_bwd", body, t, ROW_TILE,
                                    [(proj_ab, LANES, 0), (dgb_f, LANES, 0), (dgb_b, LANES, 0)], [],
                                    [a_log, dt_bias], [(LANES, BF16)], [(1, LANES)] * 2)
    return dab, dal, ddt


@functools.partial(jax.custom_vjp, nondiff_argnums=(2, 3))
def _dot(a, b, ca, cb):
    return lax.dot_general(a.astype(BF16), b.astype(BF16), (((ca,), (cb,)), ((), ())), preferred_element_type=F32)


def _dot_fwd(a, b, ca, cb):
    return _dot(a, b, ca, cb), (a, b)


def _dot_bwd(ca, cb, res, ct):
    a, b = res
    fa, fb = 1 - ca, 1 - cb
    da = _dot(ct, b, 1, fb) if ca == 1 else _dot(b, ct, fb, 1)
    db = _dot(a, ct, fa, 0) if cb == 0 else _dot(ct, a, 0, fa)
    return da, db


_dot.defvjp(_dot_fwd, _dot_bwd)


def _dot3(a, b, ca, cb):
    ah, bh = a.astype(BF16).astype(F32), b.astype(BF16).astype(F32)
    return _dot(ah, bh, ca, cb) + _dot(a - ah, bh, ca, cb) + _dot(ah, b - bh, ca, cb)


def _split3(x):
    hi = x.astype(BF16)
    r1 = x - hi.astype(F32)
    mid = r1.astype(BF16)
    return hi, mid, (r1 - mid.astype(F32)).astype(BF16)


def _dot_exact(a, b, ca, cb, exact):
    dims = (((ca,), (cb,)), ((), ()))
    if exact == 0:
        return sum(lax.dot_general(a.astype(BF16), p, dims, preferred_element_type=F32) for p in _split3(b))
    return sum(lax.dot_general(p, b.astype(BF16), dims, preferred_element_type=F32) for p in _split3(a))


def _tri_masks(n, rev):
    r = lax.broadcasted_iota(jnp.int32, (n, n), 0)
    c = lax.broadcasted_iota(jnp.int32, (n, n), 1)
    return ((c >= r), (c > r)) if rev else ((c <= r), (c < r))


@functools.partial(jax.custom_vjp, nondiff_argnums=(1,))
def _cumsum_rows(g, rev):
    incl, _ = _tri_masks(g.shape[0], rev)
    return _dot_exact(incl.astype(F32), g, 1, 0, 0)


_cumsum_rows.defvjp(lambda g, rev: (_cumsum_rows(g, rev), None),
                    lambda rev, _, ct: (_cumsum_rows(ct, not rev),))


def _eye(n):
    return (lax.broadcasted_iota(jnp.int32, (n, n), 0) == lax.broadcasted_iota(jnp.int32, (n, n), 1)).astype(F32)


@jax.custom_vjp
def _to_rows(x):
    return _dot_exact(_eye(x.shape[1]), x, 1, 1, 0)


@jax.custom_vjp
def _to_cols(y):
    return _dot_exact(y, _eye(y.shape[0]), 0, 0, 1)


_to_rows.defvjp(lambda x: (_to_rows(x), None), lambda _, ct: (_to_cols(ct),))
_to_cols.defvjp(lambda y: (_to_cols(y), None), lambda _, ct: (_to_rows(ct),))


def _pick_col(arr, idx):
    return jnp.sum(jnp.where(_lane_iota(arr) == idx, arr, 0.0), axis=1, keepdims=True)


def _pick_row(arr, idx):
    return jnp.sum(jnp.where(_row_iota(arr) == idx, arr, 0.0), axis=0, keepdims=True)


def _chunk_gates(gb, direction, rev):
    n = gb.shape[0]
    incl, strict = _tri_masks(n, rev)
    gc = _cumsum_rows(gb, rev)
    gc_rows = _to_rows(gc)
    lanes = [direction * N_HEADS + h for h in range(N_HEADS)]
    cols = [_pick_col(gc, ln) for ln in lanes]
    rows = [_pick_row(gc_rows, ln) for ln in lanes]
    betas = [_pick_col(gb, 2 * N_HEADS + ln) for ln in lanes]
    decays = [jnp.where(incl, jnp.exp(jnp.where(incl, c - r, 0.0)), 0.0) for c, r in zip(cols, rows)]
    return cols, betas, decays, strict


def _chunk_heads(chunks):
    hs = dict(qs=[], ks=[], vs=[], cols=[], betas=[], decays=[], stricts=[], lasts=[])
    for direction, (q, k, v, gb) in enumerate(chunks):
        rev = direction == 1
        cols, betas, decays, strict = _chunk_gates(gb, direction, rev)
        last_idx = 0 if rev else gb.shape[0] - 1
        hs["qs"] += _heads(q) if q is not None else []
        hs["ks"] += _heads(k)
        hs["vs"] += _heads(v) if v is not None else []
        hs["cols"] += cols
        hs["betas"] += betas
        hs["decays"] += decays
        hs["stricts"] += [strict] * N_HEADS
        hs["lasts"] += [_pick_row(c, last_idx) for c in cols]
    return hs


def _chunk_lmat(k_f, gb_f, k_b, gb_b):
    hs = _chunk_heads(((None, k_f, None, gb_f), (None, k_b, None, gb_b)))
    kk = [_dot(kh * b, kh, 1, 1) for kh, b in zip(hs["ks"], hs["betas"])]
    return tuple(jnp.where(s, x * d, 0.0) for s, x, d in zip(hs["stricts"], kk, hs["decays"]))


def _tri_inverse(lmats):
    n = lmats[0].shape[0]
    eye = _eye(n)
    powers = [-lm for lm in lmats]
    invs = [eye + p for p in powers]
    span = 2
    while span < n:
        powers = [_dot(p, p, 1, 0) for p in powers]
        steps = [_dot(p, i, 1, 0) for p, i in zip(powers, invs)]
        invs = [i + s for i, s in zip(invs, steps)]
        span *= 2
    return tuple(invs)


def _chunk_out(chunk_f, chunk_b, tmats, states):
    hs = _chunk_heads((chunk_f, chunk_b))
    qs, ks, vs, cols, betas, decays, lasts = (hs[key] for key in ("qs", "ks", "vs", "cols", "betas", "decays", "lasts"))
    us = [_dot(tm, vh * b, 1, 0) for tm, vh, b in zip(tmats, vs, betas)]
    ws = [_dot(tm, kh * b * jnp.exp(c), 1, 0) for tm, kh, b, c in zip(tmats, ks, betas, cols)]
    attns = [_dot(qh, kh, 1, 1) * d for qh, kh, d in zip(qs, ks, decays)]
    wss = [_dot(w, st, 1, 0) for w, st in zip(ws, states)]
    v_news = [u - x for u, x in zip(us, wss)]
    inter = [_dot(qh * jnp.exp(c), st, 1, 0) for qh, c, st in zip(qs, cols, states)]
    intra = [_dot(a, vn, 1, 0) for a, vn in zip(attns, v_news)]
    adds = [_dot(kh * jnp.exp(l - c), vn, 0, 0) for kh, l, c, vn in zip(ks, lasts, cols, v_news)]
    new_states = tuple(st * jnp.exp(l) + a for st, l, a in zip(states, lasts, adds))
    outs = [x + y for x, y in zip(inter, intra)]
    return jnp.concatenate(outs[:N_HEADS], axis=1), jnp.concatenate(outs[N_HEADS:], axis=1), new_states


BOTH = 2 * N_HEADS


def _gdn_specs(n, first_backwards):
    idx = [(lambda i: n - 1 - i) if (d == 0) == first_backwards else (lambda i: i) for d in range(2)]

    def both(shape_of_block, index_tail):
        return [pl.BlockSpec(shape_of_block, lambda i, f=f: (f(i),) + index_tail) for f in idx]

    return idx, both


def _gdn_fwd(qkvn, gb, t):
    n = t // CHUNK
    idx, both = _gdn_specs(n, False)

    def body(qf, kf, vf, qb, kb, vb, gbf, gbb, of_ref, ob_ref, sf_ref, sb_ref, tf_ref, tb_ref, state):
        @pl.when(pl.program_id(0) == 0)
        def _():
            state[...] = jnp.zeros_like(state)

        chunk_f = (qf[...], kf[...], vf[...], gbf[...])
        chunk_b = (qb[...], kb[...], vb[...], gbb[...])
        tmats = _tri_inverse(_chunk_lmat(chunk_f[1], chunk_f[3], chunk_b[1], chunk_b[3]))
        states = tuple(state[h] for h in range(BOTH))
        o_f, o_b, new_states = _chunk_out(chunk_f, chunk_b, tmats, states)
        of_ref[...] = o_f
        ob_ref[...] = o_b
        for h in range(BOTH):
            s_ref, t_ref = (sf_ref, tf_ref) if h < N_HEADS else (sb_ref, tb_ref)
            s_ref[0, h % N_HEADS] = states[h]
            t_ref[0, h % N_HEADS] = tmats[h]
            state[h] = new_states[h]

    qkv_specs = [pl.BlockSpec((CHUNK, D_MODEL), lambda i, f=f, p=p: (f(i), p)) for f in idx for p in range(3)]
    return pl.pallas_call(
        body, name="gdn_fwd", grid=(n,),
        in_specs=qkv_specs + both((CHUNK, LANES), (0,)),
        out_specs=both((CHUNK, D_MODEL), (0,)) + both((1, N_HEADS, HEAD_DIM, HEAD_DIM), (0, 0, 0))
        + both((1, N_HEADS, CHUNK, CHUNK), (0, 0, 0)),
        out_shape=[jax.ShapeDtypeStruct((t, D_MODEL), F32)] * 2
        + [jax.ShapeDtypeStruct((n, N_HEADS, HEAD_DIM, HEAD_DIM), F32)] * 2
        + [jax.ShapeDtypeStruct((n, N_HEADS, CHUNK, CHUNK), F32)] * 2,
        scratch_shapes=[pltpu.VMEM((BOTH, HEAD_DIM, HEAD_DIM), F32)],
        compiler_params=_params(("arbitrary",)),
    )(*([qkvn] * 6), gb, gb)


def _gdn_bwd(qkvn, gb, s_f, s_b, t_f, t_b, do, t):
    n = t // CHUNK
    idx, both = _gdn_specs(n, True)

    def body(qf, kf, vf, qb, kb, vb, gbf, gbb, sf_ref, sb_ref, tf_ref, tb_ref, dof, dob,
             dqf_ref, dqb_ref, dgf_ref, dgb_ref, dstate):
        @pl.when(pl.program_id(0) == 0)
        def _():
            dstate[...] = jnp.zeros_like(dstate)

        chunk_f = (qf[...], kf[...], vf[...], gbf[...])
        chunk_b = (qb[...], kb[...], vb[...], gbb[...])
        tmats = tuple((tf_ref if h < N_HEADS else tb_ref)[0, h % N_HEADS] for h in range(BOTH))
        states = tuple((sf_ref if h < N_HEADS else sb_ref)[0, h % N_HEADS] for h in range(BOTH))
        _, out_vjp = jax.vjp(_chunk_out, chunk_f, chunk_b, tmats, states)
        d_f, d_b, dtm, dst = out_vjp((dof[...], dob[...], tuple(dstate[h] for h in range(BOTH))))
        firsts = [_dot3(tm, d, 0, 0) for tm, d in zip(tmats, dtm)]
        dlm = tuple(-_dot3(x, tm, 1, 1) for x, tm in zip(firsts, tmats))
        _, lmat_vjp = jax.vjp(_chunk_lmat, chunk_f[1], chunk_f[3], chunk_b[1], chunk_b[3])
        dk_f, dg_f, dk_b, dg_b = lmat_vjp(dlm)
        dqf_ref[...] = jnp.concatenate([d_f[0], d_f[1] + dk_f, d_f[2]], axis=1)
        dqb_ref[...] = jnp.concatenate([d_b[0], d_b[1] + dk_b, d_b[2]], axis=1)
        dgf_ref[...] = d_f[3] + dg_f
        dgb_ref[...] = d_b[3] + dg_b
        for h in range(BOTH):
            dstate[h] = dst[h]

    qkv_specs = [pl.BlockSpec((CHUNK, D_MODEL), lambda i, f=f, p=p: (f(i), p)) for f in idx for p in range(3)]
    return pl.pallas_call(
        body, name="gdn_bwd", grid=(n,),
        in_specs=qkv_specs + both((CHUNK, LANES), (0,)) + both((1, N_HEADS, HEAD_DIM, HEAD_DIM), (0, 0, 0))
        + both((1, N_HEADS, CHUNK, CHUNK), (0, 0, 0)) + both((CHUNK, D_MODEL), (0,)),
        out_specs=both((CHUNK, 3 * D_MODEL), (0,)) + both((CHUNK, LANES), (0,)),
        out_shape=[jax.ShapeDtypeStruct((t, 3 * D_MODEL), F32)] * 2 + [jax.ShapeDtypeStruct((t, LANES), F32)] * 2,
        scratch_shapes=[pltpu.VMEM((BOTH, HEAD_DIM, HEAD_DIM), F32)],
        compiler_params=_params(("arbitrary",)),
    )(*([qkvn] * 6), gb, gb, s_f, s_b, t_f, t_b, do, do)


def _mesh_pos():
    return lax.axis_index("x"), lax.axis_index("y"), lax.axis_index("c")


def _all_gather(shards):
    na = len(shards)

    def body(*refs):
        x_refs, out_refs = refs[:na], refs[na:2 * na]
        send_sems, recv_sems, local_sems = refs[2 * na:]
        x, y, c = _mesh_pos()
        me, sibling = (x, y, c), (x, y, 1 - c)
        chips = [(1 - x, y), (x, 1 - y), (1 - x, 1 - y)]

        def block(a, px, py, pc):
            return out_refs[a].at[4 * px + 2 * py + pc]

        def copy(a, k, blk, to, src=None):
            return pltpu.make_async_remote_copy(
                src_ref=block(a, *blk) if src is None else src, dst_ref=block(a, *blk),
                send_sem=send_sems.at[a, k], recv_sem=recv_sems.at[a, k],
                device_id=to, device_id_type=pl.DeviceIdType.MESH)

        mine = [pltpu.make_async_copy(x_refs[a], block(a, *me), local_sems.at[a]) for a in range(na)]
        first, passed = [], []
        for a in range(na):
            mine[a].start()
            first.append(copy(a, 0, me, sibling, src=x_refs[a]))
            first += [copy(a, 1 + j, me, (*chip, c), src=x_refs[a]) for j, chip in enumerate(chips)]
        for cp in first:
            cp.start()
        for j, chip in enumerate(chips):
            for a in range(na):
                copy(a, 1 + j, (*chip, c), me).wait_recv()
                passed.append(copy(a, 4 + j, (*chip, c), sibling))
                passed[-1].start()
        for a in range(na):
            copy(a, 0, sibling, me).wait_recv()
            for j, chip in enumerate(chips):
                copy(a, 4 + j, (*chip, 1 - c), me).wait_recv()
        for cp in first + passed:
            cp.wait_send()
        for cp in mine:
            cp.wait()

    any_spec = pl.BlockSpec(memory_space=pl.ANY)
    return pl.pallas_call(
        body, name="weights_all_gather",
        out_shape=[jax.ShapeDtypeStruct((N_DEV,) + s.shape, s.dtype) for s in shards],
        in_specs=[any_spec] * na, out_specs=[any_spec] * na,
        scratch_shapes=[pltpu.SemaphoreType.DMA((na, N_DEV - 1)), pltpu.SemaphoreType.DMA((na, N_DEV - 1)),
                        pltpu.SemaphoreType.DMA((na,))],
    )(*shards)


def _scatter_blocks(blocks):
    na = len(blocks)

    def body(*refs):
        g_refs, land_refs = refs[:na], refs[na:2 * na]
        send_sems, recv_sems, local_sems = refs[2 * na:]
        x, y, c = _mesh_pos()
        mine = 4 * x + 2 * y + c
        local = [pltpu.make_async_copy(g_refs[a].at[mine], land_refs[a].at[mine], local_sems.at[a])
                 for a in range(na)]
        for cp in local:
            cp.start()
        sends, recvs = [], []
        for k in range(1, N_DEV):
            px = 1 - x if k & 4 else x
            py = 1 - y if k & 2 else y
            pc = 1 - c if k & 1 else c
            peer = 4 * px + 2 * py + pc
            for a in range(na):
                sends.append(pltpu.make_async_remote_copy(
                    src_ref=g_refs[a].at[peer], dst_ref=land_refs[a].at[mine],
                    send_sem=send_sems.at[a, k - 1], recv_sem=recv_sems.at[a, k - 1],
                    device_id=(px, py, pc), device_id_type=pl.DeviceIdType.MESH))
                recvs.append(pltpu.make_async_remote_copy(
                    src_ref=g_refs[a].at[mine], dst_ref=land_refs[a].at[peer],
                    send_sem=send_sems.at[a, k - 1], recv_sem=recv_sems.at[a, k - 1],
                    device_id=(x, y, c), device_id_type=pl.DeviceIdType.MESH))
        for cp in sends:
            cp.start()
        for cp in recvs:
            cp.wait_recv()
        for cp in sends:
            cp.wait_send()
        for cp in local:
            cp.wait()

    any_spec = pl.BlockSpec(memory_space=pl.ANY)
    return pl.pallas_call(
        body, name="grads_scatter",
        out_shape=[jax.ShapeDtypeStruct(b.shape, b.dtype) for b in blocks],
        in_specs=[any_spec] * na, out_specs=[any_spec] * na,
        scratch_shapes=[pltpu.SemaphoreType.DMA((na, N_DEV - 1)), pltpu.SemaphoreType.DMA((na, N_DEV - 1)),
                        pltpu.SemaphoreType.DMA((na,))],
    )(*blocks)


def _sum_slots(land, name):
    _, rows, cols = land.shape
    tr = _row_tile(rows, cols * 4 * N_DEV, budget=4 << 20)

    def body(*refs):
        acc = refs[0][0].astype(F32)
        for ref in refs[1:N_DEV]:
            acc = acc + ref[0].astype(F32)
        refs[N_DEV][...] = acc

    return pl.pallas_call(
        body, name=name, grid=(rows // tr,),
        in_specs=[pl.BlockSpec((1, tr, cols), lambda i, s=s: (s, i, 0)) for s in range(N_DEV)],
        out_specs=pl.BlockSpec((tr, cols), lambda i: (i, 0)),
        out_shape=jax.ShapeDtypeStruct((rows, cols), F32),
        compiler_params=_params(("parallel",)),
    )(*([land] * N_DEV))


def _row_tile(rows, row_bytes, budget=1 << 20):
    if rows * row_bytes <= budget or rows % SUBLANES:
        return rows
    best = SUBLANES
    for tr in range(SUBLANES, rows + 1, SUBLANES):
        if rows % tr == 0 and tr * row_bytes <= budget:
            best = tr
    return best


def _adamw(w, g, m, v, name):
    shape = w.shape
    cols = shape[-1]
    rows = w.size // cols
    tr = _row_tile(rows, cols * 4)
    b1, b2 = ADAM["b1"], ADAM["b2"]

    def body(w_ref, g_ref, m_ref, v_ref, d_ref, nm_ref, nv_ref):
        gv = g_ref[...]
        nm = b1 * m_ref[...] + (1.0 - b1) * gv
        nv = b2 * v_ref[...] + (1.0 - b2) * jnp.square(gv)
        m_hat = nm / (1.0 - b1 ** ADAM["step"])
        v_hat = nv / (1.0 - b2 ** ADAM["step"])
        d_ref[...] = -ADAM["lr"] * (m_hat / (jnp.sqrt(v_hat) + ADAM["eps"]) + ADAM["wd"] * w_ref[...])
        nm_ref[...] = nm
        nv_ref[...] = nv

    spec = pl.BlockSpec((tr, cols), lambda i: (i, 0))
    outs = pl.pallas_call(
        body, name=name, grid=(rows // tr,), in_specs=[spec] * 4, out_specs=[spec] * 3,
        out_shape=[jax.ShapeDtypeStruct((rows, cols), F32)] * 3, compiler_params=_params(("parallel",)),
    )(*[a.reshape(rows, cols) for a in (w, g, m, v)])
    return [o.reshape(shape) for o in outs]


MATRICES = ("w_in", "w_o_gdn", "w_o_sc", "w_out", "w_up", "w_down")
CONVS = ("conv_qkv", "conv_sc")
SHARDED = ("w_in", "conv_qkv", "w_o_gdn", "conv_sc", "w_o_sc", "w_out", "w_up", "w_down")
SMALL = ("a_log", "dt_bias", "gdn_norm_w", "ln1_g", "ln1_b", "b_up", "b_down", "ln2_g", "ln2_b")
COLUMN_SHARDED = ("w_in", "conv_qkv", "conv_sc", "w_up")
PACK_COLS = 1024


def _pack(parts, row_multiple):
    flat = jnp.concatenate(parts, axis=-1)
    unit = PACK_COLS * row_multiple
    pad = -flat.shape[-1] % unit
    flat = jnp.pad(flat, [(0, 0)] * (flat.ndim - 1) + [(0, pad)])
    return flat.reshape(flat.shape[:-1] + (flat.shape[-1] // PACK_COLS, PACK_COLS))


def _unshard(name, blocks):
    _, l, r, c = blocks.shape
    if name in COLUMN_SHARDED:
        return blocks.transpose(1, 2, 0, 3).reshape(l, r, N_DEV * c)
    return blocks.transpose(1, 0, 2, 3).reshape(l, N_DEV * r, c)


def _to_shards(name, full):
    l, r, c = full.shape
    if name in COLUMN_SHARDED:
        return full.reshape(l, r, N_DEV, c // N_DEV).transpose(2, 0, 1, 3).reshape(N_DEV, -1)
    return full.reshape(l, N_DEV, r // N_DEV, c).transpose(1, 0, 2, 3).reshape(N_DEV, -1)


def _gather_weights(shards):
    parts, layout = [], []
    for name in MATRICES[1:]:
        parts.append(shards[name].astype(BF16).reshape(-1))
        layout.append((name, shards[name].shape, 1))
    for name in CONVS:
        parts.append(jnp.stack(_split3(shards[name])).reshape(-1))
        layout.append((name, shards[name].shape, 3))
    w_in = shards["w_in"]
    w_in_all, rest_all = _all_gather([w_in.astype(BF16).reshape(-1, w_in.shape[-1]), _pack(parts, 16)])
    full = {"w_in": _unshard("w_in", w_in_all.reshape(N_DEV, *w_in.shape))}
    rest_all, off = rest_all.reshape(N_DEV, -1), 0
    for name, shape, pieces in layout:
        size = pieces * shape[0] * shape[1] * shape[2]
        blk = rest_all[:, off:off + size]
        off += size
        if pieces == 3:
            blk = jnp.sum(blk.reshape(N_DEV, 3, *shape).astype(F32), axis=1)
        full[name] = _unshard(name, blk.reshape(N_DEV, *shape))
    return full


def _unpack(flat, names, shapes):
    out, off = {}, 0
    for name in names:
        size = 1
        for s in shapes[name]:
            size *= s
        out[name] = flat[off:off + size].reshape(shapes[name])
        off += size
    return out


def _reduce_grads(full_grads, small_grads, shard_shapes):
    l, r, c = full_grads["w_in"].shape
    w_in = full_grads["w_in"].astype(BF16)
    w_in = w_in.reshape(l, r, N_DEV, c // N_DEV).transpose(2, 0, 1, 3).reshape(N_DEV, l * r, c // N_DEV)
    rest = _pack([_to_shards(name, full_grads[name].astype(BF16)) for name in SHARDED[1:]], LANES)
    small = jnp.concatenate([small_grads[name].reshape(-1) for name in SMALL])
    small = _pack([jnp.broadcast_to(small[None, :], (N_DEV, small.shape[0]))], SUBLANES)
    landed = _scatter_blocks([w_in, rest, small])
    out = {"w_in": _sum_slots(landed[0], "grads_sum_w_in").reshape(shard_shapes["w_in"])}
    out.update(_unpack(_sum_slots(landed[1], "grads_sum_rest").reshape(-1), SHARDED[1:], shard_shapes))
    out.update(_unpack(_sum_slots(landed[2], "grads_sum_small").reshape(-1), SMALL, shard_shapes))
    return out


def _lane_row(values):
    flat = values.reshape(1, -1)
    return jnp.pad(flat, ((0, 0), (0, LANES - flat.shape[1])))


def _forward_layer(x, x16, w, t):
    proj = _mm(x16, w["w_main"], "nn", "proj_main")
    proj_ab = _mm(x16, w["w_ab"], "nn", "proj_ab")
    conv_out, qkvn = _qkv_conv_fwd(proj, w["conv_qkv"], t)
    gb = _gating_fwd(proj_ab, w["a_log"], w["dt_bias"], t)
    o_f, o_b, s_f, s_b, t_f, t_b = _gdn_fwd(qkvn, gb, t)
    og = _gate_norm_fwd(o_f, o_b, proj, w["gdn_norm_w"], t)
    s = _sc_fwd(proj, w["conv_sc"], t)
    y_a = _mm(og, w["w_o_gdn"], "nn", "y_gdn")
    y_b = _mm(s, w["w_o_sc"], "nn", "y_sc")
    mixed = _mix_fwd(proj, y_a, y_b, t)
    r1 = _mm(mixed, w["w_out"], "nn", "out_proj")
    x1, x1_16 = _ln1_fwd(x, r1, w["ln1_g"], w["ln1_b"], t)
    hpre = _mm(x1_16, w["w_up"], "nn", "mlp_up")
    h = _relu2_fwd(hpre, w["b_up"], t)
    r2 = _mm(h, w["w_down"], "nn", "mlp_down")
    x2, x2_16 = _ln2_fwd(x1, r2, w["b_down"], w["ln2_g"], w["ln2_b"], t)
    saved = dict(x=x, x16=x16, proj=proj, proj_ab=proj_ab, conv_out=conv_out, qkvn=qkvn, gb=gb, o_f=o_f, o_b=o_b,
                 s_f=s_f, s_b=s_b, t_f=t_f, t_b=t_b, og=og, s=s, y_a=y_a, y_b=y_b, mixed=mixed, r1=r1, x1=x1,
                 x1_16=x1_16, hpre=hpre, h=h, r2=r2)
    return x2, x2_16, saved


def _backward_layer(dx2, w, a, t):
    (dx1_a, dr2), (db_down, dg2, db2) = _ln2_bwd(a["x1"], a["r2"], w["b_down"], w["ln2_g"], w["ln2_b"], dx2, t)
    dh = _mm(dr2, w["w_down"], "nt", "d_h")
    dw_down = _mm(a["h"], dr2, "tn", "dw_down")
    dhpre, db_up = _relu2_bwd(a["hpre"], w["b_up"], dh, t)
    dx1 = _mm(dhpre, w["w_up"], "nt", "d_x1", addends=[(1.0, dx1_a)])
    dw_up = _mm(a["x1_16"], dhpre, "tn", "dw_up")
    (dx_a, dr1), (dg1, db1) = _ln1_bwd(a["x"], a["r1"], w["ln1_g"], w["ln1_b"], dx1, t)
    dmixed = _mm(dr1, w["w_out"], "nt", "d_mixed")
    dw_out = _mm(a["mixed"], dr1, "tn", "dw_out")
    dgates, dy_a, dy_b = _mix_bwd(a["proj"], a["y_a"], a["y_b"], dmixed, t)
    dog = _mm(dy_a, w["w_o_gdn"], "nt", "d_og")
    dw_o_gdn = _mm(a["og"], dy_a, "tn", "dw_o_gdn")
    ds = _mm(dy_b, w["w_o_sc"], "nt", "d_s")
    dw_o_sc = _mm(a["s"], dy_b, "tn", "dw_o_sc")
    dsc, dconv_sc = _sc_bwd(a["proj"], w["conv_sc"], ds, t)
    do, dz, dnorm_w = _gate_norm_bwd(a["o_f"], a["o_b"], a["proj"], w["gdn_norm_w"], dog, t)
    dq_f, dq_b, dgb_f, dgb_b = _gdn_bwd(a["qkvn"], a["gb"], a["s_f"], a["s_b"], a["t_f"], a["t_b"], do, t)
    dc = _post_conv_bwd(a["conv_out"], dq_f, dq_b, t)
    dqkv, dconv_qkv = _qkv_conv_bwd(a["proj"], w["conv_qkv"], dc, t)
    dab, da_log, ddt_bias = _gating_bwd(a["proj_ab"], w["a_log"], w["dt_bias"], dgb_f, dgb_b, t)

    pieces = [(dqkv, 0), (dz, 3 * D_MODEL), (dsc, 4 * D_MODEL), (dgates, 7 * D_MODEL)]
    dx = _mm(dab, w["w_ab"], "nt", "dx_ab", addends=[(1.0, dx_a)])
    dw_main = []
    for j, (piece, off) in enumerate(pieces):
        dx = _mm(piece, w["w_main"], "nt", f"dx_{j}", b_off=off, kdim=piece.shape[1], addends=[(1.0, dx)])
        dw_main.append(_mm(a["x16"], piece, "tn", f"dw_in_{j}"))
    dw_ab = _mm(a["x16"], dab, "tn", "dw_ab")
    dw_main = jnp.concatenate(dw_main, axis=1)
    dw_in = jnp.concatenate([dw_main[:, :QKVZ_COLS], dw_ab[:, :AB_COLS], dw_main[:, QKVZ_COLS:]], axis=1)
    grads = dict(w_in=dw_in, conv_qkv=dconv_qkv[:3], a_log=da_log[0, :2 * N_HEADS].reshape(2, N_HEADS),
                 dt_bias=ddt_bias[0, :2 * N_HEADS].reshape(2, N_HEADS), gdn_norm_w=dnorm_w[0], w_o_gdn=dw_o_gdn,
                 conv_sc=dconv_sc[:3], w_o_sc=dw_o_sc, w_out=dw_out, ln1_g=dg1[0], ln1_b=db1[0], w_up=dw_up,
                 b_up=db_up[0], w_down=dw_down, b_down=db_down[0], ln2_g=dg2[0], ln2_b=db2[0])
    return dx, grads


def kernel(x, w_in, conv_qkv, a_log, dt_bias, gdn_norm_w, w_o_gdn, conv_sc, w_o_sc, w_out, ln1_g, ln1_b, w_up, b_up, w_down, b_down, ln2_g, ln2_b, loss_target, m_w_in, m_conv_qkv, m_a_log, m_dt_bias, m_gdn_norm_w, m_w_o_gdn, m_conv_sc, m_w_o_sc, m_w_out, m_ln1_g, m_ln1_b, m_w_up, m_b_up, m_w_down, m_b_down, m_ln2_g, m_ln2_b, v_w_in, v_conv_qkv, v_a_log, v_dt_bias, v_gdn_norm_w, v_w_o_gdn, v_conv_sc, v_w_o_sc, v_w_out, v_ln1_g, v_ln1_b, v_w_up, v_b_up, v_w_down, v_b_down, v_ln2_g, v_ln2_b):
    weights = dict(w_in=w_in, conv_qkv=conv_qkv, a_log=a_log, dt_bias=dt_bias, gdn_norm_w=gdn_norm_w,
                   w_o_gdn=w_o_gdn, conv_sc=conv_sc, w_o_sc=w_o_sc, w_out=w_out, ln1_g=ln1_g, ln1_b=ln1_b,
                   w_up=w_up, b_up=b_up, w_down=w_down, b_down=b_down, ln2_g=ln2_g, ln2_b=ln2_b)
    m_in = dict(w_in=m_w_in, conv_qkv=m_conv_qkv, a_log=m_a_log, dt_bias=m_dt_bias, gdn_norm_w=m_gdn_norm_w,
                w_o_gdn=m_w_o_gdn, conv_sc=m_conv_sc, w_o_sc=m_w_o_sc, w_out=m_w_out, ln1_g=m_ln1_g, ln1_b=m_ln1_b,
                w_up=m_w_up, b_up=m_b_up, w_down=m_w_down, b_down=m_b_down, ln2_g=m_ln2_g, ln2_b=m_ln2_b)
    v_in = dict(w_in=v_w_in, conv_qkv=v_conv_qkv, a_log=v_a_log, dt_bias=v_dt_bias, gdn_norm_w=v_gdn_norm_w,
                w_o_gdn=v_w_o_gdn, conv_sc=v_conv_sc, w_o_sc=v_w_o_sc, w_out=v_w_out, ln1_g=v_ln1_g, ln1_b=v_ln1_b,
                w_up=v_w_up, b_up=v_b_up, w_down=v_w_down, b_down=v_b_down, ln2_g=v_ln2_g, ln2_b=v_ln2_b)
    t = x.shape[1]
    depth = w_in.shape[0]
    full = _gather_weights({name: weights[name] for name in MATRICES + CONVS})

    layers = []
    for l in range(depth):
        w_in_l = full["w_in"][l]
        layers.append(dict(
            w_main=jnp.concatenate([w_in_l[:, :QKVZ_COLS], w_in_l[:, QKVZ_COLS + AB_COLS:]], axis=1),
            w_ab=jnp.pad(w_in_l[:, QKVZ_COLS:QKVZ_COLS + AB_COLS], ((0, 0), (0, LANES - AB_COLS))),
            conv_qkv=jnp.pad(full["conv_qkv"][l], ((0, SUBLANES - 3), (0, 0))),
            conv_sc=jnp.pad(full["conv_sc"][l], ((0, SUBLANES - 3), (0, 0))),
            a_log=_lane_row(a_log[l]), dt_bias=_lane_row(dt_bias[l]), gdn_norm_w=gdn_norm_w[l][None, :],
            w_o_gdn=full["w_o_gdn"][l], w_o_sc=full["w_o_sc"][l], w_out=full["w_out"][l],
            ln1_g=ln1_g[l][None, :], ln1_b=ln1_b[l][None, :], w_up=full["w_up"][l], b_up=b_up[l][None, :],
            w_down=full["w_down"][l], b_down=b_down[l][None, :], ln2_g=ln2_g[l][None, :], ln2_b=ln2_b[l][None, :]))

    h = x.reshape(t, D_MODEL)
    h16 = h.astype(BF16)
    saved = []
    for l in range(depth):
        h, h16, acts = _forward_layer(h, h16, layers[l], t)
        saved.append(acts)
    dh, loss_local = _loss_stage(h, loss_target.reshape(t, D_MODEL), t)
    loss = lax.psum(loss_local, MESH_AXES)

    layer_grads = [None] * depth
    for l in reversed(range(depth)):
        dh, layer_grads[l] = _backward_layer(dh, layers[l], saved[l], t)
    stacked = {name: jnp.stack([g[name] for g in layer_grads]) for name in SHARDED + SMALL}
    shapes = {name: weights[name].shape for name in SHARDED + SMALL}
    grads = _reduce_grads({n: stacked[n] for n in SHARDED}, {n: stacked[n] for n in SMALL}, shapes)

    names = list(weights)
    updates = {n: _adamw(weights[n], grads[n], m_in[n], v_in[n], f"adamw_{n}") for n in names}
    return (loss, dh.reshape(x.shape), *[grads[n] for n in names], *[updates[n][0] for n in names],
            *[updates[n][1] for n in names], *[updates[n][2] for n in names])
```

```python
import functools

import jax
import jax.numpy as jnp
from jax import lax
from jax.experimental import pallas as pl
from jax.experimental.pallas import tpu as pltpu

F32 = jnp.float32
BF16 = jnp.bfloat16

D_MODEL = 1024
N_HEADS = 8
HEAD_DIM = 128
CHUNK = 64
D_FF = 4 * D_MODEL
DEPTH = 4
N_DEV = 8
LN_EPS = 1e-5
RMS_EPS = 1e-6
L2_EPS = 1e-6
ALPHA = (2 * DEPTH) ** 0.25
MAIN_COLS = 9 * D_MODEL
QKVZ_COLS = 4 * D_MODEL
AB_COLS = 4 * N_HEADS
W_IN_COLS = MAIN_COLS + AB_COLS
LANES = 128
SUBLANES = 8
VMEM_LIMIT = 48 * 1024 * 1024
MM_TILE = 1024
ROW_TILE = 256
WIDE_ROW_TILE = 128
ADAM = dict(lr=0.001, b1=0.9, b2=0.999, eps=1e-08, wd=0.01, step=10)
MESH_AXES = ("x", "y", "c")


def _params(sem=None):
    return pltpu.CompilerParams(dimension_semantics=sem, vmem_limit_bytes=VMEM_LIMIT)


_DIMS = {"nn": (1, 0), "nt": (1, 1), "tn": (0, 0)}


def _mm(a, b, mode, name, *, out_dtype=F32, b_off=0, kdim=None, addends=(), epilogue=None):
    if mode == "nn":
        (m, k), n = a.shape, b.shape[1]
    elif mode == "nt":
        (m, k), n = a.shape, b.shape[0]
        k = kdim or k
    else:
        (k, m), n = a.shape, b.shape[1]
    tm, tn, tk = min(m, MM_TILE), min(n, MM_TILE), min(k, MM_TILE)
    assert m % tm == 0 and n % tn == 0 and k % tk == 0 and b_off % tk == 0
    nk = k // tk
    koff = b_off // tk
    ca, cb = _DIMS[mode]
    scales = tuple(s for s, _ in addends)
    na = len(addends)
    epi_fn, epi_in, epi_out = epilogue or (None, (), (("tile", out_dtype),))
    ne, no = len(epi_in), len(epi_out)

    def body(a_ref, b_ref, *rest):
        add_refs, epi_refs, o_refs = rest[:na], rest[na:na + ne], rest[na + ne:na + ne + no]
        kk = pl.program_id(2)
        p = lax.dot_general(a_ref[...].astype(BF16), b_ref[...].astype(BF16), (((ca,), (cb,)), ((), ())),
                            preferred_element_type=F32)

        def finish(r):
            for s, ref in zip(scales, add_refs):
                r = r + s * ref[...].astype(F32)
            vals = epi_fn(r, *[ref[...] for ref in epi_refs]) if epi_fn else (r,)
            for (kind, _), ref, val in zip(epi_out, o_refs, vals):
                if kind == "colsum":
                    val = jnp.where(lax.broadcasted_iota(jnp.int32, ref.shape, 0) == 0, val, 0.0)
                ref[...] = val.astype(ref.dtype)

        if nk == 1:
            finish(p)
        else:
            acc = rest[na + ne + no]

            @pl.when(kk == 0)
            def _():
                acc[...] = p

            @pl.when(kk > 0)
            def _():
                acc[...] += p

            @pl.when(kk == nk - 1)
            def _():
                finish(acc[...])

    if mode == "nn":
        a_spec = pl.BlockSpec((tm, tk), lambda i, j, kk: (i, kk))
        b_spec = pl.BlockSpec((tk, tn), lambda i, j, kk: (kk, j))
    elif mode == "nt":
        a_spec = pl.BlockSpec((tm, tk), lambda i, j, kk: (i, kk))
        b_spec = pl.BlockSpec((tn, tk), lambda i, j, kk: (j, kk + koff))
    else:
        a_spec = pl.BlockSpec((tk, tm), lambda i, j, kk: (kk, i))
        b_spec = pl.BlockSpec((tk, tn), lambda i, j, kk: (kk, j))
    kinds = {"tile": (pl.BlockSpec((tm, tn), lambda i, j, kk: (i, j)), (m, n)),
             "row": (pl.BlockSpec((1, tn), lambda i, j, kk: (0, j)), (1, n)),
             "colsum": (pl.BlockSpec((SUBLANES, tn), lambda i, j, kk: (i, j)), (SUBLANES * (m // tm), n))}
    o_spec = kinds["tile"][0]
    res = pl.pallas_call(
        body, name=name, grid=(m // tm, n // tn, nk),
        in_specs=[a_spec, b_spec] + [o_spec] * na + [kinds[kind][0] for _, kind in epi_in],
        out_specs=[kinds[kind][0] for kind, _ in epi_out],
        out_shape=[jax.ShapeDtypeStruct(kinds[kind][1], dt) for kind, dt in epi_out],
        scratch_shapes=[pltpu.VMEM((tm, tn), F32)] if nk > 1 else [],
        compiler_params=_params(("parallel", "parallel", "arbitrary")),
    )(a, b, *[arr for _, arr in addends], *[arr for arr, _ in epi_in])
    return res if epilogue else res[0]


def _tile_call(name, body, t, tm, tiled, halo, params, outs, accs):
    tm = min(tm, t)
    assert t % tm == 0 and tm % SUBLANES == 0
    steps = t // tm
    hb = tm // SUBLANES
    nt, nh, npar, no = len(tiled), len(halo), len(params), len(outs)

    def kern(*refs):
        i = pl.program_id(0)
        t_refs = refs[:nt + nh]
        h_refs = refs[nt + nh:nt + 3 * nh]
        p_refs = refs[nt + 3 * nh:nt + 3 * nh + npar]
        o_refs = refs[nt + 3 * nh + npar:nt + 3 * nh + npar + no]
        a_refs = refs[nt + 3 * nh + npar + no:]
        tiles = [r[...].astype(F32) for r in t_refs]
        halos = []
        for j in range(nh):
            prev = h_refs[2 * j][SUBLANES - 1:SUBLANES, :].astype(F32)
            nxt = h_refs[2 * j + 1][0:1, :].astype(F32)
            halos.append((jnp.where(i > 0, prev, 0.0), jnp.where(i < steps - 1, nxt, 0.0)))
        o_vals, a_vals = body(tiles, halos, [r[...] for r in p_refs])
        for ref, val in zip(o_refs, o_vals):
            ref[...] = val.astype(ref.dtype)
        for ref, val in zip(a_refs, a_vals):
            @pl.when(i == 0)
            def _(ref=ref, val=val):
                ref[...] = val

            @pl.when(i > 0)
            def _(ref=ref, val=val):
                ref[...] += val

    in_specs, args = [], []
    for arr, nc, cb in list(tiled) + list(halo):
        in_specs.append(pl.BlockSpec((tm, nc), lambda i, cb=cb: (i, cb)))
        args.append(arr)
    last = t // SUBLANES - 1
    for arr, nc, cb in halo:
        in_specs.append(pl.BlockSpec((SUBLANES, nc), lambda i, cb=cb: (jnp.maximum(i * hb - 1, 0), cb)))
        in_specs.append(pl.BlockSpec((SUBLANES, nc), lambda i, cb=cb: (jnp.minimum((i + 1) * hb, last), cb)))
        args += [arr, arr]
    for arr in params:
        in_specs.append(pl.BlockSpec(arr.shape, lambda i: (0, 0)))
        args.append(arr)
    out_specs = [pl.BlockSpec((tm, nc), lambda i: (i, 0)) for nc, _ in outs]
    out_specs += [pl.BlockSpec(shape, lambda i: (0, 0)) for shape in accs]
    out_shape = [jax.ShapeDtypeStruct((t, nc), dt) for nc, dt in outs]
    out_shape += [jax.ShapeDtypeStruct(shape, F32) for shape in accs]
    res = pl.pallas_call(kern, name=name, grid=(steps,), in_specs=in_specs, out_specs=out_specs,
                         out_shape=out_shape, compiler_params=_params(("arbitrary",)))(*args)
    return res[:no], res[no:]


def _row_iota(x):
    return lax.broadcasted_iota(jnp.int32, x.shape, 0)


def _lane_iota(x):
    return lax.broadcasted_iota(jnp.int32, x.shape, 1)


def _shift_down(x, first_row):
    return jnp.where(_row_iota(x) == 0, first_row, pltpu.roll(x, 1, 0))


def _shift_up(x, last_row):
    n = x.shape[0]
    return jnp.where(_row_iota(x) == n - 1, last_row, pltpu.roll(x, n - 1, 0))


def _taps(w):
    return w[0:1, :], w[1:2, :], w[2:3, :]


def _tap_rows(d0, d1, d2, rows=SUBLANES):
    r = lax.broadcasted_iota(jnp.int32, (rows, d0.shape[1]), 0)
    return jnp.where(r == 0, d0, jnp.where(r == 1, d1, jnp.where(r == 2, d2, 0.0)))


def _colsum(x):
    return jnp.sum(x, axis=0, keepdims=True)


def _silu(x):
    return x * jax.nn.sigmoid(x)


def _softplus(x):
    return jnp.maximum(x, 0.0) + jnp.log(1.0 + jnp.exp(-jnp.abs(x)))


def _heads(x):
    return [x[:, h * HEAD_DIM:(h + 1) * HEAD_DIM] for h in range(x.shape[1] // HEAD_DIM)]


def _post_conv(c):
    blocks = _heads(_silu(c))
    out = []
    for j, blk in enumerate(blocks):
        if j < 2 * N_HEADS:
            blk = blk * lax.rsqrt(jnp.sum(blk * blk, axis=-1, keepdims=True) + L2_EPS)
        if j < N_HEADS:
            blk = blk * (HEAD_DIM ** -0.5)
        out.append(blk)
    return jnp.concatenate(out, axis=1)


def _gating(ab, a_log, dt_bias):
    lane = _lane_iota(ab)
    g = -jnp.exp(a_log) * _softplus(ab + dt_bias)
    return jnp.where(lane < 2 * N_HEADS, g, jnp.where(lane < AB_COLS, jax.nn.sigmoid(ab), 0.0))


def _gate_norm(o_f, o_b, z, norm_w):
    out = []
    for oh, zh in zip(_heads(o_f + o_b), _heads(z)):
        out.append(oh * lax.rsqrt(jnp.mean(oh * oh, axis=-1, keepdims=True) + RMS_EPS) * norm_w * _silu(zh))
    return jnp.concatenate(out, axis=1)


def _mix(gate_a, gate_b, y_a, y_b):
    return jax.nn.sigmoid(gate_a) * y_a + jax.nn.sigmoid(gate_b) * y_b


def _layer_norm(u, g, b):
    mu = jnp.mean(u, axis=-1, keepdims=True)
    var = jnp.mean(jnp.square(u - mu), axis=-1, keepdims=True)
    return (u - mu) * lax.rsqrt(var + LN_EPS) * g + b


def _ln1(x, r, g, b):
    return _layer_norm(ALPHA * x + r, g, b)


def _ln2(x, r, bias, g, b):
    return _layer_norm(ALPHA * x + r + bias, g, b)


def _relu2(hpre, bias):
    return jnp.square(jnp.maximum(hpre + bias, 0.0))


def _qkv_conv_fwd(proj, conv_w, t):
    def body(tiles, halos, params):
        (x,), ((xp, xn),), (w,) = tiles, halos, params
        w0, w1, w2 = _taps(w)
        c = w0 * _shift_down(x, xp) + w1 * x + w2 * _shift_up(x, xn)
        return [c, _post_conv(c)], []

    (c, qkvn), _ = _tile_call("qkv_conv_fwd", body, t, WIDE_ROW_TILE, [], [(proj, 3 * D_MODEL, 0)], [conv_w],
                              [(3 * D_MODEL, F32), (3 * D_MODEL, F32)], [])
    return c, qkvn


def _gating_fwd(proj_ab, a_log, dt_bias, t):
    def body(tiles, halos, params):
        return [_gating(tiles[0], params[0], params[1])], []

    (gb,), _ = _tile_call("gating_fwd", body, t, ROW_TILE, [(proj_ab, LANES, 0)], [], [a_log, dt_bias],
                          [(LANES, F32)], [])
    return gb


def _gate_norm_fwd(o_f, o_b, proj, norm_w, t):
    def body(tiles, halos, params):
        return [_gate_norm(tiles[0], tiles[1], tiles[2], params[0])], []

    (og,), _ = _tile_call("gate_norm_fwd", body, t, ROW_TILE,
                          [(o_f, D_MODEL, 0), (o_b, D_MODEL, 0), (proj, D_MODEL, 3)], [], [norm_w],
                          [(D_MODEL, BF16)], [])
    return og


def _sc_fwd(proj, conv_w, t):
    def body(tiles, halos, params):
        (sb,), ((cp, cn), (xp, xn)), (w,) = tiles[:1], halos, params
        sc, sx = tiles[1], tiles[2]
        w0, w1, w2 = _taps(w)
        u = sc * sx
        return [sb * (w0 * _shift_down(u, cp * xp) + w1 * u + w2 * _shift_up(u, cn * xn))], []

    (s,), _ = _tile_call("sc_fwd", body, t, ROW_TILE, [(proj, D_MODEL, 4)],
                         [(proj, D_MODEL, 5), (proj, D_MODEL, 6)], [conv_w], [(D_MODEL, BF16)], [])
    return s


def _mix_fwd(proj, y_a, y_b, t):
    def body(tiles, halos, params):
        return [_mix(*tiles)], []

    (mixed,), _ = _tile_call("mix_fwd", body, t, ROW_TILE,
                             [(proj, D_MODEL, 7), (proj, D_MODEL, 8), (y_a, D_MODEL, 0), (y_b, D_MODEL, 0)], [], [],
                             [(D_MODEL, BF16)], [])
    return mixed


def _ln1_fwd(x, r, g, b, t):
    def body(tiles, halos, params):
        y = _ln1(tiles[0], tiles[1], params[0], params[1])
        return [y, y], []

    (y, y16), _ = _tile_call("ln1_fwd", body, t, ROW_TILE, [(x, D_MODEL, 0), (r, D_MODEL, 0)], [], [g, b],
                             [(D_MODEL, F32), (D_MODEL, BF16)], [])
    return y, y16


def _ln2_fwd(x, r, bias, g, b, t):
    def body(tiles, halos, params):
        y = _ln2(tiles[0], tiles[1], params[0], params[1], params[2])
        return [y, y], []

    (y, y16), _ = _tile_call("ln2_fwd", body, t, ROW_TILE, [(x, D_MODEL, 0), (r, D_MODEL, 0)], [], [bias, g, b],
                             [(D_MODEL, F32), (D_MODEL, BF16)], [])
    return y, y16


def _relu2_epilogue(bias):
    return (lambda r, b: (r, _relu2(r, b))), [(bias, "row")], [("tile", F32), ("tile", BF16)]


def _relu2_back_epilogue(hpre, bias):
    def fn(r, hp, b):
        _, vjp = jax.vjp(_relu2, hp, b)
        return vjp(r)

    return fn, [(hpre, "tile"), (bias, "row")], [("tile", BF16), ("colsum", F32)]


def _loss_stage(y, target, t):
    def body(tiles, halos, params):
        d = tiles[0] - tiles[1]
        part = 0.5 * jnp.sum(jnp.mean(d * d, axis=-1, keepdims=True), axis=0, keepdims=True)
        return [d * (1.0 / D_MODEL)], [jnp.broadcast_to(part, (1, LANES))]

    (dy,), (loss,) = _tile_call("loss", body, t, ROW_TILE, [(y, D_MODEL, 0), (target, D_MODEL, 0)], [], [],
                                [(D_MODEL, F32)], [(1, LANES)])
    return dy, loss[0, 0]


def _ln2_bwd(x, r, bias, g, b, dy, t):
    def body(tiles, halos, params):
        _, vjp = jax.vjp(_ln2, tiles[0], tiles[1], params[0], params[1], params[2])
        dx, dr, dbias, dg, db = vjp(tiles[2])
        return [dx, dr], [dbias, dg, db]

    return _tile_call("ln2_bwd", body, t, ROW_TILE, [(x, D_MODEL, 0), (r, D_MODEL, 0), (dy, D_MODEL, 0)], [],
                      [bias, g, b], [(D_MODEL, F32), (D_MODEL, BF16)], [(1, D_MODEL)] * 3)


def _ln1_bwd(x, r, g, b, dy, t):
    def body(tiles, halos, params):
        _, vjp = jax.vjp(_ln1, tiles[0], tiles[1], params[0], params[1])
        dx, dr, dg, db = vjp(tiles[2])
        return [dx, dr], [dg, db]

    return _tile_call("ln1_bwd", body, t, ROW_TILE, [(x, D_MODEL, 0), (r, D_MODEL, 0), (dy, D_MODEL, 0)], [],
                      [g, b], [(D_MODEL, F32), (D_MODEL, BF16)], [(1, D_MODEL)] * 2)


def _mix_bwd(proj, y_a, y_b, dmixed, t):
    def body(tiles, halos, params):
        _, vjp = jax.vjp(_mix, *tiles[:4])
        dga, dgb, dya, dyb = vjp(tiles[4])
        return [jnp.concatenate([dga, dgb], axis=1), dya, dyb], []

    (dgates, dya, dyb), _ = _tile_call(
        "mix_bwd", body, t, ROW_TILE,
        [(proj, D_MODEL, 7), (proj, D_MODEL, 8), (y_a, D_MODEL, 0), (y_b, D_MODEL, 0), (dmixed, D_MODEL, 0)], [], [],
        [(2 * D_MODEL, BF16), (D_MODEL, BF16), (D_MODEL, BF16)], [])
    return dgates, dya, dyb


def _sc_bwd(proj, conv_w, ds, t):
    def body(tiles, halos, params):
        ds_, sb, sc, sx = tiles
        (dsp, dsn), (sbp, sbn), (scp, scn), (sxp, sxn) = halos
        w0, w1, w2 = _taps(params[0])
        u = sc * sx
        u_prev, u_next = _shift_down(u, scp * sxp), _shift_up(u, scn * sxn)
        dconv = ds_ * sb
        du = w0 * _shift_up(dconv, dsn * sbn) + w1 * dconv + w2 * _shift_down(dconv, dsp * sbp)
        dsb = ds_ * (w0 * u_prev + w1 * u + w2 * u_next)
        dw = _tap_rows(_colsum(dconv * u_prev), _colsum(dconv * u), _colsum(dconv * u_next))
        return [jnp.concatenate([dsb, du * sx, du * sc], axis=1)], [dw]

    (dsc,), (dw,) = _tile_call("sc_bwd", body, t, ROW_TILE, [],
                               [(ds, D_MODEL, 0), (proj, D_MODEL, 4), (proj, D_MODEL, 5), (proj, D_MODEL, 6)],
                               [conv_w], [(3 * D_MODEL, BF16)], [(SUBLANES, D_MODEL)])
    return dsc, dw


def _gate_norm_bwd(o_f, o_b, proj, norm_w, dog, t):
    def body(tiles, halos, params):
        _, vjp = jax.vjp(_gate_norm, tiles[0], tiles[1], tiles[2], params[0])
        do, _, dz, dnw = vjp(tiles[3])
        return [do, dz], [dnw]

    (do, dz), (dnw,) = _tile_call(
        "gate_norm_bwd", body, t, ROW_TILE,
        [(o_f, D_MODEL, 0), (o_b, D_MODEL, 0), (proj, D_MODEL, 3), (dog, D_MODEL, 0)], [], [norm_w],
        [(D_MODEL, F32), (D_MODEL, BF16)], [(1, HEAD_DIM)])
    return do, dz, dnw


def _qkv_conv_bwd(proj, conv_w, c, dq_f, dq_b, t):
    def post_conv_back(cv, ct):
        _, vjp = jax.vjp(_post_conv, cv)
        return vjp(ct)[0]

    def body(tiles, halos, params):
        cv, df, db, x = tiles
        (cp, cn), (dfp, dfn), (dbp, dbn), (xp, xn) = halos
        w0, w1, w2 = _taps(params[0])
        d = post_conv_back(cv, df + db)
        d_prev, d_next = post_conv_back(cp, dfp + dbp), post_conv_back(cn, dfn + dbn)
        dx = w0 * _shift_up(d, d_next) + w1 * d + w2 * _shift_down(d, d_prev)
        dw = _tap_rows(_colsum(d * _shift_down(x, xp)), _colsum(d * x), _colsum(d * _shift_up(x, xn)))
        return [dx], [dw]

    wide = 3 * D_MODEL
    (dqkv,), (dw,) = _tile_call("qkv_conv_bwd", body, t, WIDE_ROW_TILE, [],
                                [(c, wide, 0), (dq_f, wide, 0), (dq_b, wide, 0), (proj, wide, 0)], [conv_w],
                                [(wide, BF16)], [(SUBLANES, wide)])
    return dqkv, dw


def _gating_bwd(proj_ab, a_log, dt_bias, dgb_f, dgb_b, t):
    def body(tiles, halos, params):
        _, vjp = jax.vjp(_gating, tiles[0], params[0], params[1])
        dab, dal, ddt = vjp(tiles[1] + tiles[2])
        return [dab], [dal, ddt]

    (dab,), (dal, ddt) = _tile_call("gating_bwd", body, t, ROW_TILE,
                                    [(proj_ab, LANES, 0), (dgb_f, LANES, 0), (dgb_b, LANES, 0)], [],
                                    [a_log, dt_bias], [(LANES, BF16)], [(1, LANES)] * 2)
    return dab, dal, ddt


@functools.partial(jax.custom_vjp, nondiff_argnums=(2, 3))
def _dot(a, b, ca, cb):
    return lax.dot_general(a.astype(BF16), b.astype(BF16), (((ca,), (cb,)), ((), ())), preferred_element_type=F32)


def _dot_fwd(a, b, ca, cb):
    return _dot(a, b, ca, cb), (a, b)


def _dot_bwd(ca, cb, res, ct):
    a, b = res
    fa, fb = 1 - ca, 1 - cb
    da = _dot(ct, b, 1, fb) if ca == 1 else _dot(b, ct, fb, 1)
    db = _dot(a, ct, fa, 0) if cb == 0 else _dot(ct, a, 0, fa)
    return da, db


_dot.defvjp(_dot_fwd, _dot_bwd)


def _split3(x):
    hi = x.astype(BF16)
    r1 = x - hi.astype(F32)
    mid = r1.astype(BF16)
    return hi, mid, (r1 - mid.astype(F32)).astype(BF16)


def _dot_exact(a, b, ca, cb, exact):
    dims = (((ca,), (cb,)), ((), ()))
    if exact == 0:
        return sum(lax.dot_general(a.astype(BF16), p, dims, preferred_element_type=F32) for p in _split3(b))
    return sum(lax.dot_general(p, b.astype(BF16), dims, preferred_element_type=F32) for p in _split3(a))


def _tri_masks(n, rev):
    r = lax.broadcasted_iota(jnp.int32, (n, n), 0)
    c = lax.broadcasted_iota(jnp.int32, (n, n), 1)
    return ((c >= r), (c > r)) if rev else ((c <= r), (c < r))


@functools.partial(jax.custom_vjp, nondiff_argnums=(1,))
def _cumsum_rows(g, rev):
    incl, _ = _tri_masks(g.shape[0], rev)
    return _dot_exact(incl.astype(F32), g, 1, 0, 0)


_cumsum_rows.defvjp(lambda g, rev: (_cumsum_rows(g, rev), None),
                    lambda rev, _, ct: (_cumsum_rows(ct, not rev),))


def _eye(n):
    return (lax.broadcasted_iota(jnp.int32, (n, n), 0) == lax.broadcasted_iota(jnp.int32, (n, n), 1)).astype(F32)


@jax.custom_vjp
def _to_rows(x):
    return _dot_exact(_eye(x.shape[1]), x, 1, 1, 0)


@jax.custom_vjp
def _to_cols(y):
    return _dot_exact(y, _eye(y.shape[0]), 0, 0, 1)


_to_rows.defvjp(lambda x: (_to_rows(x), None), lambda _, ct: (_to_cols(ct),))
_to_cols.defvjp(lambda y: (_to_cols(y), None), lambda _, ct: (_to_rows(ct),))


def _pick_col(arr, idx):
    return jnp.sum(jnp.where(_lane_iota(arr) == idx, arr, 0.0), axis=1, keepdims=True)


def _pick_row(arr, idx):
    return jnp.sum(jnp.where(_row_iota(arr) == idx, arr, 0.0), axis=0, keepdims=True)


def _chunk_gates(gb, direction, rev):
    n = gb.shape[0]
    incl, strict = _tri_masks(n, rev)
    gc = _cumsum_rows(gb, rev)
    gc_rows = _to_rows(gc)
    lanes = [direction * N_HEADS + h for h in range(N_HEADS)]
    cols = [_pick_col(gc, ln) for ln in lanes]
    rows = [_pick_row(gc_rows, ln) for ln in lanes]
    betas = [_pick_col(gb, 2 * N_HEADS + ln) for ln in lanes]
    decays = [jnp.where(incl, jnp.exp(jnp.where(incl, c - r, 0.0)), 0.0) for c, r in zip(cols, rows)]
    return cols, betas, decays, strict


def _chunk_heads(chunks):
    hs = dict(qs=[], ks=[], vs=[], cols=[], betas=[], decays=[], stricts=[], lasts=[])
    for direction, (q, k, v, gb) in enumerate(chunks):
        rev = direction == 1
        cols, betas, decays, strict = _chunk_gates(gb, direction, rev)
        last_idx = 0 if rev else gb.shape[0] - 1
        hs["qs"] += _heads(q) if q is not None else []
        hs["ks"] += _heads(k)
        hs["vs"] += _heads(v) if v is not None else []
        hs["cols"] += cols
        hs["betas"] += betas
        hs["decays"] += decays
        hs["stricts"] += [strict] * N_HEADS
        hs["lasts"] += [_pick_row(c, last_idx) for c in cols]
    return hs


def _chunk_lmat(k_f, gb_f, k_b, gb_b):
    hs = _chunk_heads(((None, k_f, None, gb_f), (None, k_b, None, gb_b)))
    kk = [_dot(kh * b, kh, 1, 1) for kh, b in zip(hs["ks"], hs["betas"])]
    return tuple(jnp.where(s, x * d, 0.0) for s, x, d in zip(hs["stricts"], kk, hs["decays"]))


def _tri_inverse(lmats):
    n = lmats[0].shape[0]
    eye = _eye(n)
    powers = [-lm for lm in lmats]
    invs = [eye + p for p in powers]
    span = 2
    while span < n:
        powers = [_dot(p, p, 1, 0) for p in powers]
        steps = [_dot(p, i, 1, 0) for p, i in zip(powers, invs)]
        invs = [i + s for i, s in zip(invs, steps)]
        span *= 2
    return tuple(invs)


def _chunk_out(chunk_f, chunk_b, tmats, states):
    hs = _chunk_heads((chunk_f, chunk_b))
    qs, ks, vs, cols, betas, decays, lasts = (hs[key] for key in ("qs", "ks", "vs", "cols", "betas", "decays", "lasts"))
    us = [_dot(tm, vh * b, 1, 0) for tm, vh, b in zip(tmats, vs, betas)]
    ws = [_dot(tm, kh * b * jnp.exp(c), 1, 0) for tm, kh, b, c in zip(tmats, ks, betas, cols)]
    attns = [_dot(qh, kh, 1, 1) * d for qh, kh, d in zip(qs, ks, decays)]
    wss = [_dot(w, st, 1, 0) for w, st in zip(ws, states)]
    v_news = [u - x for u, x in zip(us, wss)]
    inter = [_dot(qh * jnp.exp(c), st, 1, 0) for qh, c, st in zip(qs, cols, states)]
    intra = [_dot(a, vn, 1, 0) for a, vn in zip(attns, v_news)]
    adds = [_dot(kh * jnp.exp(l - c), vn, 0, 0) for kh, l, c, vn in zip(ks, lasts, cols, v_news)]
    new_states = tuple(st * jnp.exp(l) + a for st, l, a in zip(states, lasts, adds))
    outs = [x + y for x, y in zip(inter, intra)]
    return jnp.concatenate(outs[:N_HEADS], axis=1), jnp.concatenate(outs[N_HEADS:], axis=1), new_states


BOTH = 2 * N_HEADS


def _gdn_specs(n, first_backwards):
    idx = [(lambda i: n - 1 - i) if (d == 0) == first_backwards else (lambda i: i) for d in range(2)]

    def both(shape_of_block, index_tail):
        return [pl.BlockSpec(shape_of_block, lambda i, f=f: (f(i),) + index_tail) for f in idx]

    return idx, both


def _gdn_fwd(qkvn, gb, t):
    n = t // CHUNK
    idx, both = _gdn_specs(n, False)

    def body(qf, kf, vf, qb, kb, vb, gbf, gbb, of_ref, ob_ref, sf_ref, sb_ref, tf_ref, tb_ref, state):
        @pl.when(pl.program_id(0) == 0)
        def _():
            state[...] = jnp.zeros_like(state)

        chunk_f = (qf[...], kf[...], vf[...], gbf[...])
        chunk_b = (qb[...], kb[...], vb[...], gbb[...])
        tmats = _tri_inverse(_chunk_lmat(chunk_f[1], chunk_f[3], chunk_b[1], chunk_b[3]))
        states = tuple(state[h] for h in range(BOTH))
        o_f, o_b, new_states = _chunk_out(chunk_f, chunk_b, tmats, states)
        of_ref[...] = o_f
        ob_ref[...] = o_b
        for h in range(BOTH):
            s_ref, t_ref = (sf_ref, tf_ref) if h < N_HEADS else (sb_ref, tb_ref)
            s_ref[0, h % N_HEADS] = states[h]
            t_ref[0, h % N_HEADS] = tmats[h]
            state[h] = new_states[h]

    qkv_specs = [pl.BlockSpec((CHUNK, D_MODEL), lambda i, f=f, p=p: (f(i), p)) for f in idx for p in range(3)]
    return pl.pallas_call(
        body, name="gdn_fwd", grid=(n,),
        in_specs=qkv_specs + both((CHUNK, LANES), (0,)),
        out_specs=both((CHUNK, D_MODEL), (0,)) + both((1, N_HEADS, HEAD_DIM, HEAD_DIM), (0, 0, 0))
        + both((1, N_HEADS, CHUNK, CHUNK), (0, 0, 0)),
        out_shape=[jax.ShapeDtypeStruct((t, D_MODEL), F32)] * 2
        + [jax.ShapeDtypeStruct((n, N_HEADS, HEAD_DIM, HEAD_DIM), F32)] * 2
        + [jax.ShapeDtypeStruct((n, N_HEADS, CHUNK, CHUNK), F32)] * 2,
        scratch_shapes=[pltpu.VMEM((BOTH, HEAD_DIM, HEAD_DIM), F32)],
        compiler_params=_params(("arbitrary",)),
    )(*([qkvn] * 6), gb, gb)


def _gdn_bwd(qkvn, gb, s_f, s_b, t_f, t_b, do, t):
    n = t // CHUNK
    idx, both = _gdn_specs(n, True)

    def body(qf, kf, vf, qb, kb, vb, gbf, gbb, sf_ref, sb_ref, tf_ref, tb_ref, dof, dob,
             dqf_ref, dqb_ref, dgf_ref, dgb_ref, dstate):
        @pl.when(pl.program_id(0) == 0)
        def _():
            dstate[...] = jnp.zeros_like(dstate)

        chunk_f = (qf[...], kf[...], vf[...], gbf[...])
        chunk_b = (qb[...], kb[...], vb[...], gbb[...])
        tmats = tuple((tf_ref if h < N_HEADS else tb_ref)[0, h % N_HEADS] for h in range(BOTH))
        states = tuple((sf_ref if h < N_HEADS else sb_ref)[0, h % N_HEADS] for h in range(BOTH))
        _, out_vjp = jax.vjp(_chunk_out, chunk_f, chunk_b, tmats, states)
        d_f, d_b, dtm, dst = out_vjp((dof[...], dob[...], tuple(dstate[h] for h in range(BOTH))))
        firsts = [_dot(tm, d, 0, 0) for tm, d in zip(tmats, dtm)]
        dlm = tuple(-_dot(x, tm, 1, 1) for x, tm in zip(firsts, tmats))
        _, lmat_vjp = jax.vjp(_chunk_lmat, chunk_f[1], chunk_f[3], chunk_b[1], chunk_b[3])
        dk_f, dg_f, dk_b, dg_b = lmat_vjp(dlm)
        dqf_ref[...] = jnp.concatenate([d_f[0], d_f[1] + dk_f, d_f[2]], axis=1)
        dqb_ref[...] = jnp.concatenate([d_b[0], d_b[1] + dk_b, d_b[2]], axis=1)
        dgf_ref[...] = d_f[3] + dg_f
        dgb_ref[...] = d_b[3] + dg_b
        for h in range(BOTH):
            dstate[h] = dst[h]

    qkv_specs = [pl.BlockSpec((CHUNK, D_MODEL), lambda i, f=f, p=p: (f(i), p)) for f in idx for p in range(3)]
    return pl.pallas_call(
        body, name="gdn_bwd", grid=(n,),
        in_specs=qkv_specs + both((CHUNK, LANES), (0,)) + both((1, N_HEADS, HEAD_DIM, HEAD_DIM), (0, 0, 0))
        + both((1, N_HEADS, CHUNK, CHUNK), (0, 0, 0)) + both((CHUNK, D_MODEL), (0,)),
        out_specs=both((CHUNK, 3 * D_MODEL), (0,)) + both((CHUNK, LANES), (0,)),
        out_shape=[jax.ShapeDtypeStruct((t, 3 * D_MODEL), F32)] * 2 + [jax.ShapeDtypeStruct((t, LANES), F32)] * 2,
        scratch_shapes=[pltpu.VMEM((BOTH, HEAD_DIM, HEAD_DIM), F32)],
        compiler_params=_params(("arbitrary",)),
    )(*([qkvn] * 6), gb, gb, s_f, s_b, t_f, t_b, do, do)


def _mesh_pos():
    return lax.axis_index("x"), lax.axis_index("y"), lax.axis_index("c")


def _all_gather(shards):
    na = len(shards)

    def body(*refs):
        x_refs, out_refs = refs[:na], refs[na:2 * na]
        send_sems, recv_sems, local_sems = refs[2 * na:]
        x, y, c = _mesh_pos()
        me, sibling = (x, y, c), (x, y, 1 - c)
        chips = [(1 - x, y), (x, 1 - y), (1 - x, 1 - y)]

        def block(a, px, py, pc):
            return out_refs[a].at[4 * px + 2 * py + pc]

        def copy(a, k, blk, to, src=None):
            return pltpu.make_async_remote_copy(
                src_ref=block(a, *blk) if src is None else src, dst_ref=block(a, *blk),
                send_sem=send_sems.at[a, k], recv_sem=recv_sems.at[a, k],
                device_id=to, device_id_type=pl.DeviceIdType.MESH)

        mine = [pltpu.make_async_copy(x_refs[a], block(a, *me), local_sems.at[a]) for a in range(na)]
        first, passed = [], []
        for a in range(na):
            mine[a].start()
            first.append(copy(a, 0, me, sibling, src=x_refs[a]))
            first += [copy(a, 1 + j, me, (*chip, c), src=x_refs[a]) for j, chip in enumerate(chips)]
        for cp in first:
            cp.start()
        for j, chip in enumerate(chips):
            for a in range(na):
                copy(a, 1 + j, (*chip, c), me).wait_recv()
                passed.append(copy(a, 4 + j, (*chip, c), sibling))
                passed[-1].start()
        for a in range(na):
            copy(a, 0, sibling, me).wait_recv()
            for j, chip in enumerate(chips):
                copy(a, 4 + j, (*chip, 1 - c), me).wait_recv()
        for cp in first + passed:
            cp.wait_send()
        for cp in mine:
            cp.wait()

    any_spec = pl.BlockSpec(memory_space=pl.ANY)
    return pl.pallas_call(
        body, name="weights_all_gather",
        out_shape=[jax.ShapeDtypeStruct((N_DEV,) + s.shape, s.dtype) for s in shards],
        in_specs=[any_spec] * na, out_specs=[any_spec] * na,
        scratch_shapes=[pltpu.SemaphoreType.DMA((na, N_DEV - 1)), pltpu.SemaphoreType.DMA((na, N_DEV - 1)),
                        pltpu.SemaphoreType.DMA((na,))],
    )(*shards)


def _scatter_blocks(blocks):
    na = len(blocks)

    def body(*refs):
        g_refs, land_refs = refs[:na], refs[na:2 * na]
        send_sems, recv_sems, local_sems = refs[2 * na:]
        x, y, c = _mesh_pos()
        mine = 4 * x + 2 * y + c
        local = [pltpu.make_async_copy(g_refs[a].at[mine], land_refs[a].at[mine], local_sems.at[a])
                 for a in range(na)]
        for cp in local:
            cp.start()
        sends, recvs = [], []
        for k in range(1, N_DEV):
            px = 1 - x if k & 4 else x
            py = 1 - y if k & 2 else y
            pc = 1 - c if k & 1 else c
            peer = 4 * px + 2 * py + pc
            for a in range(na):
                sends.append(pltpu.make_async_remote_copy(
                    src_ref=g_refs[a].at[peer], dst_ref=land_refs[a].at[mine],
                    send_sem=send_sems.at[a, k - 1], recv_sem=recv_sems.at[a, k - 1],
                    device_id=(px, py, pc), device_id_type=pl.DeviceIdType.MESH))
                recvs.append(pltpu.make_async_remote_copy(
                    src_ref=g_refs[a].at[mine], dst_ref=land_refs[a].at[peer],
                    send_sem=send_sems.at[a, k - 1], recv_sem=recv_sems.at[a, k - 1],
                    device_id=(x, y, c), device_id_type=pl.DeviceIdType.MESH))
        for cp in sends:
            cp.start()
        for cp in recvs:
            cp.wait_recv()
        for cp in sends:
            cp.wait_send()
        for cp in local:
            cp.wait()

    any_spec = pl.BlockSpec(memory_space=pl.ANY)
    return pl.pallas_call(
        body, name="grads_scatter",
        out_shape=[jax.ShapeDtypeStruct(b.shape, b.dtype) for b in blocks],
        in_specs=[any_spec] * na, out_specs=[any_spec] * na,
        scratch_shapes=[pltpu.SemaphoreType.DMA((na, N_DEV - 1)), pltpu.SemaphoreType.DMA((na, N_DEV - 1)),
                        pltpu.SemaphoreType.DMA((na,))],
    )(*blocks)


def _sum_slots(land, name):
    _, rows, cols = land.shape
    tr = _row_tile(rows, cols * 4 * N_DEV, budget=4 << 20)

    def body(*refs):
        acc = refs[0][0].astype(F32)
        for ref in refs[1:N_DEV]:
            acc = acc + ref[0].astype(F32)
        refs[N_DEV][...] = acc

    return pl.pallas_call(
        body, name=name, grid=(rows // tr,),
        in_specs=[pl.BlockSpec((1, tr, cols), lambda i, s=s: (s, i, 0)) for s in range(N_DEV)],
        out_specs=pl.BlockSpec((tr, cols), lambda i: (i, 0)),
        out_shape=jax.ShapeDtypeStruct((rows, cols), F32),
        compiler_params=_params(("parallel",)),
    )(*([land] * N_DEV))


def _row_tile(rows, row_bytes, budget=1 << 20):
    if rows * row_bytes <= budget or rows % SUBLANES:
        return rows
    best = SUBLANES
    for tr in range(SUBLANES, rows + 1, SUBLANES):
        if rows % tr == 0 and tr * row_bytes <= budget:
            best = tr
    return best


def _adamw(w, g, m, v, name):
    shape = w.shape
    cols = shape[-1]
    rows = w.size // cols
    tr = _row_tile(rows, cols * 4)
    b1, b2 = ADAM["b1"], ADAM["b2"]

    def body(w_ref, g_ref, m_ref, v_ref, d_ref, nm_ref, nv_ref):
        gv = g_ref[...]
        nm = b1 * m_ref[...] + (1.0 - b1) * gv
        nv = b2 * v_ref[...] + (1.0 - b2) * jnp.square(gv)
        m_hat = nm / (1.0 - b1 ** ADAM["step"])
        v_hat = nv / (1.0 - b2 ** ADAM["step"])
        d_ref[...] = -ADAM["lr"] * (m_hat / (jnp.sqrt(v_hat) + ADAM["eps"]) + ADAM["wd"] * w_ref[...])
        nm_ref[...] = nm
        nv_ref[...] = nv

    spec = pl.BlockSpec((tr, cols), lambda i: (i, 0))
    outs = pl.pallas_call(
        body, name=name, grid=(rows // tr,), in_specs=[spec] * 4, out_specs=[spec] * 3,
        out_shape=[jax.ShapeDtypeStruct((rows, cols), F32)] * 3, compiler_params=_params(("parallel",)),
    )(*[a.reshape(rows, cols) for a in (w, g, m, v)])
    return [o.reshape(shape) for o in outs]


MATRICES = ("w_in", "w_o_gdn", "w_o_sc", "w_out", "w_up", "w_down")
CONVS = ("conv_qkv", "conv_sc")
SHARDED = ("w_in", "conv_qkv", "w_o_gdn", "conv_sc", "w_o_sc", "w_out", "w_up", "w_down")
SMALL = ("a_log", "dt_bias", "gdn_norm_w", "ln1_g", "ln1_b", "b_up", "b_down", "ln2_g", "ln2_b")
COLUMN_SHARDED = ("w_in", "conv_qkv", "conv_sc", "w_up")
PACK_COLS = 1024


def _pack(parts, row_multiple):
    flat = jnp.concatenate(parts, axis=-1)
    unit = PACK_COLS * row_multiple
    pad = -flat.shape[-1] % unit
    flat = jnp.pad(flat, [(0, 0)] * (flat.ndim - 1) + [(0, pad)])
    return flat.reshape(flat.shape[:-1] + (flat.shape[-1] // PACK_COLS, PACK_COLS))


def _unshard(name, blocks):
    _, l, r, c = blocks.shape
    if name in COLUMN_SHARDED:
        return blocks.transpose(1, 2, 0, 3).reshape(l, r, N_DEV * c)
    return blocks.transpose(1, 0, 2, 3).reshape(l, N_DEV * r, c)


def _to_shards(name, full):
    l, r, c = full.shape
    if name in COLUMN_SHARDED:
        return full.reshape(l, r, N_DEV, c // N_DEV).transpose(2, 0, 1, 3).reshape(N_DEV, -1)
    return full.reshape(l, N_DEV, r // N_DEV, c).transpose(1, 0, 2, 3).reshape(N_DEV, -1)


def _gather_weights(shards):
    parts, layout = [], []
    for name in MATRICES[1:]:
        parts.append(shards[name].astype(BF16).reshape(-1))
        layout.append((name, shards[name].shape, 1))
    for name in CONVS:
        parts.append(jnp.stack(_split3(shards[name])).reshape(-1))
        layout.append((name, shards[name].shape, 3))
    w_in = shards["w_in"]
    w_in_all, rest_all = _all_gather([w_in.astype(BF16).reshape(-1, w_in.shape[-1]), _pack(parts, 16)])
    full = {"w_in": _unshard("w_in", w_in_all.reshape(N_DEV, *w_in.shape))}
    rest_all, off = rest_all.reshape(N_DEV, -1), 0
    for name, shape, pieces in layout:
        size = pieces * shape[0] * shape[1] * shape[2]
        blk = rest_all[:, off:off + size]
        off += size
        if pieces == 3:
            blk = jnp.sum(blk.reshape(N_DEV, 3, *shape).astype(F32), axis=1)
        full[name] = _unshard(name, blk.reshape(N_DEV, *shape))
    return full


def _unpack(flat, names, shapes):
    out, off = {}, 0
    for name in names:
        size = 1
        for s in shapes[name]:
            size *= s
        out[name] = flat[off:off + size].reshape(shapes[name])
        off += size
    return out


def _reduce_grads(full_grads, small_grads, shard_shapes):
    l, r, c = full_grads["w_in"].shape
    w_in = full_grads["w_in"].astype(BF16)
    w_in = w_in.reshape(l, r, N_DEV, c // N_DEV).transpose(2, 0, 1, 3).reshape(N_DEV, l * r, c // N_DEV)
    rest = _pack([_to_shards(name, full_grads[name].astype(BF16)) for name in SHARDED[1:]], LANES)
    small = jnp.concatenate([small_grads[name].reshape(-1) for name in SMALL])
    small = _pack([jnp.broadcast_to(small[None, :], (N_DEV, small.shape[0]))], SUBLANES)
    landed = _scatter_blocks([w_in, rest, small])
    out = {"w_in": _sum_slots(landed[0], "grads_sum_w_in").reshape(shard_shapes["w_in"])}
    out.update(_unpack(_sum_slots(landed[1], "grads_sum_rest").reshape(-1), SHARDED[1:], shard_shapes))
    out.update(_unpack(_sum_slots(landed[2], "grads_sum_small").reshape(-1), SMALL, shard_shapes))
    return out


def _lane_row(values):
    flat = values.reshape(1, -1)
    return jnp.pad(flat, ((0, 0), (0, LANES - flat.shape[1])))


def _forward_layer(x, x16, w, t):
    proj = _mm(x16, w["w_main"], "nn", "proj_main")
    proj_ab = _mm(x16, w["w_ab"], "nn", "proj_ab")
    conv_out, qkvn = _qkv_conv_fwd(proj, w["conv_qkv"], t)
    gb = _gating_fwd(proj_ab, w["a_log"], w["dt_bias"], t)
    o_f, o_b, s_f, s_b, t_f, t_b = _gdn_fwd(qkvn, gb, t)
    og = _gate_norm_fwd(o_f, o_b, proj, w["gdn_norm_w"], t)
    s = _sc_fwd(proj, w["conv_sc"], t)
    y_a = _mm(og, w["w_o_gdn"], "nn", "y_gdn")
    y_b = _mm(s, w["w_o_sc"], "nn", "y_sc")
    mixed = _mix_fwd(proj, y_a, y_b, t)
    r1 = _mm(mixed, w["w_out"], "nn", "out_proj")
    x1, x1_16 = _ln1_fwd(x, r1, w["ln1_g"], w["ln1_b"], t)
    hpre, h = _mm(x1_16, w["w_up"], "nn", "mlp_up", epilogue=_relu2_epilogue(w["b_up"]))
    r2 = _mm(h, w["w_down"], "nn", "mlp_down")
    x2, x2_16 = _ln2_fwd(x1, r2, w["b_down"], w["ln2_g"], w["ln2_b"], t)
    saved = dict(x=x, x16=x16, proj=proj, proj_ab=proj_ab, conv_out=conv_out, qkvn=qkvn, gb=gb, o_f=o_f, o_b=o_b,
                 s_f=s_f, s_b=s_b, t_f=t_f, t_b=t_b, og=og, s=s, y_a=y_a, y_b=y_b, mixed=mixed, r1=r1, x1=x1,
                 x1_16=x1_16, hpre=hpre, h=h, r2=r2)
    return x2, x2_16, saved


def _backward_layer(dx2, w, a, t):
    (dx1_a, dr2), (db_down, dg2, db2) = _ln2_bwd(a["x1"], a["r2"], w["b_down"], w["ln2_g"], w["ln2_b"], dx2, t)
    dhpre, db_up_rows = _mm(dr2, w["w_down"], "nt", "d_h", epilogue=_relu2_back_epilogue(a["hpre"], w["b_up"]))
    db_up = jnp.sum(db_up_rows, axis=0, keepdims=True)
    dw_down = _mm(a["h"], dr2, "tn", "dw_down")
    dx1 = _mm(dhpre, w["w_up"], "nt", "d_x1", addends=[(1.0, dx1_a)])
    dw_up = _mm(a["x1_16"], dhpre, "tn", "dw_up")
    (dx_a, dr1), (dg1, db1) = _ln1_bwd(a["x"], a["r1"], w["ln1_g"], w["ln1_b"], dx1, t)
    dmixed = _mm(dr1, w["w_out"], "nt", "d_mixed", out_dtype=BF16)
    dw_out = _mm(a["mixed"], dr1, "tn", "dw_out")
    dgates, dy_a, dy_b = _mix_bwd(a["proj"], a["y_a"], a["y_b"], dmixed, t)
    dog = _mm(dy_a, w["w_o_gdn"], "nt", "d_og", out_dtype=BF16)
    dw_o_gdn = _mm(a["og"], dy_a, "tn", "dw_o_gdn")
    ds = _mm(dy_b, w["w_o_sc"], "nt", "d_s")
    dw_o_sc = _mm(a["s"], dy_b, "tn", "dw_o_sc")
    dsc, dconv_sc = _sc_bwd(a["proj"], w["conv_sc"], ds, t)
    do, dz, dnorm_w = _gate_norm_bwd(a["o_f"], a["o_b"], a["proj"], w["gdn_norm_w"], dog, t)
    dq_f, dq_b, dgb_f, dgb_b = _gdn_bwd(a["qkvn"], a["gb"], a["s_f"], a["s_b"], a["t_f"], a["t_b"], do, t)
    dqkv, dconv_qkv = _qkv_conv_bwd(a["proj"], w["conv_qkv"], a["conv_out"], dq_f, dq_b, t)
    dab, da_log, ddt_bias = _gating_bwd(a["proj_ab"], w["a_log"], w["dt_bias"], dgb_f, dgb_b, t)

    pieces = [(dqkv, 0), (dz, 3 * D_MODEL), (dsc, 4 * D_MODEL), (dgates, 7 * D_MODEL)]
    dx = _mm(dab, w["w_ab"], "nt", "dx_ab", addends=[(1.0, dx_a)])
    dw_main = []
    for j, (piece, off) in enumerate(pieces):
        dx = _mm(piece, w["w_main"], "nt", f"dx_{j}", b_off=off, kdim=piece.shape[1], addends=[(1.0, dx)])
        dw_main.append(_mm(a["x16"], piece, "tn", f"dw_in_{j}"))
    dw_ab = _mm(a["x16"], dab, "tn", "dw_ab")
    dw_main = jnp.concatenate(dw_main, axis=1)
    dw_in = jnp.concatenate([dw_main[:, :QKVZ_COLS], dw_ab[:, :AB_COLS], dw_main[:, QKVZ_COLS:]], axis=1)
    grads = dict(w_in=dw_in, conv_qkv=dconv_qkv[:3], a_log=da_log[0, :2 * N_HEADS].reshape(2, N_HEADS),
                 dt_bias=ddt_bias[0, :2 * N_HEADS].reshape(2, N_HEADS), gdn_norm_w=dnorm_w[0], w_o_gdn=dw_o_gdn,
                 conv_sc=dconv_sc[:3], w_o_sc=dw_o_sc, w_out=dw_out, ln1_g=dg1[0], ln1_b=db1[0], w_up=dw_up,
                 b_up=db_up[0], w_down=dw_down, b_down=db_down[0], ln2_g=dg2[0], ln2_b=db2[0])
    return dx, grads


def kernel(x, w_in, conv_qkv, a_log, dt_bias, gdn_norm_w, w_o_gdn, conv_sc, w_o_sc, w_out, ln1_g, ln1_b, w_up, b_up, w_down, b_down, ln2_g, ln2_b, loss_target, m_w_in, m_conv_qkv, m_a_log, m_dt_bias, m_gdn_norm_w, m_w_o_gdn, m_conv_sc, m_w_o_sc, m_w_out, m_ln1_g, m_ln1_b, m_w_up, m_b_up, m_w_down, m_b_down, m_ln2_g, m_ln2_b, v_w_in, v_conv_qkv, v_a_log, v_dt_bias, v_gdn_norm_w, v_w_o_gdn, v_conv_sc, v_w_o_sc, v_w_out, v_ln1_g, v_ln1_b, v_w_up, v_b_up, v_w_down, v_b_down, v_ln2_g, v_ln2_b):
    weights = dict(w_in=w_in, conv_qkv=conv_qkv, a_log=a_log, dt_bias=dt_bias, gdn_norm_w=gdn_norm_w,
                   w_o_gdn=w_o_gdn, conv_sc=conv_sc, w_o_sc=w_o_sc, w_out=w_out, ln1_g=ln1_g, ln1_b=ln1_b,
                   w_up=w_up, b_up=b_up, w_down=w_down, b_down=b_down, ln2_g=ln2_g, ln2_b=ln2_b)
    m_in = dict(w_in=m_w_in, conv_qkv=m_conv_qkv, a_log=m_a_log, dt_bias=m_dt_bias, gdn_norm_w=m_gdn_norm_w,
                w_o_gdn=m_w_o_gdn, conv_sc=m_conv_sc, w_o_sc=m_w_o_sc, w_out=m_w_out, ln1_g=m_ln1_g, ln1_b=m_ln1_b,
                w_up=m_w_up, b_up=m_b_up, w_down=m_w_down, b_down=m_b_down, ln2_g=m_ln2_g, ln2_b=m_ln2_b)
    v_in = dict(w_in=v_w_in, conv_qkv=v_conv_qkv, a_log=v_a_log, dt_bias=v_dt_bias, gdn_norm_w=v_gdn_norm_w,
                w_o_gdn=v_w_o_gdn, conv_sc=v_conv_sc, w_o_sc=v_w_o_sc, w_out=v_w_out, ln1_g=v_ln1_g, ln1_b=v_ln1_b,
                w_up=v_w_up, b_up=v_b_up, w_down=v_w_down, b_down=v_b_down, ln2_g=v_ln2_g, ln2_b=v_ln2_b)
    t = x.shape[1]
    depth = w_in.shape[0]
    full = _gather_weights({name: weights[name] for name in MATRICES + CONVS})

    layers = []
    for l in range(depth):
        w_in_l = full["w_in"][l]
        layers.append(dict(
            w_main=jnp.concatenate([w_in_l[:, :QKVZ_COLS], w_in_l[:, QKVZ_COLS + AB_COLS:]], axis=1),
            w_ab=jnp.pad(w_in_l[:, QKVZ_COLS:QKVZ_COLS + AB_COLS], ((0, 0), (0, LANES - AB_COLS))),
            conv_qkv=jnp.pad(full["conv_qkv"][l], ((0, SUBLANES - 3), (0, 0))),
            conv_sc=jnp.pad(full["conv_sc"][l], ((0, SUBLANES - 3), (0, 0))),
            a_log=_lane_row(a_log[l]), dt_bias=_lane_row(dt_bias[l]), gdn_norm_w=gdn_norm_w[l][None, :],
            w_o_gdn=full["w_o_gdn"][l], w_o_sc=full["w_o_sc"][l], w_out=full["w_out"][l],
            ln1_g=ln1_g[l][None, :], ln1_b=ln1_b[l][None, :], w_up=full["w_up"][l], b_up=b_up[l][None, :],
            w_down=full["w_down"][l], b_down=b_down[l][None, :], ln2_g=ln2_g[l][None, :], ln2_b=ln2_b[l][None, :]))

    h = x.reshape(t, D_MODEL)
    h16 = h.astype(BF16)
    saved = []
    for l in range(depth):
        h, h16, acts = _forward_layer(h, h16, layers[l], t)
        saved.append(acts)
    dh, loss_local = _loss_stage(h, loss_target.reshape(t, D_MODEL), t)
    loss = lax.psum(loss_local, MESH_AXES)

    layer_grads = [None] * depth
    for l in reversed(range(depth)):
        dh, layer_grads[l] = _backward_layer(dh, layers[l], saved[l], t)
    stacked = {name: jnp.stack([g[name] for g in layer_grads]) for name in SHARDED + SMALL}
    shapes = {name: weights[name].shape for name in SHARDED + SMALL}
    grads = _reduce_grads({n: stacked[n] for n in SHARDED}, {n: stacked[n] for n in SMALL}, shapes)

    names = list(weights)
    updates = {n: _adamw(weights[n], grads[n], m_in[n], v_in[n], f"adamw_{n}") for n in names}
    return (loss, dh.reshape(x.shape), *[grads[n] for n in names], *[updates[n][0] for n in names],
            *[updates[n][1] for n in names], *[updates[n][2] for n in names])
```

```python
import functools

import jax
import jax.numpy as jnp
from jax import lax
from jax.experimental import pallas as pl
from jax.experimental.pallas import tpu as pltpu

F32 = jnp.float32
BF16 = jnp.bfloat16

D_MODEL = 1024
N_HEADS = 8
HEAD_DIM = 128
CHUNK = 64
D_FF = 4 * D_MODEL
DEPTH = 4
N_DEV = 8
LN_EPS = 1e-5
RMS_EPS = 1e-6
L2_EPS = 1e-6
ALPHA = (2 * DEPTH) ** 0.25
MAIN_COLS = 9 * D_MODEL
QKVZ_COLS = 4 * D_MODEL
AB_COLS = 4 * N_HEADS
W_IN_COLS = MAIN_COLS + AB_COLS
LANES = 128
SUBLANES = 8
VMEM_LIMIT = 48 * 1024 * 1024
MM_TILE = 1024
ROW_TILE = 256
WIDE_ROW_TILE = 128
ADAM = dict(lr=0.001, b1=0.9, b2=0.999, eps=1e-08, wd=0.01, step=10)
MESH_AXES = ("x", "y", "c")


def _params(sem=None):
    return pltpu.CompilerParams(dimension_semantics=sem, vmem_limit_bytes=VMEM_LIMIT)


_DIMS = {"nn": (1, 0), "nt": (1, 1), "tn": (0, 0)}


def _mm(a, b, mode, name, *, out_dtype=F32, b_off=0, kdim=None, addends=(), epilogue=None, exchange=None):
    if mode == "nn":
        (m, k), n = a.shape, b.shape[1]
    elif mode == "nt":
        (m, k), n = a.shape, b.shape[0]
        k = kdim or k
    else:
        (k, m), n = a.shape, b.shape[1]
    tm, tn, tk = min(m, MM_TILE), min(n, MM_TILE), min(k, MM_TILE)
    assert m % tm == 0 and n % tn == 0 and k % tk == 0 and b_off % tk == 0
    nk = k // tk
    koff = b_off // tk
    ca, cb = _DIMS[mode]
    scales = tuple(s for s, _ in addends)
    na = len(addends)
    epi_fn, epi_in, epi_out = epilogue or (None, (), (("tile", out_dtype),))
    ne, no = len(epi_in), len(epi_out)
    plan, x_arrays, x_shapes = exchange or (None, (), ())
    nx = len(x_arrays)
    gi, gj = m // tm, n // tn

    def body(a_ref, b_ref, *rest):
        add_refs, epi_refs = rest[:na], rest[na:na + ne]
        xin_refs, rest = rest[na + ne:na + ne + nx], rest[na + ne + nx:]
        o_refs, xout_refs, rest = rest[:no], rest[no:no + nx], rest[no + nx:]
        kk = pl.program_id(2)
        if plan:
            start, forward, finish_exchange = plan(xin_refs, xout_refs, *rest[-3:])
            i, j = pl.program_id(0), pl.program_id(1)

            @pl.when((i == 0) & (j == 0) & (kk == 0))
            def _():
                start()

            @pl.when((i == gi // 2) & (j == gj // 2) & (kk == 0))
            def _():
                forward()

        p = lax.dot_general(a_ref[...].astype(BF16), b_ref[...].astype(BF16), (((ca,), (cb,)), ((), ())),
                            preferred_element_type=F32)

        def finish(r):
            for s, ref in zip(scales, add_refs):
                r = r + s * ref[...].astype(F32)
            vals = epi_fn(r, *[ref[...] for ref in epi_refs]) if epi_fn else (r,)
            for (kind, _), ref, val in zip(epi_out, o_refs, vals):
                if kind == "colsum":
                    val = jnp.where(lax.broadcasted_iota(jnp.int32, ref.shape, 0) == 0, val, 0.0)
                ref[...] = val.astype(ref.dtype)

        if nk == 1:
            finish(p)
        else:
            acc = rest[0]

            @pl.when(kk == 0)
            def _():
                acc[...] = p

            @pl.when(kk > 0)
            def _():
                acc[...] += p

            @pl.when(kk == nk - 1)
            def _():
                finish(acc[...])

        if plan:
            @pl.when((i == gi - 1) & (j == gj - 1) & (kk == nk - 1))
            def _():
                finish_exchange()

    if mode == "nn":
        a_spec = pl.BlockSpec((tm, tk), lambda i, j, kk: (i, kk))
        b_spec = pl.BlockSpec((tk, tn), lambda i, j, kk: (kk, j))
    elif mode == "nt":
        a_spec = pl.BlockSpec((tm, tk), lambda i, j, kk: (i, kk))
        b_spec = pl.BlockSpec((tn, tk), lambda i, j, kk: (j, kk + koff))
    else:
        a_spec = pl.BlockSpec((tk, tm), lambda i, j, kk: (kk, i))
        b_spec = pl.BlockSpec((tk, tn), lambda i, j, kk: (kk, j))
    kinds = {"tile": (pl.BlockSpec((tm, tn), lambda i, j, kk: (i, j)), (m, n)),
             "row": (pl.BlockSpec((1, tn), lambda i, j, kk: (0, j)), (1, n)),
             "colsum": (pl.BlockSpec((SUBLANES, tn), lambda i, j, kk: (i, j)), (SUBLANES * (m // tm), n))}
    o_spec = kinds["tile"][0]
    any_spec = pl.BlockSpec(memory_space=pl.ANY)
    res = pl.pallas_call(
        body, name=name, grid=(gi, gj, nk),
        in_specs=[a_spec, b_spec] + [o_spec] * na + [kinds[kind][0] for _, kind in epi_in] + [any_spec] * nx,
        out_specs=[kinds[kind][0] for kind, _ in epi_out] + [any_spec] * nx,
        out_shape=[jax.ShapeDtypeStruct(kinds[kind][1], dt) for kind, dt in epi_out] + list(x_shapes),
        scratch_shapes=([pltpu.VMEM((tm, tn), F32)] if nk > 1 else []) + (_exchange_sems(nx) if plan else []),
        compiler_params=_params(("arbitrary",) * 3 if plan else ("parallel", "parallel", "arbitrary")),
    )(a, b, *[arr for _, arr in addends], *[arr for arr, _ in epi_in], *x_arrays)
    if plan:
        return (tuple(res[:no]) if epilogue else res[0]), tuple(res[no:])
    return res if epilogue else res[0]


def _tile_call(name, body, t, tm, tiled, halo, params, outs, accs):
    tm = min(tm, t)
    assert t % tm == 0 and tm % SUBLANES == 0
    steps = t // tm
    hb = tm // SUBLANES
    nt, nh, npar, no = len(tiled), len(halo), len(params), len(outs)

    def kern(*refs):
        i = pl.program_id(0)
        t_refs = refs[:nt + nh]
        h_refs = refs[nt + nh:nt + 3 * nh]
        p_refs = refs[nt + 3 * nh:nt + 3 * nh + npar]
        o_refs = refs[nt + 3 * nh + npar:nt + 3 * nh + npar + no]
        a_refs = refs[nt + 3 * nh + npar + no:]
        tiles = [r[...].astype(F32) for r in t_refs]
        halos = []
        for j in range(nh):
            prev = h_refs[2 * j][SUBLANES - 1:SUBLANES, :].astype(F32)
            nxt = h_refs[2 * j + 1][0:1, :].astype(F32)
            halos.append((jnp.where(i > 0, prev, 0.0), jnp.where(i < steps - 1, nxt, 0.0)))
        o_vals, a_vals = body(tiles, halos, [r[...] for r in p_refs])
        for ref, val in zip(o_refs, o_vals):
            ref[...] = val.astype(ref.dtype)
        for ref, val in zip(a_refs, a_vals):
            @pl.when(i == 0)
            def _(ref=ref, val=val):
                ref[...] = val

            @pl.when(i > 0)
            def _(ref=ref, val=val):
                ref[...] += val

    in_specs, args = [], []
    for arr, nc, cb in list(tiled) + list(halo):
        in_specs.append(pl.BlockSpec((tm, nc), lambda i, cb=cb: (i, cb)))
        args.append(arr)
    last = t // SUBLANES - 1
    for arr, nc, cb in halo:
        in_specs.append(pl.BlockSpec((SUBLANES, nc), lambda i, cb=cb: (jnp.maximum(i * hb - 1, 0), cb)))
        in_specs.append(pl.BlockSpec((SUBLANES, nc), lambda i, cb=cb: (jnp.minimum((i + 1) * hb, last), cb)))
        args += [arr, arr]
    for arr in params:
        in_specs.append(pl.BlockSpec(arr.shape, lambda i: (0, 0)))
        args.append(arr)
    out_specs = [pl.BlockSpec((tm, nc), lambda i: (i, 0)) for nc, _ in outs]
    out_specs += [pl.BlockSpec(shape, lambda i: (0, 0)) for shape in accs]
    out_shape = [jax.ShapeDtypeStruct((t, nc), dt) for nc, dt in outs]
    out_shape += [jax.ShapeDtypeStruct(shape, F32) for shape in accs]
    res = pl.pallas_call(kern, name=name, grid=(steps,), in_specs=in_specs, out_specs=out_specs,
                         out_shape=out_shape, compiler_params=_params(("arbitrary",)))(*args)
    return res[:no], res[no:]


def _row_iota(x):
    return lax.broadcasted_iota(jnp.int32, x.shape, 0)


def _lane_iota(x):
    return lax.broadcasted_iota(jnp.int32, x.shape, 1)


def _shift_down(x, first_row):
    return jnp.where(_row_iota(x) == 0, first_row, pltpu.roll(x, 1, 0))


def _shift_up(x, last_row):
    n = x.shape[0]
    return jnp.where(_row_iota(x) == n - 1, last_row, pltpu.roll(x, n - 1, 0))


def _taps(w):
    return w[0:1, :], w[1:2, :], w[2:3, :]


def _tap_rows(d0, d1, d2, rows=SUBLANES):
    r = lax.broadcasted_iota(jnp.int32, (rows, d0.shape[1]), 0)
    return jnp.where(r == 0, d0, jnp.where(r == 1, d1, jnp.where(r == 2, d2, 0.0)))


def _colsum(x):
    return jnp.sum(x, axis=0, keepdims=True)


def _silu(x):
    return x * jax.nn.sigmoid(x)


def _softplus(x):
    return jnp.maximum(x, 0.0) + jnp.log(1.0 + jnp.exp(-jnp.abs(x)))


def _heads(x):
    return [x[:, h * HEAD_DIM:(h + 1) * HEAD_DIM] for h in range(x.shape[1] // HEAD_DIM)]


def _post_conv(c):
    blocks = _heads(_silu(c))
    out = []
    for j, blk in enumerate(blocks):
        if j < 2 * N_HEADS:
            blk = blk * lax.rsqrt(jnp.sum(blk * blk, axis=-1, keepdims=True) + L2_EPS)
        if j < N_HEADS:
            blk = blk * (HEAD_DIM ** -0.5)
        out.append(blk)
    return jnp.concatenate(out, axis=1)


def _gating(ab, a_log, dt_bias):
    lane = _lane_iota(ab)
    g = -jnp.exp(a_log) * _softplus(ab + dt_bias)
    return jnp.where(lane < 2 * N_HEADS, g, jnp.where(lane < AB_COLS, jax.nn.sigmoid(ab), 0.0))


def _gate_norm(o_f, o_b, z, norm_w):
    out = []
    for oh, zh in zip(_heads(o_f + o_b), _heads(z)):
        out.append(oh * lax.rsqrt(jnp.mean(oh * oh, axis=-1, keepdims=True) + RMS_EPS) * norm_w * _silu(zh))
    return jnp.concatenate(out, axis=1)


def _mix(gate_a, gate_b, y_a, y_b):
    return jax.nn.sigmoid(gate_a) * y_a + jax.nn.sigmoid(gate_b) * y_b


def _layer_norm(u, g, b):
    mu = jnp.mean(u, axis=-1, keepdims=True)
    var = jnp.mean(jnp.square(u - mu), axis=-1, keepdims=True)
    return (u - mu) * lax.rsqrt(var + LN_EPS) * g + b


def _ln1(x, r, g, b):
    return _layer_norm(ALPHA * x + r, g, b)


def _ln2(x, r, bias, g, b):
    return _layer_norm(ALPHA * x + r + bias, g, b)


def _relu2(hpre, bias):
    return jnp.square(jnp.maximum(hpre + bias, 0.0))


def _qkv_conv_fwd(proj, conv_w, t):
    def body(tiles, halos, params):
        (x,), ((xp, xn),), (w,) = tiles, halos, params
        w0, w1, w2 = _taps(w)
        c = w0 * _shift_down(x, xp) + w1 * x + w2 * _shift_up(x, xn)
        return [c, _post_conv(c)], []

    (c, qkvn), _ = _tile_call("qkv_conv_fwd", body, t, WIDE_ROW_TILE, [], [(proj, 3 * D_MODEL, 0)], [conv_w],
                              [(3 * D_MODEL, F32), (3 * D_MODEL, F32)], [])
    return c, qkvn


def _gating_fwd(proj_ab, a_log, dt_bias, t):
    def body(tiles, halos, params):
        return [_gating(tiles[0], params[0], params[1])], []

    (gb,), _ = _tile_call("gating_fwd", body, t, ROW_TILE, [(proj_ab, LANES, 0)], [], [a_log, dt_bias],
                          [(LANES, F32)], [])
    return gb


def _gate_norm_fwd(o_f, o_b, proj, norm_w, t):
    def body(tiles, halos, params):
        return [_gate_norm(tiles[0], tiles[1], tiles[2], params[0])], []

    (og,), _ = _tile_call("gate_norm_fwd", body, t, ROW_TILE,
                          [(o_f, D_MODEL, 0), (o_b, D_MODEL, 0), (proj, D_MODEL, 3)], [], [norm_w],
                          [(D_MODEL, BF16)], [])
    return og


def _sc_fwd(proj, conv_w, t):
    def body(tiles, halos, params):
        (sb,), ((cp, cn), (xp, xn)), (w,) = tiles[:1], halos, params
        sc, sx = tiles[1], tiles[2]
        w0, w1, w2 = _taps(w)
        u = sc * sx
        return [sb * (w0 * _shift_down(u, cp * xp) + w1 * u + w2 * _shift_up(u, cn * xn))], []

    (s,), _ = _tile_call("sc_fwd", body, t, ROW_TILE, [(proj, D_MODEL, 4)],
                         [(proj, D_MODEL, 5), (proj, D_MODEL, 6)], [conv_w], [(D_MODEL, BF16)], [])
    return s


def _mix_fwd(proj, y_a, y_b, t):
    def body(tiles, halos, params):
        return [_mix(*tiles)], []

    (mixed,), _ = _tile_call("mix_fwd", body, t, ROW_TILE,
                             [(proj, D_MODEL, 7), (proj, D_MODEL, 8), (y_a, D_MODEL, 0), (y_b, D_MODEL, 0)], [], [],
                             [(D_MODEL, BF16)], [])
    return mixed


def _ln1_fwd(x, r, g, b, t):
    def body(tiles, halos, params):
        y = _ln1(tiles[0], tiles[1], params[0], params[1])
        return [y, y], []

    (y, y16), _ = _tile_call("ln1_fwd", body, t, ROW_TILE, [(x, D_MODEL, 0), (r, D_MODEL, 0)], [], [g, b],
                             [(D_MODEL, F32), (D_MODEL, BF16)], [])
    return y, y16


def _ln2_fwd(x, r, bias, g, b, t):
    def body(tiles, halos, params):
        y = _ln2(tiles[0], tiles[1], params[0], params[1], params[2])
        return [y, y], []

    (y, y16), _ = _tile_call("ln2_fwd", body, t, ROW_TILE, [(x, D_MODEL, 0), (r, D_MODEL, 0)], [], [bias, g, b],
                             [(D_MODEL, F32), (D_MODEL, BF16)], [])
    return y, y16


def _relu2_epilogue(bias):
    return (lambda r, b: (r, _relu2(r, b))), [(bias, "row")], [("tile", F32), ("tile", BF16)]


def _relu2_back_epilogue(hpre, bias):
    def fn(r, hp, b):
        _, vjp = jax.vjp(_relu2, hp, b)
        return vjp(r)

    return fn, [(hpre, "tile"), (bias, "row")], [("tile", BF16), ("colsum", F32)]


def _loss_stage(y, target, t):
    def body(tiles, halos, params):
        d = tiles[0] - tiles[1]
        part = 0.5 * jnp.sum(jnp.mean(d * d, axis=-1, keepdims=True), axis=0, keepdims=True)
        return [d * (1.0 / D_MODEL)], [jnp.broadcast_to(part, (1, LANES))]

    (dy,), (loss,) = _tile_call("loss", body, t, ROW_TILE, [(y, D_MODEL, 0), (target, D_MODEL, 0)], [], [],
                                [(D_MODEL, F32)], [(1, LANES)])
    return dy, loss[0, 0]


def _ln2_bwd(x, r, bias, g, b, dy, t):
    def body(tiles, halos, params):
        _, vjp = jax.vjp(_ln2, tiles[0], tiles[1], params[0], params[1], params[2])
        dx, dr, dbias, dg, db = vjp(tiles[2])
        return [dx, dr], [dbias, dg, db]

    return _tile_call("ln2_bwd", body, t, ROW_TILE, [(x, D_MODEL, 0), (r, D_MODEL, 0), (dy, D_MODEL, 0)], [],
                      [bias, g, b], [(D_MODEL, F32), (D_MODEL, BF16)], [(1, D_MODEL)] * 3)


def _ln1_bwd(x, r, g, b, dy, t):
    def body(tiles, halos, params):
        _, vjp = jax.vjp(_ln1, tiles[0], tiles[1], params[0], params[1])
        dx, dr, dg, db = vjp(tiles[2])
        return [dx, dr], [dg, db]

    return _tile_call("ln1_bwd", body, t, ROW_TILE, [(x, D_MODEL, 0), (r, D_MODEL, 0), (dy, D_MODEL, 0)], [],
                      [g, b], [(D_MODEL, F32), (D_MODEL, BF16)], [(1, D_MODEL)] * 2)


def _mix_bwd(proj, y_a, y_b, dmixed, t):
    def body(tiles, halos, params):
        _, vjp = jax.vjp(_mix, *tiles[:4])
        dga, dgb, dya, dyb = vjp(tiles[4])
        return [jnp.concatenate([dga, dgb], axis=1), dya, dyb], []

    (dgates, dya, dyb), _ = _tile_call(
        "mix_bwd", body, t, ROW_TILE,
        [(proj, D_MODEL, 7), (proj, D_MODEL, 8), (y_a, D_MODEL, 0), (y_b, D_MODEL, 0), (dmixed, D_MODEL, 0)], [], [],
        [(2 * D_MODEL, BF16), (D_MODEL, BF16), (D_MODEL, BF16)], [])
    return dgates, dya, dyb


def _sc_bwd(proj, conv_w, ds, t):
    def body(tiles, halos, params):
        ds_, sb, sc, sx = tiles
        (dsp, dsn), (sbp, sbn), (scp, scn), (sxp, sxn) = halos
        w0, w1, w2 = _taps(params[0])
        u = sc * sx
        u_prev, u_next = _shift_down(u, scp * sxp), _shift_up(u, scn * sxn)
        dconv = ds_ * sb
        du = w0 * _shift_up(dconv, dsn * sbn) + w1 * dconv + w2 * _shift_down(dconv, dsp * sbp)
        dsb = ds_ * (w0 * u_prev + w1 * u + w2 * u_next)
        dw = _tap_rows(_colsum(dconv * u_prev), _colsum(dconv * u), _colsum(dconv * u_next))
        return [jnp.concatenate([dsb, du * sx, du * sc], axis=1)], [dw]

    (dsc,), (dw,) = _tile_call("sc_bwd", body, t, ROW_TILE, [],
                               [(ds, D_MODEL, 0), (proj, D_MODEL, 4), (proj, D_MODEL, 5), (proj, D_MODEL, 6)],
                               [conv_w], [(3 * D_MODEL, BF16)], [(SUBLANES, D_MODEL)])
    return dsc, dw


def _gate_norm_bwd(o_f, o_b, proj, norm_w, dog, t):
    def body(tiles, halos, params):
        _, vjp = jax.vjp(_gate_norm, tiles[0], tiles[1], tiles[2], params[0])
        do, _, dz, dnw = vjp(tiles[3])
        return [do, dz], [dnw]

    (do, dz), (dnw,) = _tile_call(
        "gate_norm_bwd", body, t, ROW_TILE,
        [(o_f, D_MODEL, 0), (o_b, D_MODEL, 0), (proj, D_MODEL, 3), (dog, D_MODEL, 0)], [], [norm_w],
        [(D_MODEL, F32), (D_MODEL, BF16)], [(1, HEAD_DIM)])
    return do, dz, dnw


def _qkv_conv_bwd(proj, conv_w, c, dq_f, dq_b, t):
    def post_conv_back(cv, ct):
        _, vjp = jax.vjp(_post_conv, cv)
        return vjp(ct)[0]

    def body(tiles, halos, params):
        cv, df, db, x = tiles
        (cp, cn), (dfp, dfn), (dbp, dbn), (xp, xn) = halos
        w0, w1, w2 = _taps(params[0])
        d = post_conv_back(cv, df + db)
        d_prev, d_next = post_conv_back(cp, dfp + dbp), post_conv_back(cn, dfn + dbn)
        dx = w0 * _shift_up(d, d_next) + w1 * d + w2 * _shift_down(d, d_prev)
        dw = _tap_rows(_colsum(d * _shift_down(x, xp)), _colsum(d * x), _colsum(d * _shift_up(x, xn)))
        return [dx], [dw]

    wide = 3 * D_MODEL
    (dqkv,), (dw,) = _tile_call("qkv_conv_bwd", body, t, WIDE_ROW_TILE, [],
                                [(c, wide, 0), (dq_f, wide, 0), (dq_b, wide, 0), (proj, wide, 0)], [conv_w],
                                [(wide, BF16)], [(SUBLANES, wide)])
    return dqkv, dw


def _gating_bwd(proj_ab, a_log, dt_bias, dgb_f, dgb_b, t):
    def body(tiles, halos, params):
        _, vjp = jax.vjp(_gating, tiles[0], params[0], params[1])
        dab, dal, ddt = vjp(tiles[1] + tiles[2])
        return [dab], [dal, ddt]

    (dab,), (dal, ddt) = _tile_call("gating_bwd", body, t, ROW_TILE,
                                    [(proj_ab, LANES, 0), (dgb_f, LANES, 0), (dgb_b, LANES, 0)], [],
                                    [a_log, dt_bias], [(LANES, BF16)], [(1, LANES)] * 2)
    return dab, dal, ddt


@functools.partial(jax.custom_vjp, nondiff_argnums=(2, 3))
def _dot(a, b, ca, cb):
    return lax.dot_general(a.astype(BF16), b.astype(BF16), (((ca,), (cb,)), ((), ())), preferred_element_type=F32)


def _dot_fwd(a, b, ca, cb):
    return _dot(a, b, ca, cb), (a, b)


def _dot_bwd(ca, cb, res, ct):
    a, b = res
    fa, fb = 1 - ca, 1 - cb
    da = _dot(ct, b, 1, fb) if ca == 1 else _dot(b, ct, fb, 1)
    db = _dot(a, ct, fa, 0) if cb == 0 else _dot(ct, a, 0, fa)
    return da, db


_dot.defvjp(_dot_fwd, _dot_bwd)


def _split3(x):
    hi = x.astype(BF16)
    r1 = x - hi.astype(F32)
    mid = r1.astype(BF16)
    return hi, mid, (r1 - mid.astype(F32)).astype(BF16)


def _dot_exact(a, b, ca, cb, exact):
    dims = (((ca,), (cb,)), ((), ()))
    if exact == 0:
        return sum(lax.dot_general(a.astype(BF16), p, dims, preferred_element_type=F32) for p in _split3(b))
    return sum(lax.dot_general(p, b.astype(BF16), dims, preferred_element_type=F32) for p in _split3(a))


def _tri_masks(n, rev):
    r = lax.broadcasted_iota(jnp.int32, (n, n), 0)
    c = lax.broadcasted_iota(jnp.int32, (n, n), 1)
    return ((c >= r), (c > r)) if rev else ((c <= r), (c < r))


@functools.partial(jax.custom_vjp, nondiff_argnums=(1,))
def _cumsum_rows(g, rev):
    incl, _ = _tri_masks(g.shape[0], rev)
    return _dot_exact(incl.astype(F32), g, 1, 0, 0)


_cumsum_rows.defvjp(lambda g, rev: (_cumsum_rows(g, rev), None),
                    lambda rev, _, ct: (_cumsum_rows(ct, not rev),))


def _eye(n):
    return (lax.broadcasted_iota(jnp.int32, (n, n), 0) == lax.broadcasted_iota(jnp.int32, (n, n), 1)).astype(F32)


@jax.custom_vjp
def _to_rows(x):
    return _dot_exact(_eye(x.shape[1]), x, 1, 1, 0)


@jax.custom_vjp
def _to_cols(y):
    return _dot_exact(y, _eye(y.shape[0]), 0, 0, 1)


_to_rows.defvjp(lambda x: (_to_rows(x), None), lambda _, ct: (_to_cols(ct),))
_to_cols.defvjp(lambda y: (_to_cols(y), None), lambda _, ct: (_to_rows(ct),))


def _pick_col(arr, idx):
    return jnp.sum(jnp.where(_lane_iota(arr) == idx, arr, 0.0), axis=1, keepdims=True)


def _pick_row(arr, idx):
    return jnp.sum(jnp.where(_row_iota(arr) == idx, arr, 0.0), axis=0, keepdims=True)


def _chunk_gates(gb, direction, rev):
    n = gb.shape[0]
    incl, strict = _tri_masks(n, rev)
    gc = _cumsum_rows(gb, rev)
    gc_rows = _to_rows(gc)
    lanes = [direction * N_HEADS + h for h in range(N_HEADS)]
    cols = [_pick_col(gc, ln) for ln in lanes]
    rows = [_pick_row(gc_rows, ln) for ln in lanes]
    betas = [_pick_col(gb, 2 * N_HEADS + ln) for ln in lanes]
    decays = [jnp.where(incl, jnp.exp(jnp.where(incl, c - r, 0.0)), 0.0) for c, r in zip(cols, rows)]
    return cols, betas, decays, strict


def _chunk_heads(chunks):
    hs = dict(qs=[], ks=[], vs=[], cols=[], betas=[], decays=[], stricts=[], lasts=[])
    for direction, (q, k, v, gb) in enumerate(chunks):
        rev = direction == 1
        cols, betas, decays, strict = _chunk_gates(gb, direction, rev)
        last_idx = 0 if rev else gb.shape[0] - 1
        hs["qs"] += _heads(q) if q is not None else []
        hs["ks"] += _heads(k)
        hs["vs"] += _heads(v) if v is not None else []
        hs["cols"] += cols
        hs["betas"] += betas
        hs["decays"] += decays
        hs["stricts"] += [strict] * N_HEADS
        hs["lasts"] += [_pick_row(c, last_idx) for c in cols]
    return hs


def _chunk_lmat(k_f, gb_f, k_b, gb_b):
    hs = _chunk_heads(((None, k_f, None, gb_f), (None, k_b, None, gb_b)))
    kk = [_dot(kh * b, kh, 1, 1) for kh, b in zip(hs["ks"], hs["betas"])]
    return tuple(jnp.where(s, x * d, 0.0) for s, x, d in zip(hs["stricts"], kk, hs["decays"]))


def _tri_inverse(lmats):
    n = lmats[0].shape[0]
    eye = _eye(n)
    powers = [-lm for lm in lmats]
    invs = [eye + p for p in powers]
    span = 2
    while span < n:
        powers = [_dot(p, p, 1, 0) for p in powers]
        steps = [_dot(p, i, 1, 0) for p, i in zip(powers, invs)]
        invs = [i + s for i, s in zip(invs, steps)]
        span *= 2
    return tuple(invs)


def _chunk_out(chunk_f, chunk_b, tmats, states):
    hs = _chunk_heads((chunk_f, chunk_b))
    qs, ks, vs, cols, betas, decays, lasts = (hs[key] for key in ("qs", "ks", "vs", "cols", "betas", "decays", "lasts"))
    us = [_dot(tm, vh * b, 1, 0) for tm, vh, b in zip(tmats, vs, betas)]
    ws = [_dot(tm, kh * b * jnp.exp(c), 1, 0) for tm, kh, b, c in zip(tmats, ks, betas, cols)]
    attns = [_dot(qh, kh, 1, 1) * d for qh, kh, d in zip(qs, ks, decays)]
    wss = [_dot(w, st, 1, 0) for w, st in zip(ws, states)]
    v_news = [u - x for u, x in zip(us, wss)]
    inter = [_dot(qh * jnp.exp(c), st, 1, 0) for qh, c, st in zip(qs, cols, states)]
    intra = [_dot(a, vn, 1, 0) for a, vn in zip(attns, v_news)]
    adds = [_dot(kh * jnp.exp(l - c), vn, 0, 0) for kh, l, c, vn in zip(ks, lasts, cols, v_news)]
    new_states = tuple(st * jnp.exp(l) + a for st, l, a in zip(states, lasts, adds))
    outs = [x + y for x, y in zip(inter, intra)]
    return jnp.concatenate(outs[:N_HEADS], axis=1), jnp.concatenate(outs[N_HEADS:], axis=1), new_states


BOTH = 2 * N_HEADS


def _gdn_specs(n, first_backwards):
    idx = [(lambda i: n - 1 - i) if (d == 0) == first_backwards else (lambda i: i) for d in range(2)]

    def both(shape_of_block, index_tail):
        return [pl.BlockSpec(shape_of_block, lambda i, f=f: (f(i),) + index_tail) for f in idx]

    return idx, both


def _gdn_fwd(qkvn, gb, t):
    n = t // CHUNK
    idx, both = _gdn_specs(n, False)

    def body(qf, kf, vf, qb, kb, vb, gbf, gbb, of_ref, ob_ref, sf_ref, sb_ref, tf_ref, tb_ref, state):
        @pl.when(pl.program_id(0) == 0)
        def _():
            state[...] = jnp.zeros_like(state)

        chunk_f = (qf[...], kf[...], vf[...], gbf[...])
        chunk_b = (qb[...], kb[...], vb[...], gbb[...])
        tmats = _tri_inverse(_chunk_lmat(chunk_f[1], chunk_f[3], chunk_b[1], chunk_b[3]))
        states = tuple(state[h] for h in range(BOTH))
        o_f, o_b, new_states = _chunk_out(chunk_f, chunk_b, tmats, states)
        of_ref[...] = o_f
        ob_ref[...] = o_b
        for h in range(BOTH):
            s_ref, t_ref = (sf_ref, tf_ref) if h < N_HEADS else (sb_ref, tb_ref)
            s_ref[0, h % N_HEADS] = states[h]
            t_ref[0, h % N_HEADS] = tmats[h]
            state[h] = new_states[h]

    qkv_specs = [pl.BlockSpec((CHUNK, D_MODEL), lambda i, f=f, p=p: (f(i), p)) for f in idx for p in range(3)]
    return pl.pallas_call(
        body, name="gdn_fwd", grid=(n,),
        in_specs=qkv_specs + both((CHUNK, LANES), (0,)),
        out_specs=both((CHUNK, D_MODEL), (0,)) + both((1, N_HEADS, HEAD_DIM, HEAD_DIM), (0, 0, 0))
        + both((1, N_HEADS, CHUNK, CHUNK), (0, 0, 0)),
        out_shape=[jax.ShapeDtypeStruct((t, D_MODEL), F32)] * 2
        + [jax.ShapeDtypeStruct((n, N_HEADS, HEAD_DIM, HEAD_DIM), F32)] * 2
        + [jax.ShapeDtypeStruct((n, N_HEADS, CHUNK, CHUNK), F32)] * 2,
        scratch_shapes=[pltpu.VMEM((BOTH, HEAD_DIM, HEAD_DIM), F32)],
        compiler_params=_params(("arbitrary",)),
    )(*([qkvn] * 6), gb, gb)


def _gdn_bwd(qkvn, gb, s_f, s_b, t_f, t_b, do, t):
    n = t // CHUNK
    idx, both = _gdn_specs(n, True)

    def body(qf, kf, vf, qb, kb, vb, gbf, gbb, sf_ref, sb_ref, tf_ref, tb_ref, dof, dob,
             dqf_ref, dqb_ref, dgf_ref, dgb_ref, dstate):
        @pl.when(pl.program_id(0) == 0)
        def _():
            dstate[...] = jnp.zeros_like(dstate)

        chunk_f = (qf[...], kf[...], vf[...], gbf[...])
        chunk_b = (qb[...], kb[...], vb[...], gbb[...])
        tmats = tuple((tf_ref if h < N_HEADS else tb_ref)[0, h % N_HEADS] for h in range(BOTH))
        states = tuple((sf_ref if h < N_HEADS else sb_ref)[0, h % N_HEADS] for h in range(BOTH))
        _, out_vjp = jax.vjp(_chunk_out, chunk_f, chunk_b, tmats, states)
        d_f, d_b, dtm, dst = out_vjp((dof[...], dob[...], tuple(dstate[h] for h in range(BOTH))))
        firsts = [_dot(tm, d, 0, 0) for tm, d in zip(tmats, dtm)]
        dlm = tuple(-_dot(x, tm, 1, 1) for x, tm in zip(firsts, tmats))
        _, lmat_vjp = jax.vjp(_chunk_lmat, chunk_f[1], chunk_f[3], chunk_b[1], chunk_b[3])
        dk_f, dg_f, dk_b, dg_b = lmat_vjp(dlm)
        dqf_ref[...] = jnp.concatenate([d_f[0], d_f[1] + dk_f, d_f[2]], axis=1)
        dqb_ref[...] = jnp.concatenate([d_b[0], d_b[1] + dk_b, d_b[2]], axis=1)
        dgf_ref[...] = d_f[3] + dg_f
        dgb_ref[...] = d_b[3] + dg_b
        for h in range(BOTH):
            dstate[h] = dst[h]

    qkv_specs = [pl.BlockSpec((CHUNK, D_MODEL), lambda i, f=f, p=p: (f(i), p)) for f in idx for p in range(3)]
    return pl.pallas_call(
        body, name="gdn_bwd", grid=(n,),
        in_specs=qkv_specs + both((CHUNK, LANES), (0,)) + both((1, N_HEADS, HEAD_DIM, HEAD_DIM), (0, 0, 0))
        + both((1, N_HEADS, CHUNK, CHUNK), (0, 0, 0)) + both((CHUNK, D_MODEL), (0,)),
        out_specs=both((CHUNK, 3 * D_MODEL), (0,)) + both((CHUNK, LANES), (0,)),
        out_shape=[jax.ShapeDtypeStruct((t, 3 * D_MODEL), F32)] * 2 + [jax.ShapeDtypeStruct((t, LANES), F32)] * 2,
        scratch_shapes=[pltpu.VMEM((BOTH, HEAD_DIM, HEAD_DIM), F32)],
        compiler_params=_params(("arbitrary",)),
    )(*([qkvn] * 6), gb, gb, s_f, s_b, t_f, t_b, do, do)


def _mesh_pos():
    return lax.axis_index("x"), lax.axis_index("y"), lax.axis_index("c")


def _exchange_sems(na):
    return [pltpu.SemaphoreType.DMA((na, N_DEV - 1)), pltpu.SemaphoreType.DMA((na, N_DEV - 1)),
            pltpu.SemaphoreType.DMA((na,))]


def _gather_plan(x_refs, out_refs, send_sems, recv_sems, local_sems):
    na = len(x_refs)
    x, y, c = _mesh_pos()
    me, sibling = (x, y, c), (x, y, 1 - c)
    chips = [(1 - x, y), (x, 1 - y), (1 - x, 1 - y)]

    def block(a, px, py, pc):
        return out_refs[a].at[4 * px + 2 * py + pc]

    def copy(a, k, blk, to, src=None):
        return pltpu.make_async_remote_copy(
            src_ref=block(a, *blk) if src is None else src, dst_ref=block(a, *blk),
            send_sem=send_sems.at[a, k], recv_sem=recv_sems.at[a, k],
            device_id=to, device_id_type=pl.DeviceIdType.MESH)

    def mine(a):
        return pltpu.make_async_copy(x_refs[a], block(a, *me), local_sems.at[a])

    def first(a):
        return [copy(a, 0, me, sibling, src=x_refs[a])] + [
            copy(a, 1 + j, me, (*chip, c), src=x_refs[a]) for j, chip in enumerate(chips)]

    def passed(a, j):
        return copy(a, 4 + j, (*chips[j], c), sibling)

    def start():
        for a in range(na):
            mine(a).start()
            for cp in first(a):
                cp.start()

    def forward():
        for j, chip in enumerate(chips):
            for a in range(na):
                copy(a, 1 + j, (*chip, c), me).wait_recv()
                passed(a, j).start()

    def finish():
        for a in range(na):
            copy(a, 0, sibling, me).wait_recv()
            for j, chip in enumerate(chips):
                copy(a, 4 + j, (*chip, 1 - c), me).wait_recv()
        for a in range(na):
            for cp in first(a) + [passed(a, j) for j in range(len(chips))]:
                cp.wait_send()
            mine(a).wait()

    return start, forward, finish


def _scatter_plan(g_refs, land_refs, send_sems, recv_sems, local_sems):
    na = len(g_refs)
    x, y, c = _mesh_pos()
    mine = 4 * x + 2 * y + c

    def local(a):
        return pltpu.make_async_copy(g_refs[a].at[mine], land_refs[a].at[mine], local_sems.at[a])

    def peers():
        for k in range(1, N_DEV):
            px = 1 - x if k & 4 else x
            py = 1 - y if k & 2 else y
            pc = 1 - c if k & 1 else c
            yield k, (px, py, pc), 4 * px + 2 * py + pc

    def send(a, k, to, peer):
        return pltpu.make_async_remote_copy(
            src_ref=g_refs[a].at[peer], dst_ref=land_refs[a].at[mine],
            send_sem=send_sems.at[a, k - 1], recv_sem=recv_sems.at[a, k - 1],
            device_id=to, device_id_type=pl.DeviceIdType.MESH)

    def recv(a, k, peer):
        return pltpu.make_async_remote_copy(
            src_ref=g_refs[a].at[mine], dst_ref=land_refs[a].at[peer],
            send_sem=send_sems.at[a, k - 1], recv_sem=recv_sems.at[a, k - 1],
            device_id=(x, y, c), device_id_type=pl.DeviceIdType.MESH)

    def start():
        for a in range(na):
            local(a).start()
        for k, to, peer in peers():
            for a in range(na):
                send(a, k, to, peer).start()

    def finish():
        for k, to, peer in peers():
            for a in range(na):
                recv(a, k, peer).wait_recv()
        for k, to, peer in peers():
            for a in range(na):
                send(a, k, to, peer).wait_send()
        for a in range(na):
            local(a).wait()

    return start, (lambda: None), finish


def _exchange_call(name, plan, arrays, out_shapes):
    na = len(arrays)

    def body(*refs):
        start, forward, finish = plan(refs[:na], refs[na:2 * na], *refs[2 * na:])
        start()
        forward()
        finish()

    any_spec = pl.BlockSpec(memory_space=pl.ANY)
    return pl.pallas_call(body, name=name, out_shape=list(out_shapes), in_specs=[any_spec] * na,
                          out_specs=[any_spec] * na, scratch_shapes=_exchange_sems(na))(*arrays)


def _gathered_shapes(shards):
    return [jax.ShapeDtypeStruct((N_DEV,) + s.shape, s.dtype) for s in shards]


def _landed_shapes(blocks):
    return [jax.ShapeDtypeStruct(b.shape, b.dtype) for b in blocks]


def _sum_slots(land, name):
    _, rows, cols = land.shape
    tr = _row_tile(rows, cols * 4 * N_DEV, budget=4 << 20)

    def body(*refs):
        acc = refs[0][0].astype(F32)
        for ref in refs[1:N_DEV]:
            acc = acc + ref[0].astype(F32)
        refs[N_DEV][...] = acc

    return pl.pallas_call(
        body, name=name, grid=(rows // tr,),
        in_specs=[pl.BlockSpec((1, tr, cols), lambda i, s=s: (s, i, 0)) for s in range(N_DEV)],
        out_specs=pl.BlockSpec((tr, cols), lambda i: (i, 0)),
        out_shape=jax.ShapeDtypeStruct((rows, cols), F32),
        compiler_params=_params(("parallel",)),
    )(*([land] * N_DEV))


def _row_tile(rows, row_bytes, budget=1 << 20):
    if rows * row_bytes <= budget or rows % SUBLANES:
        return rows
    best = SUBLANES
    for tr in range(SUBLANES, rows + 1, SUBLANES):
        if rows % tr == 0 and tr * row_bytes <= budget:
            best = tr
    return best


def _adamw(w, g, m, v, name):
    shape = w.shape
    cols = shape[-1]
    rows = w.size // cols
    tr = _row_tile(rows, cols * 4)
    b1, b2 = ADAM["b1"], ADAM["b2"]

    def body(w_ref, g_ref, m_ref, v_ref, d_ref, nm_ref, nv_ref):
        gv = g_ref[...]
        nm = b1 * m_ref[...] + (1.0 - b1) * gv
        nv = b2 * v_ref[...] + (1.0 - b2) * jnp.square(gv)
        m_hat = nm / (1.0 - b1 ** ADAM["step"])
        v_hat = nv / (1.0 - b2 ** ADAM["step"])
        d_ref[...] = -ADAM["lr"] * (m_hat / (jnp.sqrt(v_hat) + ADAM["eps"]) + ADAM["wd"] * w_ref[...])
        nm_ref[...] = nm
        nv_ref[...] = nv

    spec = pl.BlockSpec((tr, cols), lambda i: (i, 0))
    outs = pl.pallas_call(
        body, name=name, grid=(rows // tr,), in_specs=[spec] * 4, out_specs=[spec] * 3,
        out_shape=[jax.ShapeDtypeStruct((rows, cols), F32)] * 3, compiler_params=_params(("parallel",)),
    )(*[a.reshape(rows, cols) for a in (w, g, m, v)])
    return [o.reshape(shape) for o in outs]


MATRICES = ("w_in", "w_o_gdn", "w_o_sc", "w_out", "w_up", "w_down")
CONVS = ("conv_qkv", "conv_sc")
SHARDED = ("w_in", "conv_qkv", "w_o_gdn", "conv_sc", "w_o_sc", "w_out", "w_up", "w_down")
SMALL = ("a_log", "dt_bias", "gdn_norm_w", "ln1_g", "ln1_b", "b_up", "b_down", "ln2_g", "ln2_b")
COLUMN_SHARDED = ("w_in", "conv_qkv", "conv_sc", "w_up")
PACK_COLS = 1024


def _pack(parts, row_multiple):
    flat = jnp.concatenate(parts, axis=-1)
    unit = PACK_COLS * row_multiple
    pad = -flat.shape[-1] % unit
    flat = jnp.pad(flat, [(0, 0)] * (flat.ndim - 1) + [(0, pad)])
    return flat.reshape(flat.shape[:-1] + (flat.shape[-1] // PACK_COLS, PACK_COLS))


def _unshard(name, blocks):
    _, l, r, c = blocks.shape
    if name in COLUMN_SHARDED:
        return blocks.transpose(1, 2, 0, 3).reshape(l, r, N_DEV * c)
    return blocks.transpose(1, 0, 2, 3).reshape(l, N_DEV * r, c)


def _to_shards(name, full):
    l, r, c = full.shape
    if name in COLUMN_SHARDED:
        return full.reshape(l, r, N_DEV, c // N_DEV).transpose(2, 0, 1, 3).reshape(N_DEV, -1)
    return full.reshape(l, N_DEV, r // N_DEV, c).transpose(1, 0, 2, 3).reshape(N_DEV, -1)


def _pack_weights(shards):
    parts, layout = [], []
    for name in MATRICES[1:]:
        parts.append(shards[name].astype(BF16).reshape(-1))
        layout.append((name, shards[name].shape, 1))
    for name in CONVS:
        parts.append(jnp.stack(_split3(shards[name])).reshape(-1))
        layout.append((name, shards[name].shape, 3))
    w_in = shards["w_in"]
    return [w_in.astype(BF16).reshape(-1, w_in.shape[-1]), _pack(parts, 16)], (layout, w_in.shape)


def _unpack_weights(gathered, meta):
    (w_in_all, rest_all), (layout, w_in_shape) = gathered, meta
    full = {"w_in": _unshard("w_in", w_in_all.reshape(N_DEV, *w_in_shape))}
    rest_all, off = rest_all.reshape(N_DEV, -1), 0
    for name, shape, pieces in layout:
        size = pieces * shape[0] * shape[1] * shape[2]
        blk = rest_all[:, off:off + size]
        off += size
        if pieces == 3:
            blk = jnp.sum(blk.reshape(N_DEV, 3, *shape).astype(F32), axis=1)
        full[name] = _unshard(name, blk.reshape(N_DEV, *shape))
    return full


def _unpack(flat, names, shapes):
    out, off = {}, 0
    for name in names:
        size = 1
        for s in shapes[name]:
            size *= s
        out[name] = flat[off:off + size].reshape(shapes[name])
        off += size
    return out


def _pack_layer_grads(grads):
    r, c = grads["w_in"].shape
    w_in = grads["w_in"].astype(BF16).reshape(r, N_DEV, c // N_DEV).transpose(1, 0, 2)
    rest = _pack([_to_shards(name, grads[name].astype(BF16)[None]) for name in SHARDED[1:]], LANES)
    return [w_in, rest]


def _pack_small_grads(small_grads):
    small = jnp.concatenate([small_grads[name].reshape(-1) for name in SMALL])
    return _pack([jnp.broadcast_to(small[None, :], (N_DEV, small.shape[0]))], SUBLANES)


def _sum_layer_grads(landed, layer_shapes):
    out = {"w_in": _sum_slots(landed[0], "grads_sum_w_in")}
    shapes = {name: (1,) + layer_shapes[name] for name in SHARDED[1:]}
    rest = _unpack(_sum_slots(landed[1], "grads_sum_rest").reshape(-1), SHARDED[1:], shapes)
    out.update({name: val[0] for name, val in rest.items()})
    return out


def _lane_row(values):
    flat = values.reshape(1, -1)
    return jnp.pad(flat, ((0, 0), (0, LANES - flat.shape[1])))


def _forward_layer(x, x16, w, t, next_arrays):
    if next_arrays is None:
        proj, gathered = _mm(x16, w["w_main"], "nn", "proj_main"), None
    else:
        proj, gathered = _mm(x16, w["w_main"], "nn", "proj_main_gather",
                             exchange=(_gather_plan, next_arrays, _gathered_shapes(next_arrays)))
    proj_ab = _mm(x16, w["w_ab"], "nn", "proj_ab")
    conv_out, qkvn = _qkv_conv_fwd(proj, w["conv_qkv"], t)
    gb = _gating_fwd(proj_ab, w["a_log"], w["dt_bias"], t)
    o_f, o_b, s_f, s_b, t_f, t_b = _gdn_fwd(qkvn, gb, t)
    og = _gate_norm_fwd(o_f, o_b, proj, w["gdn_norm_w"], t)
    s = _sc_fwd(proj, w["conv_sc"], t)
    y_a = _mm(og, w["w_o_gdn"], "nn", "y_gdn")
    y_b = _mm(s, w["w_o_sc"], "nn", "y_sc")
    mixed = _mix_fwd(proj, y_a, y_b, t)
    r1 = _mm(mixed, w["w_out"], "nn", "out_proj")
    x1, x1_16 = _ln1_fwd(x, r1, w["ln1_g"], w["ln1_b"], t)
    hpre, h = _mm(x1_16, w["w_up"], "nn", "mlp_up", epilogue=_relu2_epilogue(w["b_up"]))
    r2 = _mm(h, w["w_down"], "nn", "mlp_down")
    x2, x2_16 = _ln2_fwd(x1, r2, w["b_down"], w["ln2_g"], w["ln2_b"], t)
    saved = dict(x=x, x16=x16, proj=proj, proj_ab=proj_ab, conv_out=conv_out, qkvn=qkvn, gb=gb, o_f=o_f, o_b=o_b,
                 s_f=s_f, s_b=s_b, t_f=t_f, t_b=t_b, og=og, s=s, y_a=y_a, y_b=y_b, mixed=mixed, r1=r1, x1=x1,
                 x1_16=x1_16, hpre=hpre, h=h, r2=r2)
    return x2, x2_16, saved, gathered


def _backward_layer(dx2, w, a, t, pending):
    (dx1_a, dr2), (db_down, dg2, db2) = _ln2_bwd(a["x1"], a["r2"], w["b_down"], w["ln2_g"], w["ln2_b"], dx2, t)
    relu2_back = _relu2_back_epilogue(a["hpre"], w["b_up"])
    if pending is None:
        dhpre, db_up_rows = _mm(dr2, w["w_down"], "nt", "d_h", epilogue=relu2_back)
        dx1 = _mm(dhpre, w["w_up"], "nt", "d_x1", addends=[(1.0, dx1_a)])
        landed = None
    else:
        (dhpre, db_up_rows), landed_w_in = _mm(
            dr2, w["w_down"], "nt", "d_h_scatter", epilogue=relu2_back,
            exchange=(_scatter_plan, pending[:1], _landed_shapes(pending[:1])))
        dx1, landed_rest = _mm(dhpre, w["w_up"], "nt", "d_x1_scatter", addends=[(1.0, dx1_a)],
                               exchange=(_scatter_plan, pending[1:], _landed_shapes(pending[1:])))
        landed = [landed_w_in[0], landed_rest[0]]
    db_up = jnp.sum(db_up_rows, axis=0, keepdims=True)
    dw_down = _mm(a["h"], dr2, "tn", "dw_down")
    dw_up = _mm(a["x1_16"], dhpre, "tn", "dw_up")
    (dx_a, dr1), (dg1, db1) = _ln1_bwd(a["x"], a["r1"], w["ln1_g"], w["ln1_b"], dx1, t)
    dmixed = _mm(dr1, w["w_out"], "nt", "d_mixed", out_dtype=BF16)
    dw_out = _mm(a["mixed"], dr1, "tn", "dw_out")
    dgates, dy_a, dy_b = _mix_bwd(a["proj"], a["y_a"], a["y_b"], dmixed, t)
    dog = _mm(dy_a, w["w_o_gdn"], "nt", "d_og", out_dtype=BF16)
    dw_o_gdn = _mm(a["og"], dy_a, "tn", "dw_o_gdn")
    ds = _mm(dy_b, w["w_o_sc"], "nt", "d_s")
    dw_o_sc = _mm(a["s"], dy_b, "tn", "dw_o_sc")
    dsc, dconv_sc = _sc_bwd(a["proj"], w["conv_sc"], ds, t)
    do, dz, dnorm_w = _gate_norm_bwd(a["o_f"], a["o_b"], a["proj"], w["gdn_norm_w"], dog, t)
    dq_f, dq_b, dgb_f, dgb_b = _gdn_bwd(a["qkvn"], a["gb"], a["s_f"], a["s_b"], a["t_f"], a["t_b"], do, t)
    dqkv, dconv_qkv = _qkv_conv_bwd(a["proj"], w["conv_qkv"], a["conv_out"], dq_f, dq_b, t)
    dab, da_log, ddt_bias = _gating_bwd(a["proj_ab"], w["a_log"], w["dt_bias"], dgb_f, dgb_b, t)

    pieces = [(dqkv, 0), (dz, 3 * D_MODEL), (dsc, 4 * D_MODEL), (dgates, 7 * D_MODEL)]
    dx = _mm(dab, w["w_ab"], "nt", "dx_ab", addends=[(1.0, dx_a)])
    dw_main = []
    for j, (piece, off) in enumerate(pieces):
        dx = _mm(piece, w["w_main"], "nt", f"dx_{j}", b_off=off, kdim=piece.shape[1], addends=[(1.0, dx)])
        dw_main.append(_mm(a["x16"], piece, "tn", f"dw_in_{j}"))
    dw_ab = _mm(a["x16"], dab, "tn", "dw_ab")
    dw_main = jnp.concatenate(dw_main, axis=1)
    dw_in = jnp.concatenate([dw_main[:, :QKVZ_COLS], dw_ab[:, :AB_COLS], dw_main[:, QKVZ_COLS:]], axis=1)
    grads = dict(w_in=dw_in, conv_qkv=dconv_qkv[:3], a_log=da_log[0, :2 * N_HEADS].reshape(2, N_HEADS),
                 dt_bias=ddt_bias[0, :2 * N_HEADS].reshape(2, N_HEADS), gdn_norm_w=dnorm_w[0], w_o_gdn=dw_o_gdn,
                 conv_sc=dconv_sc[:3], w_o_sc=dw_o_sc, w_out=dw_out, ln1_g=dg1[0], ln1_b=db1[0], w_up=dw_up,
                 b_up=db_up[0], w_down=dw_down, b_down=db_down[0], ln2_g=dg2[0], ln2_b=db2[0])
    return dx, grads, landed


def kernel(x, w_in, conv_qkv, a_log, dt_bias, gdn_norm_w, w_o_gdn, conv_sc, w_o_sc, w_out, ln1_g, ln1_b, w_up, b_up, w_down, b_down, ln2_g, ln2_b, loss_target, m_w_in, m_conv_qkv, m_a_log, m_dt_bias, m_gdn_norm_w, m_w_o_gdn, m_conv_sc, m_w_o_sc, m_w_out, m_ln1_g, m_ln1_b, m_w_up, m_b_up, m_w_down, m_b_down, m_ln2_g, m_ln2_b, v_w_in, v_conv_qkv, v_a_log, v_dt_bias, v_gdn_norm_w, v_w_o_gdn, v_conv_sc, v_w_o_sc, v_w_out, v_ln1_g, v_ln1_b, v_w_up, v_b_up, v_w_down, v_b_down, v_ln2_g, v_ln2_b):
    weights = dict(w_in=w_in, conv_qkv=conv_qkv, a_log=a_log, dt_bias=dt_bias, gdn_norm_w=gdn_norm_w,
                   w_o_gdn=w_o_gdn, conv_sc=conv_sc, w_o_sc=w_o_sc, w_out=w_out, ln1_g=ln1_g, ln1_b=ln1_b,
                   w_up=w_up, b_up=b_up, w_down=w_down, b_down=b_down, ln2_g=ln2_g, ln2_b=ln2_b)
    m_in = dict(w_in=m_w_in, conv_qkv=m_conv_qkv, a_log=m_a_log, dt_bias=m_dt_bias, gdn_norm_w=m_gdn_norm_w,
                w_o_gdn=m_w_o_gdn, conv_sc=m_conv_sc, w_o_sc=m_w_o_sc, w_out=m_w_out, ln1_g=m_ln1_g, ln1_b=m_ln1_b,
                w_up=m_w_up, b_up=m_b_up, w_down=m_w_down, b_down=m_b_down, ln2_g=m_ln2_g, ln2_b=m_ln2_b)
    v_in = dict(w_in=v_w_in, conv_qkv=v_conv_qkv, a_log=v_a_log, dt_bias=v_dt_bias, gdn_norm_w=v_gdn_norm_w,
                w_o_gdn=v_w_o_gdn, conv_sc=v_conv_sc, w_o_sc=v_w_o_sc, w_out=v_w_out, ln1_g=v_ln1_g, ln1_b=v_ln1_b,
                w_up=v_w_up, b_up=v_b_up, w_down=v_w_down, b_down=v_b_down, ln2_g=v_ln2_g, ln2_b=v_ln2_b)
    t = x.shape[1]
    depth = w_in.shape[0]

    def layer_weights(full, l):
        w_in_l = full["w_in"][0]
        return dict(
            w_main=jnp.concatenate([w_in_l[:, :QKVZ_COLS], w_in_l[:, QKVZ_COLS + AB_COLS:]], axis=1),
            w_ab=jnp.pad(w_in_l[:, QKVZ_COLS:QKVZ_COLS + AB_COLS], ((0, 0), (0, LANES - AB_COLS))),
            conv_qkv=jnp.pad(full["conv_qkv"][0], ((0, SUBLANES - 3), (0, 0))),
            conv_sc=jnp.pad(full["conv_sc"][0], ((0, SUBLANES - 3), (0, 0))),
            a_log=_lane_row(a_log[l]), dt_bias=_lane_row(dt_bias[l]), gdn_norm_w=gdn_norm_w[l][None, :],
            w_o_gdn=full["w_o_gdn"][0], w_o_sc=full["w_o_sc"][0], w_out=full["w_out"][0],
            ln1_g=ln1_g[l][None, :], ln1_b=ln1_b[l][None, :], w_up=full["w_up"][0], b_up=b_up[l][None, :],
            w_down=full["w_down"][0], b_down=b_down[l][None, :], ln2_g=ln2_g[l][None, :], ln2_b=ln2_b[l][None, :])

    packed = [_pack_weights({name: weights[name][l:l + 1] for name in MATRICES + CONVS}) for l in range(depth)]
    gathered = _exchange_call("weights_all_gather", _gather_plan, packed[0][0], _gathered_shapes(packed[0][0]))
    h = x.reshape(t, D_MODEL)
    h16 = h.astype(BF16)
    layers, saved = [], []
    for l in range(depth):
        layers.append(layer_weights(_unpack_weights(gathered, packed[l][1]), l))
        next_arrays = packed[l + 1][0] if l + 1 < depth else None
        h, h16, acts, gathered = _forward_layer(h, h16, layers[l], t, next_arrays)
        saved.append(acts)
    dh, loss_local = _loss_stage(h, loss_target.reshape(t, D_MODEL), t)
    loss = lax.psum(loss_local, MESH_AXES)

    layer_shapes = {name: weights[name].shape[1:] for name in SHARDED}
    layer_grads, reduced, pending = [None] * depth, [None] * depth, None
    for l in reversed(range(depth)):
        dh, layer_grads[l], landed = _backward_layer(dh, layers[l], saved[l], t, pending)
        if landed is not None:
            reduced[l + 1] = _sum_layer_grads(landed, layer_shapes)
        pending = _pack_layer_grads(layer_grads[l])
    small = _pack_small_grads({name: jnp.stack([g[name] for g in layer_grads]) for name in SMALL})
    last = pending + [small]
    landed = _exchange_call("grads_scatter", _scatter_plan, last, _landed_shapes(last))
    reduced[0] = _sum_layer_grads(landed[:2], layer_shapes)
    grads = {name: jnp.stack([r[name] for r in reduced]) for name in SHARDED}
    small_shapes = {name: weights[name].shape for name in SMALL}
    grads.update(_unpack(_sum_slots(landed[2], "grads_sum_small").reshape(-1), SMALL, small_shapes))

    names = list(weights)
    updates = {n: _adamw(weights[n], grads[n], m_in[n], v_in[n], f"adamw_{n}") for n in names}
    return (loss, dh.reshape(x.shape), *[grads[n] for n in names], *[updates[n][0] for n in names],
            *[updates[n][1] for n in names], *[updates[n][2] for n in names])
```

```python
import functools

import jax
import jax.numpy as jnp
from jax import lax
from jax.experimental import pallas as pl
from jax.experimental.pallas import tpu as pltpu

F32 = jnp.float32
BF16 = jnp.bfloat16

D_MODEL = 1024
N_HEADS = 8
HEAD_DIM = 128
CHUNK = 64
D_FF = 4 * D_MODEL
DEPTH = 4
N_DEV = 8
LN_EPS = 1e-5
RMS_EPS = 1e-6
L2_EPS = 1e-6
ALPHA = (2 * DEPTH) ** 0.25
MAIN_COLS = 9 * D_MODEL
QKVZ_COLS = 4 * D_MODEL
AB_COLS = 4 * N_HEADS
W_IN_COLS = MAIN_COLS + AB_COLS
LANES = 128
SUBLANES = 8
VMEM_LIMIT = 48 * 1024 * 1024
MM_TILE = 1024
LN_MM_ROWS = 512
ROW_TILE = 256
WIDE_ROW_TILE = 128
ADAM = dict(lr=0.001, b1=0.9, b2=0.999, eps=1e-08, wd=0.01, step=10)
MESH_AXES = ("x", "y", "c")


def _params(sem=None):
    return pltpu.CompilerParams(dimension_semantics=sem, vmem_limit_bytes=VMEM_LIMIT)


_DIMS = {"nn": (1, 0), "nt": (1, 1), "tn": (0, 0)}


def _mm(a, b, mode, name, *, out_dtype=F32, b_off=0, kdim=None, addends=(), epilogue=None, exchange=None, tm=None):
    if mode == "nn":
        (m, k), n = a.shape, b.shape[1]
    elif mode == "nt":
        (m, k), n = a.shape, b.shape[0]
        k = kdim or k
    else:
        (k, m), n = a.shape, b.shape[1]
    tm, tn, tk = min(m, tm or MM_TILE), min(n, MM_TILE), min(k, MM_TILE)
    assert m % tm == 0 and n % tn == 0 and k % tk == 0 and b_off % tk == 0
    nk = k // tk
    koff = b_off // tk
    ca, cb = _DIMS[mode]
    scales = tuple(s for s, _ in addends)
    na = len(addends)
    epi_fn, epi_in, epi_out = epilogue or (None, (), (("tile", out_dtype),))
    ne, no = len(epi_in), len(epi_out)
    plan, x_arrays, x_shapes = exchange or (None, (), ())
    nx = len(x_arrays)
    gi, gj = m // tm, n // tn

    def body(a_ref, b_ref, *rest):
        add_refs, epi_refs = rest[:na], rest[na:na + ne]
        xin_refs, rest = rest[na + ne:na + ne + nx], rest[na + ne + nx:]
        o_refs, xout_refs, rest = rest[:no], rest[no:no + nx], rest[no + nx:]
        kk = pl.program_id(2)
        if plan:
            start, forward, finish_exchange = plan(xin_refs, xout_refs, *rest[-3:])
            i, j = pl.program_id(0), pl.program_id(1)

            @pl.when((i == 0) & (j == 0) & (kk == 0))
            def _():
                start()

            @pl.when((i == gi // 2) & (j == gj // 2) & (kk == 0))
            def _():
                forward()

        p = lax.dot_general(a_ref[...].astype(BF16), b_ref[...].astype(BF16), (((ca,), (cb,)), ((), ())),
                            preferred_element_type=F32)

        def finish(r):
            for s, ref in zip(scales, add_refs):
                r = r + s * ref[...].astype(F32)
            vals = epi_fn(r, *[ref[...] for ref in epi_refs]) if epi_fn else (r,)
            for (kind, _), ref, val in zip(epi_out, o_refs, vals):
                if kind == "colsum":
                    val = jnp.where(lax.broadcasted_iota(jnp.int32, ref.shape, 0) == 0, val, 0.0)
                ref[...] = val.astype(ref.dtype)

        if nk == 1:
            finish(p)
        else:
            acc = rest[0]

            @pl.when(kk == 0)
            def _():
                acc[...] = p

            @pl.when(kk > 0)
            def _():
                acc[...] += p

            @pl.when(kk == nk - 1)
            def _():
                finish(acc[...])

        if plan:
            @pl.when((i == gi - 1) & (j == gj - 1) & (kk == nk - 1))
            def _():
                finish_exchange()

    if mode == "nn":
        a_spec = pl.BlockSpec((tm, tk), lambda i, j, kk: (i, kk))
        b_spec = pl.BlockSpec((tk, tn), lambda i, j, kk: (kk, j))
    elif mode == "nt":
        a_spec = pl.BlockSpec((tm, tk), lambda i, j, kk: (i, kk))
        b_spec = pl.BlockSpec((tn, tk), lambda i, j, kk: (j, kk + koff))
    else:
        a_spec = pl.BlockSpec((tk, tm), lambda i, j, kk: (kk, i))
        b_spec = pl.BlockSpec((tk, tn), lambda i, j, kk: (kk, j))
    kinds = {"tile": (pl.BlockSpec((tm, tn), lambda i, j, kk: (i, j)), (m, n)),
             "row": (pl.BlockSpec((1, tn), lambda i, j, kk: (0, j)), (1, n)),
             "colsum": (pl.BlockSpec((SUBLANES, tn), lambda i, j, kk: (i, j)), (SUBLANES * (m // tm), n))}
    o_spec = kinds["tile"][0]
    any_spec = pl.BlockSpec(memory_space=pl.ANY)
    res = pl.pallas_call(
        body, name=name, grid=(gi, gj, nk),
        in_specs=[a_spec, b_spec] + [o_spec] * na + [kinds[kind][0] for _, kind in epi_in] + [any_spec] * nx,
        out_specs=[kinds[kind][0] for kind, _ in epi_out] + [any_spec] * nx,
        out_shape=[jax.ShapeDtypeStruct(kinds[kind][1], dt) for kind, dt in epi_out] + list(x_shapes),
        scratch_shapes=([pltpu.VMEM((tm, tn), F32)] if nk > 1 else []) + (_exchange_sems(nx) if plan else []),
        compiler_params=_params(("arbitrary",) * 3 if plan else ("parallel", "parallel", "arbitrary")),
    )(a, b, *[arr for _, arr in addends], *[arr for arr, _ in epi_in], *x_arrays)
    if plan:
        return (tuple(res[:no]) if epilogue else res[0]), tuple(res[no:])
    return res if epilogue else res[0]


def _tile_call(name, body, t, tm, tiled, halo, params, outs, accs):
    tm = min(tm, t)
    assert t % tm == 0 and tm % SUBLANES == 0
    steps = t // tm
    hb = tm // SUBLANES
    nt, nh, npar, no = len(tiled), len(halo), len(params), len(outs)

    def kern(*refs):
        i = pl.program_id(0)
        t_refs = refs[:nt + nh]
        h_refs = refs[nt + nh:nt + 3 * nh]
        p_refs = refs[nt + 3 * nh:nt + 3 * nh + npar]
        o_refs = refs[nt + 3 * nh + npar:nt + 3 * nh + npar + no]
        a_refs = refs[nt + 3 * nh + npar + no:]
        tiles = [r[...].astype(F32) for r in t_refs]
        halos = []
        for j in range(nh):
            prev = h_refs[2 * j][SUBLANES - 1:SUBLANES, :].astype(F32)
            nxt = h_refs[2 * j + 1][0:1, :].astype(F32)
            halos.append((jnp.where(i > 0, prev, 0.0), jnp.where(i < steps - 1, nxt, 0.0)))
        o_vals, a_vals = body(tiles, halos, [r[...] for r in p_refs])
        for ref, val in zip(o_refs, o_vals):
            ref[...] = val.astype(ref.dtype)
        for ref, val in zip(a_refs, a_vals):
            @pl.when(i == 0)
            def _(ref=ref, val=val):
                ref[...] = val

            @pl.when(i > 0)
            def _(ref=ref, val=val):
                ref[...] += val

    in_specs, args = [], []
    for arr, nc, cb in list(tiled) + list(halo):
        in_specs.append(pl.BlockSpec((tm, nc), lambda i, cb=cb: (i, cb)))
        args.append(arr)
    last = t // SUBLANES - 1
    for arr, nc, cb in halo:
        in_specs.append(pl.BlockSpec((SUBLANES, nc), lambda i, cb=cb: (jnp.maximum(i * hb - 1, 0), cb)))
        in_specs.append(pl.BlockSpec((SUBLANES, nc), lambda i, cb=cb: (jnp.minimum((i + 1) * hb, last), cb)))
        args += [arr, arr]
    for arr in params:
        in_specs.append(pl.BlockSpec(arr.shape, lambda i: (0, 0)))
        args.append(arr)
    out_specs = [pl.BlockSpec((tm, nc), lambda i: (i, 0)) for nc, _ in outs]
    out_specs += [pl.BlockSpec(shape, lambda i: (0, 0)) for shape in accs]
    out_shape = [jax.ShapeDtypeStruct((t, nc), dt) for nc, dt in outs]
    out_shape += [jax.ShapeDtypeStruct(shape, F32) for shape in accs]
    res = pl.pallas_call(kern, name=name, grid=(steps,), in_specs=in_specs, out_specs=out_specs,
                         out_shape=out_shape, compiler_params=_params(("arbitrary",)))(*args)
    return res[:no], res[no:]


def _row_iota(x):
    return lax.broadcasted_iota(jnp.int32, x.shape, 0)


def _lane_iota(x):
    return lax.broadcasted_iota(jnp.int32, x.shape, 1)


def _shift_down(x, first_row):
    return jnp.where(_row_iota(x) == 0, first_row, pltpu.roll(x, 1, 0))


def _shift_up(x, last_row):
    n = x.shape[0]
    return jnp.where(_row_iota(x) == n - 1, last_row, pltpu.roll(x, n - 1, 0))


def _taps(w):
    return w[0:1, :], w[1:2, :], w[2:3, :]


def _tap_rows(d0, d1, d2, rows=SUBLANES):
    r = lax.broadcasted_iota(jnp.int32, (rows, d0.shape[1]), 0)
    return jnp.where(r == 0, d0, jnp.where(r == 1, d1, jnp.where(r == 2, d2, 0.0)))


def _colsum(x):
    return jnp.sum(x, axis=0, keepdims=True)


def _silu(x):
    return x * jax.nn.sigmoid(x)


def _softplus(x):
    return jnp.maximum(x, 0.0) + jnp.log(1.0 + jnp.exp(-jnp.abs(x)))


def _heads(x):
    return [x[:, h * HEAD_DIM:(h + 1) * HEAD_DIM] for h in range(x.shape[1] // HEAD_DIM)]


def _post_conv(c):
    blocks = _heads(_silu(c))
    out = []
    for j, blk in enumerate(blocks):
        if j < 2 * N_HEADS:
            blk = blk * lax.rsqrt(jnp.sum(blk * blk, axis=-1, keepdims=True) + L2_EPS)
        if j < N_HEADS:
            blk = blk * (HEAD_DIM ** -0.5)
        out.append(blk)
    return jnp.concatenate(out, axis=1)


def _gating(ab, a_log, dt_bias):
    lane = _lane_iota(ab)
    g = -jnp.exp(a_log) * _softplus(ab + dt_bias)
    return jnp.where(lane < 2 * N_HEADS, g, jnp.where(lane < AB_COLS, jax.nn.sigmoid(ab), 0.0))


def _gate_norm(o_f, o_b, z, norm_w):
    out = []
    for oh, zh in zip(_heads(o_f + o_b), _heads(z)):
        out.append(oh * lax.rsqrt(jnp.mean(oh * oh, axis=-1, keepdims=True) + RMS_EPS) * norm_w * _silu(zh))
    return jnp.concatenate(out, axis=1)


def _mix(gate_a, gate_b, y_a, y_b):
    return jax.nn.sigmoid(gate_a) * y_a + jax.nn.sigmoid(gate_b) * y_b


def _layer_norm(u, g, b):
    mu = jnp.mean(u, axis=-1, keepdims=True)
    var = jnp.mean(jnp.square(u - mu), axis=-1, keepdims=True)
    return (u - mu) * lax.rsqrt(var + LN_EPS) * g + b


def _ln1(x, r, g, b):
    return _layer_norm(ALPHA * x + r, g, b)


def _ln2(x, r, bias, g, b):
    return _layer_norm(ALPHA * x + r + bias, g, b)


def _relu2(hpre, bias):
    return jnp.square(jnp.maximum(hpre + bias, 0.0))


def _qkv_conv_fwd(proj, conv_w, t):
    def body(tiles, halos, params):
        (x,), ((xp, xn),), (w,) = tiles, halos, params
        w0, w1, w2 = _taps(w)
        c = w0 * _shift_down(x, xp) + w1 * x + w2 * _shift_up(x, xn)
        return [c, _post_conv(c)], []

    (c, qkvn), _ = _tile_call("qkv_conv_fwd", body, t, WIDE_ROW_TILE, [], [(proj, 3 * D_MODEL, 0)], [conv_w],
                              [(3 * D_MODEL, F32), (3 * D_MODEL, F32)], [])
    return c, qkvn


def _gating_fwd(proj_ab, a_log, dt_bias, t):
    def body(tiles, halos, params):
        return [_gating(tiles[0], params[0], params[1])], []

    (gb,), _ = _tile_call("gating_fwd", body, t, ROW_TILE, [(proj_ab, LANES, 0)], [], [a_log, dt_bias],
                          [(LANES, F32)], [])
    return gb


def _gate_norm_fwd(o_f, o_b, proj, norm_w, t):
    def body(tiles, halos, params):
        return [_gate_norm(tiles[0], tiles[1], tiles[2], params[0])], []

    (og,), _ = _tile_call("gate_norm_fwd", body, t, ROW_TILE,
                          [(o_f, D_MODEL, 0), (o_b, D_MODEL, 0), (proj, D_MODEL, 3)], [], [norm_w],
                          [(D_MODEL, BF16)], [])
    return og


def _sc_fwd(proj, conv_w, t):
    def body(tiles, halos, params):
        (sb,), ((cp, cn), (xp, xn)), (w,) = tiles[:1], halos, params
        sc, sx = tiles[1], tiles[2]
        w0, w1, w2 = _taps(w)
        u = sc * sx
        return [sb * (w0 * _shift_down(u, cp * xp) + w1 * u + w2 * _shift_up(u, cn * xn))], []

    (s,), _ = _tile_call("sc_fwd", body, t, ROW_TILE, [(proj, D_MODEL, 4)],
                         [(proj, D_MODEL, 5), (proj, D_MODEL, 6)], [conv_w], [(D_MODEL, BF16)], [])
    return s


def _mix_fwd(proj, y_a, y_b, t):
    def body(tiles, halos, params):
        return [_mix(*tiles)], []

    (mixed,), _ = _tile_call("mix_fwd", body, t, ROW_TILE,
                             [(proj, D_MODEL, 7), (proj, D_MODEL, 8), (y_a, D_MODEL, 0), (y_b, D_MODEL, 0)], [], [],
                             [(D_MODEL, BF16)], [])
    return mixed


LN_OUTS = [("tile", F32), ("tile", F32), ("tile", BF16)]


def _ln1_epilogue(x, g, b):
    def fn(r, xv, gv, bv):
        y = _ln1(xv, r, gv, bv)
        return r, y, y

    return fn, [(x, "tile"), (g, "row"), (b, "row")], LN_OUTS


def _ln2_epilogue(x, bias, g, b):
    def fn(r, xv, biasv, gv, bv):
        y = _ln2(xv, r, biasv, gv, bv)
        return r, y, y

    return fn, [(x, "tile"), (bias, "row"), (g, "row"), (b, "row")], LN_OUTS


def _relu2_epilogue(bias):
    return (lambda r, b: (r, _relu2(r, b))), [(bias, "row")], [("tile", F32), ("tile", BF16)]


def _relu2_back_epilogue(hpre, bias):
    def fn(r, hp, b):
        _, vjp = jax.vjp(_relu2, hp, b)
        return vjp(r)

    return fn, [(hpre, "tile"), (bias, "row")], [("tile", BF16), ("colsum", F32)]


def _loss_stage(y, target, t):
    def body(tiles, halos, params):
        d = tiles[0] - tiles[1]
        part = 0.5 * jnp.sum(jnp.mean(d * d, axis=-1, keepdims=True), axis=0, keepdims=True)
        return [d * (1.0 / D_MODEL)], [jnp.broadcast_to(part, (1, LANES))]

    (dy,), (loss,) = _tile_call("loss", body, t, ROW_TILE, [(y, D_MODEL, 0), (target, D_MODEL, 0)], [], [],
                                [(D_MODEL, F32)], [(1, LANES)])
    return dy, loss[0, 0]


def _ln2_bwd(x, r, bias, g, b, dy, t):
    def body(tiles, halos, params):
        _, vjp = jax.vjp(_ln2, tiles[0], tiles[1], params[0], params[1], params[2])
        dx, dr, dbias, dg, db = vjp(tiles[2])
        return [dx, dr], [dbias, dg, db]

    return _tile_call("ln2_bwd", body, t, ROW_TILE, [(x, D_MODEL, 0), (r, D_MODEL, 0), (dy, D_MODEL, 0)], [],
                      [bias, g, b], [(D_MODEL, F32), (D_MODEL, BF16)], [(1, D_MODEL)] * 3)


def _ln1_bwd(x, r, g, b, dy, t):
    def body(tiles, halos, params):
        _, vjp = jax.vjp(_ln1, tiles[0], tiles[1], params[0], params[1])
        dx, dr, dg, db = vjp(tiles[2])
        return [dx, dr], [dg, db]

    return _tile_call("ln1_bwd", body, t, ROW_TILE, [(x, D_MODEL, 0), (r, D_MODEL, 0), (dy, D_MODEL, 0)], [],
                      [g, b], [(D_MODEL, F32), (D_MODEL, BF16)], [(1, D_MODEL)] * 2)


def _mix_bwd(proj, y_a, y_b, dmixed, t):
    def body(tiles, halos, params):
        _, vjp = jax.vjp(_mix, *tiles[:4])
        dga, dgb, dya, dyb = vjp(tiles[4])
        return [jnp.concatenate([dga, dgb], axis=1), dya, dyb], []

    (dgates, dya, dyb), _ = _tile_call(
        "mix_bwd", body, t, ROW_TILE,
        [(proj, D_MODEL, 7), (proj, D_MODEL, 8), (y_a, D_MODEL, 0), (y_b, D_MODEL, 0), (dmixed, D_MODEL, 0)], [], [],
        [(2 * D_MODEL, BF16), (D_MODEL, BF16), (D_MODEL, BF16)], [])
    return dgates, dya, dyb


def _sc_bwd(proj, conv_w, ds, t):
    def body(tiles, halos, params):
        ds_, sb, sc, sx = tiles
        (dsp, dsn), (sbp, sbn), (scp, scn), (sxp, sxn) = halos
        w0, w1, w2 = _taps(params[0])
        u = sc * sx
        u_prev, u_next = _shift_down(u, scp * sxp), _shift_up(u, scn * sxn)
        dconv = ds_ * sb
        du = w0 * _shift_up(dconv, dsn * sbn) + w1 * dconv + w2 * _shift_down(dconv, dsp * sbp)
        dsb = ds_ * (w0 * u_prev + w1 * u + w2 * u_next)
        dw = _tap_rows(_colsum(dconv * u_prev), _colsum(dconv * u), _colsum(dconv * u_next))
        return [jnp.concatenate([dsb, du * sx, du * sc], axis=1)], [dw]

    (dsc,), (dw,) = _tile_call("sc_bwd", body, t, ROW_TILE, [],
                               [(ds, D_MODEL, 0), (proj, D_MODEL, 4), (proj, D_MODEL, 5), (proj, D_MODEL, 6)],
                               [conv_w], [(3 * D_MODEL, BF16)], [(SUBLANES, D_MODEL)])
    return dsc, dw


def _gate_norm_bwd(o_f, o_b, proj, norm_w, dog, t):
    def body(tiles, halos, params):
        _, vjp = jax.vjp(_gate_norm, tiles[0], tiles[1], tiles[2], params[0])
        do, _, dz, dnw = vjp(tiles[3])
        return [do, dz], [dnw]

    (do, dz), (dnw,) = _tile_call(
        "gate_norm_bwd", body, t, ROW_TILE,
        [(o_f, D_MODEL, 0), (o_b, D_MODEL, 0), (proj, D_MODEL, 3), (dog, D_MODEL, 0)], [], [norm_w],
        [(D_MODEL, F32), (D_MODEL, BF16)], [(1, HEAD_DIM)])
    return do, dz, dnw


def _qkv_conv_bwd(proj, conv_w, c, dq_f, dq_b, t):
    def post_conv_back(cv, ct):
        _, vjp = jax.vjp(_post_conv, cv)
        return vjp(ct)[0]

    def body(tiles, halos, params):
        cv, df, db, x = tiles
        (cp, cn), (dfp, dfn), (dbp, dbn), (xp, xn) = halos
        w0, w1, w2 = _taps(params[0])
        d = post_conv_back(cv, df + db)
        d_prev, d_next = post_conv_back(cp, dfp + dbp), post_conv_back(cn, dfn + dbn)
        dx = w0 * _shift_up(d, d_next) + w1 * d + w2 * _shift_down(d, d_prev)
        dw = _tap_rows(_colsum(d * _shift_down(x, xp)), _colsum(d * x), _colsum(d * _shift_up(x, xn)))
        return [dx], [dw]

    wide = 3 * D_MODEL
    (dqkv,), (dw,) = _tile_call("qkv_conv_bwd", body, t, WIDE_ROW_TILE, [],
                                [(c, wide, 0), (dq_f, wide, 0), (dq_b, wide, 0), (proj, wide, 0)], [conv_w],
                                [(wide, BF16)], [(SUBLANES, wide)])
    return dqkv, dw


def _gating_bwd(proj_ab, a_log, dt_bias, dgb_f, dgb_b, t):
    def body(tiles, halos, params):
        _, vjp = jax.vjp(_gating, tiles[0], params[0], params[1])
        dab, dal, ddt = vjp(tiles[1] + tiles[2])
        return [dab], [dal, ddt]

    (dab,), (dal, ddt) = _tile_call("gating_bwd", body, t, ROW_TILE,
                                    [(proj_ab, LANES, 0), (dgb_f, LANES, 0), (dgb_b, LANES, 0)], [],
                                    [a_log, dt_bias], [(LANES, BF16)], [(1, LANES)] * 2)
    return dab, dal, ddt


@functools.partial(jax.custom_vjp, nondiff_argnums=(2, 3))
def _dot(a, b, ca, cb):
    return lax.dot_general(a.astype(BF16), b.astype(BF16), (((ca,), (cb,)), ((), ())), preferred_element_type=F32)


def _dot_fwd(a, b, ca, cb):
    return _dot(a, b, ca, cb), (a, b)


def _dot_bwd(ca, cb, res, ct):
    a, b = res
    fa, fb = 1 - ca, 1 - cb
    da = _dot(ct, b, 1, fb) if ca == 1 else _dot(b, ct, fb, 1)
    db = _dot(a, ct, fa, 0) if cb == 0 else _dot(ct, a, 0, fa)
    return da, db


_dot.defvjp(_dot_fwd, _dot_bwd)


def _split3(x):
    hi = x.astype(BF16)
    r1 = x - hi.astype(F32)
    mid = r1.astype(BF16)
    return hi, mid, (r1 - mid.astype(F32)).astype(BF16)


def _dot_exact(a, b, ca, cb, exact):
    dims = (((ca,), (cb,)), ((), ()))
    if exact == 0:
        return sum(lax.dot_general(a.astype(BF16), p, dims, preferred_element_type=F32) for p in _split3(b))
    return sum(lax.dot_general(p, b.astype(BF16), dims, preferred_element_type=F32) for p in _split3(a))


def _tri_masks(n, rev):
    r = lax.broadcasted_iota(jnp.int32, (n, n), 0)
    c = lax.broadcasted_iota(jnp.int32, (n, n), 1)
    return ((c >= r), (c > r)) if rev else ((c <= r), (c < r))


@functools.partial(jax.custom_vjp, nondiff_argnums=(1,))
def _cumsum_rows(g, rev):
    incl, _ = _tri_masks(g.shape[0], rev)
    return _dot_exact(incl.astype(F32), g, 1, 0, 0)


_cumsum_rows.defvjp(lambda g, rev: (_cumsum_rows(g, rev), None),
                    lambda rev, _, ct: (_cumsum_rows(ct, not rev),))


def _eye(n):
    return (lax.broadcasted_iota(jnp.int32, (n, n), 0) == lax.broadcasted_iota(jnp.int32, (n, n), 1)).astype(F32)


@jax.custom_vjp
def _to_rows(x):
    return _dot_exact(_eye(x.shape[1]), x, 1, 1, 0)


@jax.custom_vjp
def _to_cols(y):
    return _dot_exact(y, _eye(y.shape[0]), 0, 0, 1)


_to_rows.defvjp(lambda x: (_to_rows(x), None), lambda _, ct: (_to_cols(ct),))
_to_cols.defvjp(lambda y: (_to_cols(y), None), lambda _, ct: (_to_rows(ct),))


def _pick_col(arr, idx):
    return jnp.sum(jnp.where(_lane_iota(arr) == idx, arr, 0.0), axis=1, keepdims=True)


def _pick_row(arr, idx):
    return jnp.sum(jnp.where(_row_iota(arr) == idx, arr, 0.0), axis=0, keepdims=True)


def _chunk_gates(gb, direction, rev):
    n = gb.shape[0]
    incl, strict = _tri_masks(n, rev)
    gc = _cumsum_rows(gb, rev)
    gc_rows = _to_rows(gc)
    lanes = [direction * N_HEADS + h for h in range(N_HEADS)]
    cols = [_pick_col(gc, ln) for ln in lanes]
    rows = [_pick_row(gc_rows, ln) for ln in lanes]
    betas = [_pick_col(gb, 2 * N_HEADS + ln) for ln in lanes]
    decays = [jnp.where(incl, jnp.exp(jnp.where(incl, c - r, 0.0)), 0.0) for c, r in zip(cols, rows)]
    return cols, betas, decays, strict


def _chunk_gates_both(gb_f, gb_b):
    gates = dict(cols=[], betas=[], decays=[], lasts=[])
    for direction, gb in enumerate((gb_f, gb_b)):
        rev = direction == 1
        cols, betas, decays, _ = _chunk_gates(gb, direction, rev)
        last_idx = 0 if rev else gb.shape[0] - 1
        gates["cols"] += cols
        gates["betas"] += betas
        gates["decays"] += decays
        gates["lasts"] += [_pick_row(c, last_idx) for c in cols]
    return gates


def _chunk_lmat(k_f, k_b, gates):
    n = k_f.shape[0]
    stricts = [_tri_masks(n, False)[1]] * N_HEADS + [_tri_masks(n, True)[1]] * N_HEADS
    ks = _heads(k_f) + _heads(k_b)
    kk = [_dot(kh * b, kh, 1, 1) for kh, b in zip(ks, gates["betas"])]
    return tuple(jnp.where(s, x * d, 0.0) for s, x, d in zip(stricts, kk, gates["decays"]))


def _tri_inverse(lmats):
    n = lmats[0].shape[0]
    eye = _eye(n)
    powers = [-lm for lm in lmats]
    invs = [eye + p for p in powers]
    span = 2
    while span < n:
        powers = [_dot(p, p, 1, 0) for p in powers]
        steps = [_dot(p, i, 1, 0) for p, i in zip(powers, invs)]
        invs = [i + s for i, s in zip(invs, steps)]
        span *= 2
    return tuple(invs)


def _chunk_out(qkv_f, qkv_b, gates, tmats, states):
    qs, ks, vs = (_heads(qkv_f[p]) + _heads(qkv_b[p]) for p in range(3))
    cols, betas, decays, lasts = (gates[key] for key in ("cols", "betas", "decays", "lasts"))
    n = ks[0].shape[0]
    vk = [jnp.concatenate([vh * b, kh * b * jnp.exp(c)], axis=1) for vh, kh, b, c in zip(vs, ks, betas, cols)]
    uw = [_dot(tm, x, 1, 0) for tm, x in zip(tmats, vk)]
    attns = [_dot(qh, kh, 1, 1) * d for qh, kh, d in zip(qs, ks, decays)]
    wq = [jnp.concatenate([x[:, HEAD_DIM:], qh * jnp.exp(c)], axis=0) for x, qh, c in zip(uw, qs, cols)]
    wqs = [_dot(x, st, 1, 0) for x, st in zip(wq, states)]
    v_news = [x[:, :HEAD_DIM] - y[:n] for x, y in zip(uw, wqs)]
    inter = [y[n:] for y in wqs]
    intra = [_dot(a, vn, 1, 0) for a, vn in zip(attns, v_news)]
    adds = [_dot(kh * jnp.exp(l - c), vn, 0, 0) for kh, l, c, vn in zip(ks, lasts, cols, v_news)]
    new_states = tuple(st * jnp.exp(l) + a for st, l, a in zip(states, lasts, adds))
    outs = [x + y for x, y in zip(inter, intra)]
    return jnp.concatenate(outs[:N_HEADS], axis=1), jnp.concatenate(outs[N_HEADS:], axis=1), new_states


BOTH = 2 * N_HEADS


def _gdn_specs(n, first_backwards):
    idx = [(lambda i: n - 1 - i) if (d == 0) == first_backwards else (lambda i: i) for d in range(2)]

    def both(shape_of_block, index_tail):
        return [pl.BlockSpec(shape_of_block, lambda i, f=f: (f(i),) + index_tail) for f in idx]

    return idx, both


def _gdn_fwd(qkvn, gb, t):
    n = t // CHUNK
    idx, both = _gdn_specs(n, False)

    def body(qf, kf, vf, qb, kb, vb, gbf, gbb, of_ref, ob_ref, sf_ref, sb_ref, tf_ref, tb_ref, state):
        @pl.when(pl.program_id(0) == 0)
        def _():
            state[...] = jnp.zeros_like(state)

        qkv_f = (qf[...], kf[...], vf[...])
        qkv_b = (qb[...], kb[...], vb[...])
        gates = _chunk_gates_both(gbf[...], gbb[...])
        tmats = _tri_inverse(_chunk_lmat(qkv_f[1], qkv_b[1], gates))
        states = tuple(state[h] for h in range(BOTH))
        o_f, o_b, new_states = _chunk_out(qkv_f, qkv_b, gates, tmats, states)
        of_ref[...] = o_f
        ob_ref[...] = o_b
        for h in range(BOTH):
            s_ref, t_ref = (sf_ref, tf_ref) if h < N_HEADS else (sb_ref, tb_ref)
            s_ref[0, h % N_HEADS] = states[h]
            t_ref[0, h % N_HEADS] = tmats[h]
            state[h] = new_states[h]

    qkv_specs = [pl.BlockSpec((CHUNK, D_MODEL), lambda i, f=f, p=p: (f(i), p)) for f in idx for p in range(3)]
    return pl.pallas_call(
        body, name="gdn_fwd", grid=(n,),
        in_specs=qkv_specs + both((CHUNK, LANES), (0,)),
        out_specs=both((CHUNK, D_MODEL), (0,)) + both((1, N_HEADS, HEAD_DIM, HEAD_DIM), (0, 0, 0))
        + both((1, N_HEADS, CHUNK, CHUNK), (0, 0, 0)),
        out_shape=[jax.ShapeDtypeStruct((t, D_MODEL), F32)] * 2
        + [jax.ShapeDtypeStruct((n, N_HEADS, HEAD_DIM, HEAD_DIM), F32)] * 2
        + [jax.ShapeDtypeStruct((n, N_HEADS, CHUNK, CHUNK), F32)] * 2,
        scratch_shapes=[pltpu.VMEM((BOTH, HEAD_DIM, HEAD_DIM), F32)],
        compiler_params=_params(("arbitrary",)),
    )(*([qkvn] * 6), gb, gb)


def _gdn_bwd(qkvn, gb, s_f, s_b, t_f, t_b, do, t):
    n = t // CHUNK
    idx, both = _gdn_specs(n, True)

    def body(qf, kf, vf, qb, kb, vb, gbf, gbb, sf_ref, sb_ref, tf_ref, tb_ref, dof, dob,
             dqf_ref, dqb_ref, dgf_ref, dgb_ref, dstate):
        @pl.when(pl.program_id(0) == 0)
        def _():
            dstate[...] = jnp.zeros_like(dstate)

        qkv_f = (qf[...], kf[...], vf[...])
        qkv_b = (qb[...], kb[...], vb[...])
        tmats = tuple((tf_ref if h < N_HEADS else tb_ref)[0, h % N_HEADS] for h in range(BOTH))
        states = tuple((sf_ref if h < N_HEADS else sb_ref)[0, h % N_HEADS] for h in range(BOTH))
        gates, gates_vjp = jax.vjp(_chunk_gates_both, gbf[...], gbb[...])
        _, out_vjp = jax.vjp(_chunk_out, qkv_f, qkv_b, gates, tmats, states)
        d_f, d_b, dgates, dtm, dst = out_vjp((dof[...], dob[...], tuple(dstate[h] for h in range(BOTH))))
        firsts = [_dot(tm, d, 0, 0) for tm, d in zip(tmats, dtm)]
        dlm = tuple(-_dot(x, tm, 1, 1) for x, tm in zip(firsts, tmats))
        _, lmat_vjp = jax.vjp(_chunk_lmat, qkv_f[1], qkv_b[1], gates)
        dk_f, dk_b, dgates_lmat = lmat_vjp(dlm)
        dg_f, dg_b = gates_vjp(jax.tree.map(jnp.add, dgates, dgates_lmat))
        dqf_ref[...] = jnp.concatenate([d_f[0], d_f[1] + dk_f, d_f[2]], axis=1)
        dqb_ref[...] = jnp.concatenate([d_b[0], d_b[1] + dk_b, d_b[2]], axis=1)
        dgf_ref[...] = dg_f
        dgb_ref[...] = dg_b
        for h in range(BOTH):
            dstate[h] = dst[h]

    qkv_specs = [pl.BlockSpec((CHUNK, D_MODEL), lambda i, f=f, p=p: (f(i), p)) for f in idx for p in range(3)]
    return pl.pallas_call(
        body, name="gdn_bwd", grid=(n,),
        in_specs=qkv_specs + both((CHUNK, LANES), (0,)) + both((1, N_HEADS, HEAD_DIM, HEAD_DIM), (0, 0, 0))
        + both((1, N_HEADS, CHUNK, CHUNK), (0, 0, 0)) + both((CHUNK, D_MODEL), (0,)),
        out_specs=both((CHUNK, 3 * D_MODEL), (0,)) + both((CHUNK, LANES), (0,)),
        out_shape=[jax.ShapeDtypeStruct((t, 3 * D_MODEL), F32)] * 2 + [jax.ShapeDtypeStruct((t, LANES), F32)] * 2,
        scratch_shapes=[pltpu.VMEM((BOTH, HEAD_DIM, HEAD_DIM), F32)],
        compiler_params=_params(("arbitrary",)),
    )(*([qkvn] * 6), gb, gb, s_f, s_b, t_f, t_b, do, do)


def _mesh_pos():
    return lax.axis_index("x"), lax.axis_index("y"), lax.axis_index("c")


def _exchange_sems(na):
    return [pltpu.SemaphoreType.DMA((na, N_DEV - 1)), pltpu.SemaphoreType.DMA((na, N_DEV - 1)),
            pltpu.SemaphoreType.DMA((na,))]


def _gather_plan(x_refs, out_refs, send_sems, recv_sems, local_sems):
    na = len(x_refs)
    x, y, c = _mesh_pos()
    me, sibling = (x, y, c), (x, y, 1 - c)
    chips = [(1 - x, y), (x, 1 - y), (1 - x, 1 - y)]

    def block(a, px, py, pc):
        return out_refs[a].at[4 * px + 2 * py + pc]

    def copy(a, k, blk, to, src=None):
        return pltpu.make_async_remote_copy(
            src_ref=block(a, *blk) if src is None else src, dst_ref=block(a, *blk),
            send_sem=send_sems.at[a, k], recv_sem=recv_sems.at[a, k],
            device_id=to, device_id_type=pl.DeviceIdType.MESH)

    def mine(a):
        return pltpu.make_async_copy(x_refs[a], block(a, *me), local_sems.at[a])

    def first(a):
        return [copy(a, 0, me, sibling, src=x_refs[a])] + [
            copy(a, 1 + j, me, (*chip, c), src=x_refs[a]) for j, chip in enumerate(chips)]

    def passed(a, j):
        return copy(a, 4 + j, (*chips[j], c), sibling)

    def start():
        for a in range(na):
            mine(a).start()
            for cp in first(a):
                cp.start()

    def forward():
        for j, chip in enumerate(chips):
            for a in range(na):
                copy(a, 1 + j, (*chip, c), me).wait_recv()
                passed(a, j).start()

    def finish():
        for a in range(na):
            copy(a, 0, sibling, me).wait_recv()
            for j, chip in enumerate(chips):
                copy(a, 4 + j, (*chip, 1 - c), me).wait_recv()
        for a in range(na):
            for cp in first(a) + [passed(a, j) for j in range(len(chips))]:
                cp.wait_send()
            mine(a).wait()

    return start, forward, finish


def _scatter_plan(g_refs, land_refs, send_sems, recv_sems, local_sems):
    na = len(g_refs)
    x, y, c = _mesh_pos()
    mine = 4 * x + 2 * y + c

    def local(a):
        return pltpu.make_async_copy(g_refs[a].at[mine], land_refs[a].at[mine], local_sems.at[a])

    def peers():
        for k in range(1, N_DEV):
            px = 1 - x if k & 4 else x
            py = 1 - y if k & 2 else y
            pc = 1 - c if k & 1 else c
            yield k, (px, py, pc), 4 * px + 2 * py + pc

    def send(a, k, to, peer):
        return pltpu.make_async_remote_copy(
            src_ref=g_refs[a].at[peer], dst_ref=land_refs[a].at[mine],
            send_sem=send_sems.at[a, k - 1], recv_sem=recv_sems.at[a, k - 1],
            device_id=to, device_id_type=pl.DeviceIdType.MESH)

    def recv(a, k, peer):
        return pltpu.make_async_remote_copy(
            src_ref=g_refs[a].at[mine], dst_ref=land_refs[a].at[peer],
            send_sem=send_sems.at[a, k - 1], recv_sem=recv_sems.at[a, k - 1],
            device_id=(x, y, c), device_id_type=pl.DeviceIdType.MESH)

    def start():
        for a in range(na):
            local(a).start()
        for k, to, peer in peers():
            for a in range(na):
                send(a, k, to, peer).start()

    def finish():
        for k, to, peer in peers():
            for a in range(na):
                recv(a, k, peer).wait_recv()
        for k, to, peer in peers():
            for a in range(na):
                send(a, k, to, peer).wait_send()
        for a in range(na):
            local(a).wait()

    return start, (lambda: None), finish


def _exchange_call(name, plan, arrays, out_shapes):
    na = len(arrays)

    def body(*refs):
        start, forward, finish = plan(refs[:na], refs[na:2 * na], *refs[2 * na:])
        start()
        forward()
        finish()

    any_spec = pl.BlockSpec(memory_space=pl.ANY)
    return pl.pallas_call(body, name=name, out_shape=list(out_shapes), in_specs=[any_spec] * na,
                          out_specs=[any_spec] * na, scratch_shapes=_exchange_sems(na))(*arrays)


def _gathered_shapes(shards):
    return [jax.ShapeDtypeStruct((N_DEV,) + s.shape, s.dtype) for s in shards]


def _landed_shapes(blocks):
    return [jax.ShapeDtypeStruct(b.shape, b.dtype) for b in blocks]


def _sum_slots(land, name):
    _, rows, cols = land.shape
    tr = _row_tile(rows, cols * 4 * N_DEV, budget=4 << 20)

    def body(*refs):
        acc = refs[0][0].astype(F32)
        for ref in refs[1:N_DEV]:
            acc = acc + ref[0].astype(F32)
        refs[N_DEV][...] = acc

    return pl.pallas_call(
        body, name=name, grid=(rows // tr,),
        in_specs=[pl.BlockSpec((1, tr, cols), lambda i, s=s: (s, i, 0)) for s in range(N_DEV)],
        out_specs=pl.BlockSpec((tr, cols), lambda i: (i, 0)),
        out_shape=jax.ShapeDtypeStruct((rows, cols), F32),
        compiler_params=_params(("parallel",)),
    )(*([land] * N_DEV))


def _row_tile(rows, row_bytes, budget=1 << 20):
    if rows * row_bytes <= budget or rows % SUBLANES:
        return rows
    best = SUBLANES
    for tr in range(SUBLANES, rows + 1, SUBLANES):
        if rows % tr == 0 and tr * row_bytes <= budget:
            best = tr
    return best


def _adamw(w, g, m, v, name):
    shape = w.shape
    cols = shape[-1]
    rows = w.size // cols
    tr = _row_tile(rows, cols * 4)
    b1, b2 = ADAM["b1"], ADAM["b2"]

    def body(w_ref, g_ref, m_ref, v_ref, d_ref, nm_ref, nv_ref):
        gv = g_ref[...]
        nm = b1 * m_ref[...] + (1.0 - b1) * gv
        nv = b2 * v_ref[...] + (1.0 - b2) * jnp.square(gv)
        m_hat = nm / (1.0 - b1 ** ADAM["step"])
        v_hat = nv / (1.0 - b2 ** ADAM["step"])
        d_ref[...] = -ADAM["lr"] * (m_hat / (jnp.sqrt(v_hat) + ADAM["eps"]) + ADAM["wd"] * w_ref[...])
        nm_ref[...] = nm
        nv_ref[...] = nv

    spec = pl.BlockSpec((tr, cols), lambda i: (i, 0))
    outs = pl.pallas_call(
        body, name=name, grid=(rows // tr,), in_specs=[spec] * 4, out_specs=[spec] * 3,
        out_shape=[jax.ShapeDtypeStruct((rows, cols), F32)] * 3, compiler_params=_params(("parallel",)),
    )(*[a.reshape(rows, cols) for a in (w, g, m, v)])
    return [o.reshape(shape) for o in outs]


MATRICES = ("w_in", "w_o_gdn", "w_o_sc", "w_out", "w_up", "w_down")
CONVS = ("conv_qkv", "conv_sc")
SHARDED = ("w_in", "conv_qkv", "w_o_gdn", "conv_sc", "w_o_sc", "w_out", "w_up", "w_down")
SMALL = ("a_log", "dt_bias", "gdn_norm_w", "ln1_g", "ln1_b", "b_up", "b_down", "ln2_g", "ln2_b")
COLUMN_SHARDED = ("w_in", "conv_qkv", "conv_sc", "w_up")
PACK_COLS = 1024


def _pack(parts, row_multiple):
    flat = jnp.concatenate(parts, axis=-1)
    unit = PACK_COLS * row_multiple
    pad = -flat.shape[-1] % unit
    flat = jnp.pad(flat, [(0, 0)] * (flat.ndim - 1) + [(0, pad)])
    return flat.reshape(flat.shape[:-1] + (flat.shape[-1] // PACK_COLS, PACK_COLS))


def _unshard(name, blocks):
    _, l, r, c = blocks.shape
    if name in COLUMN_SHARDED:
        return blocks.transpose(1, 2, 0, 3).reshape(l, r, N_DEV * c)
    return blocks.transpose(1, 0, 2, 3).reshape(l, N_DEV * r, c)


def _to_shards(name, full):
    l, r, c = full.shape
    if name in COLUMN_SHARDED:
        return full.reshape(l, r, N_DEV, c // N_DEV).transpose(2, 0, 1, 3).reshape(N_DEV, -1)
    return full.reshape(l, N_DEV, r // N_DEV, c).transpose(1, 0, 2, 3).reshape(N_DEV, -1)


def _pack_weights(shards):
    parts, layout = [], []
    for name in MATRICES[1:]:
        parts.append(shards[name].astype(BF16).reshape(-1))
        layout.append((name, shards[name].shape, 1))
    for name in CONVS:
        parts.append(jnp.stack(_split3(shards[name])).reshape(-1))
        layout.append((name, shards[name].shape, 3))
    w_in = shards["w_in"]
    return [w_in.astype(BF16).reshape(-1, w_in.shape[-1]), _pack(parts, 16)], (layout, w_in.shape)


def _unpack_weights(gathered, meta):
    (w_in_all, rest_all), (layout, w_in_shape) = gathered, meta
    full = {"w_in": _unshard("w_in", w_in_all.reshape(N_DEV, *w_in_shape))}
    rest_all, off = rest_all.reshape(N_DEV, -1), 0
    for name, shape, pieces in layout:
        size = pieces * shape[0] * shape[1] * shape[2]
        blk = rest_all[:, off:off + size]
        off += size
        if pieces == 3:
            blk = jnp.sum(blk.reshape(N_DEV, 3, *shape).astype(F32), axis=1)
        full[name] = _unshard(name, blk.reshape(N_DEV, *shape))
    return full


def _unpack(flat, names, shapes):
    out, off = {}, 0
    for name in names:
        size = 1
        for s in shapes[name]:
            size *= s
        out[name] = flat[off:off + size].reshape(shapes[name])
        off += size
    return out


def _pack_layer_grads(grads):
    r, c = grads["w_in"].shape
    w_in = grads["w_in"].astype(BF16).reshape(r, N_DEV, c // N_DEV).transpose(1, 0, 2)
    rest = _pack([_to_shards(name, grads[name].astype(BF16)[None]) for name in SHARDED[1:]], LANES)
    return [w_in, rest]


def _pack_small_grads(small_grads):
    small = jnp.concatenate([small_grads[name].reshape(-1) for name in SMALL])
    return _pack([jnp.broadcast_to(small[None, :], (N_DEV, small.shape[0]))], SUBLANES)


def _sum_layer_grads(landed, layer_shapes):
    out = {"w_in": _sum_slots(landed[0], "grads_sum_w_in")}
    shapes = {name: (1,) + layer_shapes[name] for name in SHARDED[1:]}
    rest = _unpack(_sum_slots(landed[1], "grads_sum_rest").reshape(-1), SHARDED[1:], shapes)
    out.update({name: val[0] for name, val in rest.items()})
    return out


def _lane_row(values):
    flat = values.reshape(1, -1)
    return jnp.pad(flat, ((0, 0), (0, LANES - flat.shape[1])))


def _forward_layer(x, x16, w, t, next_arrays):
    if next_arrays is None:
        proj, gathered = _mm(x16, w["w_main"], "nn", "proj_main"), None
    else:
        proj, gathered = _mm(x16, w["w_main"], "nn", "proj_main_gather",
                             exchange=(_gather_plan, next_arrays, _gathered_shapes(next_arrays)))
    proj_ab = _mm(x16, w["w_ab"], "nn", "proj_ab")
    conv_out, qkvn = _qkv_conv_fwd(proj, w["conv_qkv"], t)
    gb = _gating_fwd(proj_ab, w["a_log"], w["dt_bias"], t)
    o_f, o_b, s_f, s_b, t_f, t_b = _gdn_fwd(qkvn, gb, t)
    og = _gate_norm_fwd(o_f, o_b, proj, w["gdn_norm_w"], t)
    s = _sc_fwd(proj, w["conv_sc"], t)
    y_a = _mm(og, w["w_o_gdn"], "nn", "y_gdn")
    y_b = _mm(s, w["w_o_sc"], "nn", "y_sc")
    mixed = _mix_fwd(proj, y_a, y_b, t)
    assert D_MODEL <= MM_TILE
    r1, x1, x1_16 = _mm(mixed, w["w_out"], "nn", "out_proj", tm=LN_MM_ROWS,
                        epilogue=_ln1_epilogue(x, w["ln1_g"], w["ln1_b"]))
    hpre, h = _mm(x1_16, w["w_up"], "nn", "mlp_up", epilogue=_relu2_epilogue(w["b_up"]))
    r2, x2, x2_16 = _mm(h, w["w_down"], "nn", "mlp_down", tm=LN_MM_ROWS,
                        epilogue=_ln2_epilogue(x1, w["b_down"], w["ln2_g"], w["ln2_b"]))
    saved = dict(x=x, x16=x16, proj=proj, proj_ab=proj_ab, conv_out=conv_out, qkvn=qkvn, gb=gb, o_f=o_f, o_b=o_b,
                 s_f=s_f, s_b=s_b, t_f=t_f, t_b=t_b, og=og, s=s, y_a=y_a, y_b=y_b, mixed=mixed, r1=r1, x1=x1,
                 x1_16=x1_16, hpre=hpre, h=h, r2=r2)
    return x2, x2_16, saved, gathered


def _backward_layer(dx2, w, a, t, pending):
    (dx1_a, dr2), (db_down, dg2, db2) = _ln2_bwd(a["x1"], a["r2"], w["b_down"], w["ln2_g"], w["ln2_b"], dx2, t)
    relu2_back = _relu2_back_epilogue(a["hpre"], w["b_up"])
    if pending is None:
        dhpre, db_up_rows = _mm(dr2, w["w_down"], "nt", "d_h", epilogue=relu2_back)
        dx1 = _mm(dhpre, w["w_up"], "nt", "d_x1", addends=[(1.0, dx1_a)])
        landed = None
    else:
        (dhpre, db_up_rows), landed_w_in = _mm(
            dr2, w["w_down"], "nt", "d_h_scatter", epilogue=relu2_back,
            exchange=(_scatter_plan, pending[:1], _landed_shapes(pending[:1])))
        dx1, landed_rest = _mm(dhpre, w["w_up"], "nt", "d_x1_scatter", addends=[(1.0, dx1_a)],
                               exchange=(_scatter_plan, pending[1:], _landed_shapes(pending[1:])))
        landed = [landed_w_in[0], landed_rest[0]]
    db_up = jnp.sum(db_up_rows, axis=0, keepdims=True)
    dw_down = _mm(a["h"], dr2, "tn", "dw_down")
    dw_up = _mm(a["x1_16"], dhpre, "tn", "dw_up")
    (dx_a, dr1), (dg1, db1) = _ln1_bwd(a["x"], a["r1"], w["ln1_g"], w["ln1_b"], dx1, t)
    dmixed = _mm(dr1, w["w_out"], "nt", "d_mixed", out_dtype=BF16)
    dw_out = _mm(a["mixed"], dr1, "tn", "dw_out")
    dgates, dy_a, dy_b = _mix_bwd(a["proj"], a["y_a"], a["y_b"], dmixed, t)
    dog = _mm(dy_a, w["w_o_gdn"], "nt", "d_og", out_dtype=BF16)
    dw_o_gdn = _mm(a["og"], dy_a, "tn", "dw_o_gdn")
    ds = _mm(dy_b, w["w_o_sc"], "nt", "d_s")
    dw_o_sc = _mm(a["s"], dy_b, "tn", "dw_o_sc")
    dsc, dconv_sc = _sc_bwd(a["proj"], w["conv_sc"], ds, t)
    do, dz, dnorm_w = _gate_norm_bwd(a["o_f"], a["o_b"], a["proj"], w["gdn_norm_w"], dog, t)
    dq_f, dq_b, dgb_f, dgb_b = _gdn_bwd(a["qkvn"], a["gb"], a["s_f"], a["s_b"], a["t_f"], a["t_b"], do, t)
    dqkv, dconv_qkv = _qkv_conv_bwd(a["proj"], w["conv_qkv"], a["conv_out"], dq_f, dq_b, t)
    dab, da_log, ddt_bias = _gating_bwd(a["proj_ab"], w["a_log"], w["dt_bias"], dgb_f, dgb_b, t)

    pieces = [(dqkv, 0), (dz, 3 * D_MODEL), (dsc, 4 * D_MODEL), (dgates, 7 * D_MODEL)]
    dx = _mm(dab, w["w_ab"], "nt", "dx_ab", addends=[(1.0, dx_a)])
    dw_main = []
    for j, (piece, off) in enumerate(pieces):
        dx = _mm(piece, w["w_main"], "nt", f"dx_{j}", b_off=off, kdim=piece.shape[1], addends=[(1.0, dx)])
        dw_main.append(_mm(a["x16"], piece, "tn", f"dw_in_{j}"))
    dw_ab = _mm(a["x16"], dab, "tn", "dw_ab")
    dw_main = jnp.concatenate(dw_main, axis=1)
    dw_in = jnp.concatenate([dw_main[:, :QKVZ_COLS], dw_ab[:, :AB_COLS], dw_main[:, QKVZ_COLS:]], axis=1)
    grads = dict(w_in=dw_in, conv_qkv=dconv_qkv[:3], a_log=da_log[0, :2 * N_HEADS].reshape(2, N_HEADS),
                 dt_bias=ddt_bias[0, :2 * N_HEADS].reshape(2, N_HEADS), gdn_norm_w=dnorm_w[0], w_o_gdn=dw_o_gdn,
                 conv_sc=dconv_sc[:3], w_o_sc=dw_o_sc, w_out=dw_out, ln1_g=dg1[0], ln1_b=db1[0], w_up=dw_up,
                 b_up=db_up[0], w_down=dw_down, b_down=db_down[0], ln2_g=dg2[0], ln2_b=db2[0])
    return dx, grads, landed


def kernel(x, w_in, conv_qkv, a_log, dt_bias, gdn_norm_w, w_o_gdn, conv_sc, w_o_sc, w_out, ln1_g, ln1_b, w_up, b_up, w_down, b_down, ln2_g, ln2_b, loss_target, m_w_in, m_conv_qkv, m_a_log, m_dt_bias, m_gdn_norm_w, m_w_o_gdn, m_conv_sc, m_w_o_sc, m_w_out, m_ln1_g, m_ln1_b, m_w_up, m_b_up, m_w_down, m_b_down, m_ln2_g, m_ln2_b, v_w_in, v_conv_qkv, v_a_log, v_dt_bias, v_gdn_norm_w, v_w_o_gdn, v_conv_sc, v_w_o_sc, v_w_out, v_ln1_g, v_ln1_b, v_w_up, v_b_up, v_w_down, v_b_down, v_ln2_g, v_ln2_b):
    weights = dict(w_in=w_in, conv_qkv=conv_qkv, a_log=a_log, dt_bias=dt_bias, gdn_norm_w=gdn_norm_w,
                   w_o_gdn=w_o_gdn, conv_sc=conv_sc, w_o_sc=w_o_sc, w_out=w_out, ln1_g=ln1_g, ln1_b=ln1_b,
                   w_up=w_up, b_up=b_up, w_down=w_down, b_down=b_down, ln2_g=ln2_g, ln2_b=ln2_b)
    m_in = dict(w_in=m_w_in, conv_qkv=m_conv_qkv, a_log=m_a_log, dt_bias=m_dt_bias, gdn_norm_w=m_gdn_norm_w,
                w_o_gdn=m_w_o_gdn, conv_sc=m_conv_sc, w_o_sc=m_w_o_sc, w_out=m_w_out, ln1_g=m_ln1_g, ln1_b=m_ln1_b,
                w_up=m_w_up, b_up=m_b_up, w_down=m_w_down, b_down=m_b_down, ln2_g=m_ln2_g, ln2_b=m_ln2_b)
    v_in = dict(w_in=v_w_in, conv_qkv=v_conv_qkv, a_log=v_a_log, dt_bias=v_dt_bias, gdn_norm_w=v_gdn_norm_w,
                w_o_gdn=v_w_o_gdn, conv_sc=v_conv_sc, w_o_sc=v_w_o_sc, w_out=v_w_out, ln1_g=v_ln1_g, ln1_b=v_ln1_b,
                w_up=v_w_up, b_up=v_b_up, w_down=v_w_down, b_down=v_b_down, ln2_g=v_ln2_g, ln2_b=v_ln2_b)
    t = x.shape[1]
    depth = w_in.shape[0]

    def layer_weights(full, l):
        w_in_l = full["w_in"][0]
        return dict(
            w_main=jnp.concatenate([w_in_l[:, :QKVZ_COLS], w_in_l[:, QKVZ_COLS + AB_COLS:]], axis=1),
            w_ab=jnp.pad(w_in_l[:, QKVZ_COLS:QKVZ_COLS + AB_COLS], ((0, 0), (0, LANES - AB_COLS))),
            conv_qkv=jnp.pad(full["conv_qkv"][0], ((0, SUBLANES - 3), (0, 0))),
            conv_sc=jnp.pad(full["conv_sc"][0], ((0, SUBLANES - 3), (0, 0))),
            a_log=_lane_row(a_log[l]), dt_bias=_lane_row(dt_bias[l]), gdn_norm_w=gdn_norm_w[l][None, :],
            w_o_gdn=full["w_o_gdn"][0], w_o_sc=full["w_o_sc"][0], w_out=full["w_out"][0],
            ln1_g=ln1_g[l][None, :], ln1_b=ln1_b[l][None, :], w_up=full["w_up"][0], b_up=b_up[l][None, :],
            w_down=full["w_down"][0], b_down=b_down[l][None, :], ln2_g=ln2_g[l][None, :], ln2_b=ln2_b[l][None, :])

    packed = [_pack_weights({name: weights[name][l:l + 1] for name in MATRICES + CONVS}) for l in range(depth)]
    gathered = _exchange_call("weights_all_gather", _gather_plan, packed[0][0], _gathered_shapes(packed[0][0]))
    h = x.reshape(t, D_MODEL)
    h16 = h.astype(BF16)
    layers, saved = [], []
    for l in range(depth):
        layers.append(layer_weights(_unpack_weights(gathered, packed[l][1]), l))
        next_arrays = packed[l + 1][0] if l + 1 < depth else None
        h, h16, acts, gathered = _forward_layer(h, h16, layers[l], t, next_arrays)
        saved.append(acts)
    dh, loss_local = _loss_stage(h, loss_target.reshape(t, D_MODEL), t)
    loss = lax.psum(loss_local, MESH_AXES)

    layer_shapes = {name: weights[name].shape[1:] for name in SHARDED}
    layer_grads, reduced, pending = [None] * depth, [None] * depth, None
    for l in reversed(range(depth)):
        dh, layer_grads[l], landed = _backward_layer(dh, layers[l], saved[l], t, pending)
        if landed is not None:
            reduced[l + 1] = _sum_layer_grads(landed, layer_shapes)
        pending = _pack_layer_grads(layer_grads[l])
    small = _pack_small_grads({name: jnp.stack([g[name] for g in layer_grads]) for name in SMALL})
    last = pending + [small]
    landed = _exchange_call("grads_scatter", _scatter_plan, last, _landed_shapes(last))
    reduced[0] = _sum_layer_grads(landed[:2], layer_shapes)
    grads = {name: jnp.stack([r[name] for r in reduced]) for name in SHARDED}
    small_shapes = {name: weights[name].shape for name in SMALL}
    grads.update(_unpack(_sum_slots(landed[2], "grads_sum_small").reshape(-1), SMALL, small_shapes))

    names = list(weights)
    updates = {n: _adamw(weights[n], grads[n], m_in[n], v_in[n], f"adamw_{n}") for n in names}
    return (loss, dh.reshape(x.shape), *[grads[n] for n in names], *[updates[n][0] for n in names],
            *[updates[n][1] for n in names], *[updates[n][2] for n in names])
```

```python
import functools

import jax
import jax.numpy as jnp
from jax import lax
from jax.experimental import pallas as pl
from jax.experimental.pallas import tpu as pltpu

F32 = jnp.float32
BF16 = jnp.bfloat16

D_MODEL = 1024
N_HEADS = 8
HEAD_DIM = 128
CHUNK = 64
D_FF = 4 * D_MODEL
DEPTH = 4
N_DEV = 8
LN_EPS = 1e-5
RMS_EPS = 1e-6
L2_EPS = 1e-6
ALPHA = (2 * DEPTH) ** 0.25
MAIN_COLS = 9 * D_MODEL
QKVZ_COLS = 4 * D_MODEL
AB_COLS = 4 * N_HEADS
W_IN_COLS = MAIN_COLS + AB_COLS
LANES = 128
SUBLANES = 8
VMEM_LIMIT = 48 * 1024 * 1024
MM_TILE = 1024
LN_MM_ROWS = 512
ROW_TILE = 256
WIDE_ROW_TILE = 128
ADAM = dict(lr=0.001, b1=0.9, b2=0.999, eps=1e-08, wd=0.01, step=10)
MESH_AXES = ("x", "y", "c")


def _params(sem=None):
    return pltpu.CompilerParams(dimension_semantics=sem, vmem_limit_bytes=VMEM_LIMIT)


_DIMS = {"nn": (1, 0), "nt": (1, 1), "tn": (0, 0)}


def _mm(a, b, mode, name, *, out_dtype=F32, addends=(), epilogue=None, exchange=None, tm=None):
    if mode == "nn":
        (m, k), n = a.shape, b.shape[1]
    elif mode == "nt":
        (m, k), n = a.shape, b.shape[0]
    else:
        (k, m), n = a.shape, b.shape[1]
    tm, tn, tk = min(m, tm or MM_TILE), min(n, MM_TILE), min(k, MM_TILE)
    assert m % tm == 0 and n % tn == 0 and k % tk == 0
    nk = k // tk
    ca, cb = _DIMS[mode]
    scales = tuple(s for s, _ in addends)
    na = len(addends)
    epi_fn, epi_in, epi_out = epilogue or (None, (), (("tile", out_dtype),))
    ne, no = len(epi_in), len(epi_out)
    plan, x_arrays, x_shapes = exchange or (None, (), ())
    nx = len(x_arrays)
    gi, gj = m // tm, n // tn

    def body(a_ref, b_ref, *rest):
        add_refs, epi_refs = rest[:na], rest[na:na + ne]
        xin_refs, rest = rest[na + ne:na + ne + nx], rest[na + ne + nx:]
        o_refs, xout_refs, rest = rest[:no], rest[no:no + nx], rest[no + nx:]
        kk = pl.program_id(2)
        if plan:
            start, forward, finish_exchange = plan(xin_refs, xout_refs, *rest[-3:])
            i, j = pl.program_id(0), pl.program_id(1)

            @pl.when((i == 0) & (j == 0) & (kk == 0))
            def _():
                start()

            @pl.when((i == gi // 2) & (j == gj // 2) & (kk == 0))
            def _():
                forward()

        p = lax.dot_general(a_ref[...].astype(BF16), b_ref[...].astype(BF16), (((ca,), (cb,)), ((), ())),
                            preferred_element_type=F32)

        def finish(r):
            for s, ref in zip(scales, add_refs):
                r = r + s * ref[...].astype(F32)
            vals = epi_fn(r, *[ref[...] for ref in epi_refs]) if epi_fn else (r,)
            for (kind, _), ref, val in zip(epi_out, o_refs, vals):
                if kind == "colsum":
                    val = jnp.where(lax.broadcasted_iota(jnp.int32, ref.shape, 0) == 0, val, 0.0)
                ref[...] = val.astype(ref.dtype)

        if nk == 1:
            finish(p)
        else:
            acc = rest[0]

            @pl.when(kk == 0)
            def _():
                acc[...] = p

            @pl.when(kk > 0)
            def _():
                acc[...] += p

            @pl.when(kk == nk - 1)
            def _():
                finish(acc[...])

        if plan:
            @pl.when((i == gi - 1) & (j == gj - 1) & (kk == nk - 1))
            def _():
                finish_exchange()

    if mode == "nn":
        a_spec = pl.BlockSpec((tm, tk), lambda i, j, kk: (i, kk))
        b_spec = pl.BlockSpec((tk, tn), lambda i, j, kk: (kk, j))
    elif mode == "nt":
        a_spec = pl.BlockSpec((tm, tk), lambda i, j, kk: (i, kk))
        b_spec = pl.BlockSpec((tn, tk), lambda i, j, kk: (j, kk))
    else:
        a_spec = pl.BlockSpec((tk, tm), lambda i, j, kk: (kk, i))
        b_spec = pl.BlockSpec((tk, tn), lambda i, j, kk: (kk, j))
    kinds = {"tile": (pl.BlockSpec((tm, tn), lambda i, j, kk: (i, j)), (m, n)),
             "row": (pl.BlockSpec((1, tn), lambda i, j, kk: (0, j)), (1, n)),
             "colsum": (pl.BlockSpec((SUBLANES, tn), lambda i, j, kk: (i, j)), (SUBLANES * (m // tm), n))}
    o_spec = kinds["tile"][0]
    any_spec = pl.BlockSpec(memory_space=pl.ANY)
    res = pl.pallas_call(
        body, name=name, grid=(gi, gj, nk),
        in_specs=[a_spec, b_spec] + [o_spec] * na + [kinds[kind][0] for _, kind in epi_in] + [any_spec] * nx,
        out_specs=[kinds[kind][0] for kind, _ in epi_out] + [any_spec] * nx,
        out_shape=[jax.ShapeDtypeStruct(kinds[kind][1], dt) for kind, dt in epi_out] + list(x_shapes),
        scratch_shapes=([pltpu.VMEM((tm, tn), F32)] if nk > 1 else []) + (_exchange_sems(nx) if plan else []),
        compiler_params=_params(("arbitrary",) * 3 if plan else ("parallel", "parallel", "arbitrary")),
    )(a, b, *[arr for _, arr in addends], *[arr for arr, _ in epi_in], *x_arrays)
    if plan:
        return (tuple(res[:no]) if epilogue else res[0]), tuple(res[no:])
    return res if epilogue else res[0]


def _mm_nt_pieces(pieces, b, name, addend):
    m, n = pieces[0].shape[0], b.shape[0]
    tm, tn, tk = min(m, MM_TILE), min(n, MM_TILE), min(b.shape[1], MM_TILE)
    assert m % tm == 0 and n % tn == 0 and all(p.shape[1] % tk == 0 for p in pieces)
    counts = [p.shape[1] // tk for p in pieces]
    starts = [sum(counts[:j]) for j in range(len(pieces))]
    nk, npieces = sum(counts), len(pieces)
    assert nk * tk == b.shape[1]

    def body(*refs):
        a_refs, b_ref, add_ref, o_ref, acc = refs[:npieces], *refs[npieces:]
        kk = pl.program_id(2)
        for a_ref, start, count in zip(a_refs, starts, counts):
            @pl.when((kk >= start) & (kk < start + count))
            def _(a_ref=a_ref):
                p = lax.dot_general(a_ref[...].astype(BF16), b_ref[...].astype(BF16), (((1,), (1,)), ((), ())),
                                    preferred_element_type=F32)

                @pl.when(kk == 0)
                def _():
                    acc[...] = p

                @pl.when(kk > 0)
                def _():
                    acc[...] += p

        @pl.when(kk == nk - 1)
        def _():
            o_ref[...] = acc[...] + add_ref[...]

    a_specs = [pl.BlockSpec((tm, tk), lambda i, j, kk, s=s, c=c: (i, jnp.clip(kk - s, 0, c - 1)))
               for s, c in zip(starts, counts)]
    o_spec = pl.BlockSpec((tm, tn), lambda i, j, kk: (i, j))
    return pl.pallas_call(
        body, name=name, grid=(m // tm, n // tn, nk),
        in_specs=a_specs + [pl.BlockSpec((tn, tk), lambda i, j, kk: (j, kk)), o_spec], out_specs=o_spec,
        out_shape=jax.ShapeDtypeStruct((m, n), F32), scratch_shapes=[pltpu.VMEM((tm, tn), F32)],
        compiler_params=_params(("parallel", "parallel", "arbitrary")),
    )(*pieces, b, addend)


def _tile_call(name, body, t, tm, tiled, halo, params, outs, accs):
    tm = min(tm, t)
    assert t % tm == 0 and tm % SUBLANES == 0
    steps = t // tm
    hb = tm // SUBLANES
    nt, nh, npar, no = len(tiled), len(halo), len(params), len(outs)

    def kern(*refs):
        i = pl.program_id(0)
        t_refs = refs[:nt + nh]
        h_refs = refs[nt + nh:nt + 3 * nh]
        p_refs = refs[nt + 3 * nh:nt + 3 * nh + npar]
        o_refs = refs[nt + 3 * nh + npar:nt + 3 * nh + npar + no]
        a_refs = refs[nt + 3 * nh + npar + no:]
        tiles = [r[...].astype(F32) for r in t_refs]
        halos = []
        for j in range(nh):
            prev = h_refs[2 * j][SUBLANES - 1:SUBLANES, :].astype(F32)
            nxt = h_refs[2 * j + 1][0:1, :].astype(F32)
            halos.append((jnp.where(i > 0, prev, 0.0), jnp.where(i < steps - 1, nxt, 0.0)))
        o_vals, a_vals = body(tiles, halos, [r[...] for r in p_refs])
        for ref, val in zip(o_refs, o_vals):
            ref[...] = val.astype(ref.dtype)
        for ref, val in zip(a_refs, a_vals):
            @pl.when(i == 0)
            def _(ref=ref, val=val):
                ref[...] = val

            @pl.when(i > 0)
            def _(ref=ref, val=val):
                ref[...] += val

    in_specs, args = [], []
    for arr, nc, cb in list(tiled) + list(halo):
        in_specs.append(pl.BlockSpec((tm, nc), lambda i, cb=cb: (i, cb)))
        args.append(arr)
    last = t // SUBLANES - 1
    for arr, nc, cb in halo:
        in_specs.append(pl.BlockSpec((SUBLANES, nc), lambda i, cb=cb: (jnp.maximum(i * hb - 1, 0), cb)))
        in_specs.append(pl.BlockSpec((SUBLANES, nc), lambda i, cb=cb: (jnp.minimum((i + 1) * hb, last), cb)))
        args += [arr, arr]
    for arr in params:
        in_specs.append(pl.BlockSpec(arr.shape, lambda i: (0, 0)))
        args.append(arr)
    out_specs = [pl.BlockSpec((tm, nc), lambda i: (i, 0)) for nc, _ in outs]
    out_specs += [pl.BlockSpec(shape, lambda i: (0, 0)) for shape in accs]
    out_shape = [jax.ShapeDtypeStruct((t, nc), dt) for nc, dt in outs]
    out_shape += [jax.ShapeDtypeStruct(shape, F32) for shape in accs]
    res = pl.pallas_call(kern, name=name, grid=(steps,), in_specs=in_specs, out_specs=out_specs,
                         out_shape=out_shape, compiler_params=_params(("arbitrary",)))(*args)
    return res[:no], res[no:]


def _row_iota(x):
    return lax.broadcasted_iota(jnp.int32, x.shape, 0)


def _lane_iota(x):
    return lax.broadcasted_iota(jnp.int32, x.shape, 1)


def _shift_down(x, first_row):
    return jnp.where(_row_iota(x) == 0, first_row, pltpu.roll(x, 1, 0))


def _shift_up(x, last_row):
    n = x.shape[0]
    return jnp.where(_row_iota(x) == n - 1, last_row, pltpu.roll(x, n - 1, 0))


def _taps(w):
    return w[0:1, :], w[1:2, :], w[2:3, :]


def _tap_rows(d0, d1, d2, rows=SUBLANES):
    r = lax.broadcasted_iota(jnp.int32, (rows, d0.shape[1]), 0)
    return jnp.where(r == 0, d0, jnp.where(r == 1, d1, jnp.where(r == 2, d2, 0.0)))


def _colsum(x):
    return jnp.sum(x, axis=0, keepdims=True)


def _silu(x):
    return x * jax.nn.sigmoid(x)


def _softplus(x):
    return jnp.maximum(x, 0.0) + jnp.log(1.0 + jnp.exp(-jnp.abs(x)))


def _heads(x):
    return [x[:, h * HEAD_DIM:(h + 1) * HEAD_DIM] for h in range(x.shape[1] // HEAD_DIM)]


def _post_conv(c):
    blocks = _heads(_silu(c))
    out = []
    for j, blk in enumerate(blocks):
        if j < 2 * N_HEADS:
            blk = blk * lax.rsqrt(jnp.sum(blk * blk, axis=-1, keepdims=True) + L2_EPS)
        if j < N_HEADS:
            blk = blk * (HEAD_DIM ** -0.5)
        out.append(blk)
    return jnp.concatenate(out, axis=1)


def _gating(ab, a_log, dt_bias):
    lane = _lane_iota(ab)
    g = -jnp.exp(a_log) * _softplus(ab + dt_bias)
    return jnp.where(lane < 2 * N_HEADS, g, jnp.where(lane < AB_COLS, jax.nn.sigmoid(ab), 0.0))


def _gate_norm(o_f, o_b, z, norm_w):
    out = []
    for oh, zh in zip(_heads(o_f + o_b), _heads(z)):
        out.append(oh * lax.rsqrt(jnp.mean(oh * oh, axis=-1, keepdims=True) + RMS_EPS) * norm_w * _silu(zh))
    return jnp.concatenate(out, axis=1)


def _mix(gate_a, gate_b, y_a, y_b):
    return jax.nn.sigmoid(gate_a) * y_a + jax.nn.sigmoid(gate_b) * y_b


def _layer_norm(u, g, b):
    mu = jnp.mean(u, axis=-1, keepdims=True)
    var = jnp.mean(jnp.square(u - mu), axis=-1, keepdims=True)
    return (u - mu) * lax.rsqrt(var + LN_EPS) * g + b


def _ln1(x, r, g, b):
    return _layer_norm(ALPHA * x + r, g, b)


def _ln2(x, r, bias, g, b):
    return _layer_norm(ALPHA * x + r + bias, g, b)


def _relu2(hpre, bias):
    return jnp.square(jnp.maximum(hpre + bias, 0.0))


def _qkv_conv_fwd(proj, conv_w, t):
    def body(tiles, halos, params):
        (x,), ((xp, xn),), (w,) = tiles, halos, params
        w0, w1, w2 = _taps(w)
        c = w0 * _shift_down(x, xp) + w1 * x + w2 * _shift_up(x, xn)
        return [c, _post_conv(c)], []

    (c, qkvn), _ = _tile_call("qkv_conv_fwd", body, t, WIDE_ROW_TILE, [], [(proj, 3 * D_MODEL, 0)], [conv_w],
                              [(3 * D_MODEL, F32), (3 * D_MODEL, F32)], [])
    return c, qkvn


def _gating_fwd(proj_ab, a_log, dt_bias, t):
    def body(tiles, halos, params):
        return [_gating(tiles[0], params[0], params[1])], []

    (gb,), _ = _tile_call("gating_fwd", body, t, ROW_TILE, [(proj_ab, LANES, 0)], [], [a_log, dt_bias],
                          [(LANES, F32)], [])
    return gb


def _gate_norm_fwd(o_f, o_b, proj, norm_w, t):
    def body(tiles, halos, params):
        return [_gate_norm(tiles[0], tiles[1], tiles[2], params[0])], []

    (og,), _ = _tile_call("gate_norm_fwd", body, t, ROW_TILE,
                          [(o_f, D_MODEL, 0), (o_b, D_MODEL, 0), (proj, D_MODEL, 3)], [], [norm_w],
                          [(D_MODEL, BF16)], [])
    return og


def _sc_fwd(proj, conv_w, t):
    def body(tiles, halos, params):
        (sb,), ((cp, cn), (xp, xn)), (w,) = tiles[:1], halos, params
        sc, sx = tiles[1], tiles[2]
        w0, w1, w2 = _taps(w)
        u = sc * sx
        return [sb * (w0 * _shift_down(u, cp * xp) + w1 * u + w2 * _shift_up(u, cn * xn))], []

    (s,), _ = _tile_call("sc_fwd", body, t, ROW_TILE, [(proj, D_MODEL, 4)],
                         [(proj, D_MODEL, 5), (proj, D_MODEL, 6)], [conv_w], [(D_MODEL, BF16)], [])
    return s


def _mix_fwd(proj, y_a, y_b, t):
    def body(tiles, halos, params):
        return [_mix(*tiles)], []

    (mixed,), _ = _tile_call("mix_fwd", body, t, ROW_TILE,
                             [(proj, D_MODEL, 7), (proj, D_MODEL, 8), (y_a, D_MODEL, 0), (y_b, D_MODEL, 0)], [], [],
                             [(D_MODEL, BF16)], [])
    return mixed


LN_OUTS = [("tile", F32), ("tile", F32), ("tile", BF16)]


def _ln1_epilogue(x, g, b):
    def fn(r, xv, gv, bv):
        y = _ln1(xv, r, gv, bv)
        return r, y, y

    return fn, [(x, "tile"), (g, "row"), (b, "row")], LN_OUTS


def _ln2_epilogue(x, bias, g, b):
    def fn(r, xv, biasv, gv, bv):
        y = _ln2(xv, r, biasv, gv, bv)
        return r, y, y

    return fn, [(x, "tile"), (bias, "row"), (g, "row"), (b, "row")], LN_OUTS


def _relu2_epilogue(bias):
    return (lambda r, b: (r, _relu2(r, b))), [(bias, "row")], [("tile", F32), ("tile", BF16)]


def _relu2_back_epilogue(hpre, bias):
    def fn(r, hp, b):
        _, vjp = jax.vjp(_relu2, hp, b)
        return vjp(r)

    return fn, [(hpre, "tile"), (bias, "row")], [("tile", BF16), ("colsum", F32)]


def _loss_stage(y, target, t):
    def body(tiles, halos, params):
        d = tiles[0] - tiles[1]
        part = 0.5 * jnp.sum(jnp.mean(d * d, axis=-1, keepdims=True), axis=0, keepdims=True)
        return [d * (1.0 / D_MODEL)], [jnp.broadcast_to(part, (1, LANES))]

    (dy,), (loss,) = _tile_call("loss", body, t, ROW_TILE, [(y, D_MODEL, 0), (target, D_MODEL, 0)], [], [],
                                [(D_MODEL, F32)], [(1, LANES)])
    return dy, loss[0, 0]


def _ln2_bwd(x, r, bias, g, b, dy, t):
    def body(tiles, halos, params):
        _, vjp = jax.vjp(_ln2, tiles[0], tiles[1], params[0], params[1], params[2])
        dx, dr, dbias, dg, db = vjp(tiles[2])
        return [dx, dr], [dbias, dg, db]

    return _tile_call("ln2_bwd", body, t, ROW_TILE, [(x, D_MODEL, 0), (r, D_MODEL, 0), (dy, D_MODEL, 0)], [],
                      [bias, g, b], [(D_MODEL, F32), (D_MODEL, BF16)], [(1, D_MODEL)] * 3)


def _ln1_bwd(x, r, g, b, dy, t):
    def body(tiles, halos, params):
        _, vjp = jax.vjp(_ln1, tiles[0], tiles[1], params[0], params[1])
        dx, dr, dg, db = vjp(tiles[2])
        return [dx, dr], [dg, db]

    return _tile_call("ln1_bwd", body, t, ROW_TILE, [(x, D_MODEL, 0), (r, D_MODEL, 0), (dy, D_MODEL, 0)], [],
                      [g, b], [(D_MODEL, F32), (D_MODEL, BF16)], [(1, D_MODEL)] * 2)


def _mix_bwd(proj, y_a, y_b, dmixed, t):
    def body(tiles, halos, params):
        _, vjp = jax.vjp(_mix, *tiles[:4])
        dga, dgb, dya, dyb = vjp(tiles[4])
        return [jnp.concatenate([dga, dgb], axis=1), dya, dyb], []

    (dgates, dya, dyb), _ = _tile_call(
        "mix_bwd", body, t, ROW_TILE,
        [(proj, D_MODEL, 7), (proj, D_MODEL, 8), (y_a, D_MODEL, 0), (y_b, D_MODEL, 0), (dmixed, D_MODEL, 0)], [], [],
        [(2 * D_MODEL, BF16), (D_MODEL, BF16), (D_MODEL, BF16)], [])
    return dgates, dya, dyb


def _sc_bwd(proj, conv_w, ds, t):
    def body(tiles, halos, params):
        ds_, sb, sc, sx = tiles
        (dsp, dsn), (sbp, sbn), (scp, scn), (sxp, sxn) = halos
        w0, w1, w2 = _taps(params[0])
        u = sc * sx
        u_prev, u_next = _shift_down(u, scp * sxp), _shift_up(u, scn * sxn)
        dconv = ds_ * sb
        du = w0 * _shift_up(dconv, dsn * sbn) + w1 * dconv + w2 * _shift_down(dconv, dsp * sbp)
        dsb = ds_ * (w0 * u_prev + w1 * u + w2 * u_next)
        dw = _tap_rows(_colsum(dconv * u_prev), _colsum(dconv * u), _colsum(dconv * u_next))
        return [jnp.concatenate([dsb, du * sx, du * sc], axis=1)], [dw]

    (dsc,), (dw,) = _tile_call("sc_bwd", body, t, ROW_TILE, [],
                               [(ds, D_MODEL, 0), (proj, D_MODEL, 4), (proj, D_MODEL, 5), (proj, D_MODEL, 6)],
                               [conv_w], [(3 * D_MODEL, BF16)], [(SUBLANES, D_MODEL)])
    return dsc, dw


def _gate_norm_bwd(o_f, o_b, proj, norm_w, dog, t):
    def body(tiles, halos, params):
        _, vjp = jax.vjp(_gate_norm, tiles[0], tiles[1], tiles[2], params[0])
        do, _, dz, dnw = vjp(tiles[3])
        return [do, dz], [dnw]

    (do, dz), (dnw,) = _tile_call(
        "gate_norm_bwd", body, t, ROW_TILE,
        [(o_f, D_MODEL, 0), (o_b, D_MODEL, 0), (proj, D_MODEL, 3), (dog, D_MODEL, 0)], [], [norm_w],
        [(D_MODEL, F32), (D_MODEL, BF16)], [(1, HEAD_DIM)])
    return do, dz, dnw


def _qkv_conv_bwd(proj, conv_w, c, dq_f, dq_b, t):
    def post_conv_back(cv, ct):
        _, vjp = jax.vjp(_post_conv, cv)
        return vjp(ct)[0]

    def body(tiles, halos, params):
        cv, df, db, x = tiles
        (cp, cn), (dfp, dfn), (dbp, dbn), (xp, xn) = halos
        w0, w1, w2 = _taps(params[0])
        d = post_conv_back(cv, df + db)
        d_prev, d_next = post_conv_back(cp, dfp + dbp), post_conv_back(cn, dfn + dbn)
        dx = w0 * _shift_up(d, d_next) + w1 * d + w2 * _shift_down(d, d_prev)
        dw = _tap_rows(_colsum(d * _shift_down(x, xp)), _colsum(d * x), _colsum(d * _shift_up(x, xn)))
        return [dx], [dw]

    wide = 3 * D_MODEL
    (dqkv,), (dw,) = _tile_call("qkv_conv_bwd", body, t, WIDE_ROW_TILE, [],
                                [(c, wide, 0), (dq_f, wide, 0), (dq_b, wide, 0), (proj, wide, 0)], [conv_w],
                                [(wide, BF16)], [(SUBLANES, wide)])
    return dqkv, dw


def _gating_bwd(proj_ab, a_log, dt_bias, dgb_f, dgb_b, t):
    def body(tiles, halos, params):
        _, vjp = jax.vjp(_gating, tiles[0], params[0], params[1])
        dab, dal, ddt = vjp(tiles[1] + tiles[2])
        return [dab], [dal, ddt]

    (dab,), (dal, ddt) = _tile_call("gating_bwd", body, t, ROW_TILE,
                                    [(proj_ab, LANES, 0), (dgb_f, LANES, 0), (dgb_b, LANES, 0)], [],
                                    [a_log, dt_bias], [(LANES, BF16)], [(1, LANES)] * 2)
    return dab, dal, ddt


@functools.partial(jax.custom_vjp, nondiff_argnums=(2, 3))
def _dot(a, b, ca, cb):
    return lax.dot_general(a.astype(BF16), b.astype(BF16), (((ca,), (cb,)), ((), ())), preferred_element_type=F32)


def _dot_fwd(a, b, ca, cb):
    return _dot(a, b, ca, cb), (a, b)


def _dot_bwd(ca, cb, res, ct):
    a, b = res
    fa, fb = 1 - ca, 1 - cb
    da = _dot(ct, b, 1, fb) if ca == 1 else _dot(b, ct, fb, 1)
    db = _dot(a, ct, fa, 0) if cb == 0 else _dot(ct, a, 0, fa)
    return da, db


_dot.defvjp(_dot_fwd, _dot_bwd)


def _split3(x):
    hi = x.astype(BF16)
    r1 = x - hi.astype(F32)
    mid = r1.astype(BF16)
    return hi, mid, (r1 - mid.astype(F32)).astype(BF16)


def _dot_exact(a, b, ca, cb, exact):
    dims = (((ca,), (cb,)), ((), ()))
    if exact == 0:
        return sum(lax.dot_general(a.astype(BF16), p, dims, preferred_element_type=F32) for p in _split3(b))
    return sum(lax.dot_general(p, b.astype(BF16), dims, preferred_element_type=F32) for p in _split3(a))


def _tri_masks(n, rev):
    r = lax.broadcasted_iota(jnp.int32, (n, n), 0)
    c = lax.broadcasted_iota(jnp.int32, (n, n), 1)
    return ((c >= r), (c > r)) if rev else ((c <= r), (c < r))


@functools.partial(jax.custom_vjp, nondiff_argnums=(1,))
def _cumsum_rows(g, rev):
    incl, _ = _tri_masks(g.shape[0], rev)
    return _dot_exact(incl.astype(F32), g, 1, 0, 0)


_cumsum_rows.defvjp(lambda g, rev: (_cumsum_rows(g, rev), None),
                    lambda rev, _, ct: (_cumsum_rows(ct, not rev),))


def _eye(n):
    return (lax.broadcasted_iota(jnp.int32, (n, n), 0) == lax.broadcasted_iota(jnp.int32, (n, n), 1)).astype(F32)


@jax.custom_vjp
def _to_rows(x):
    return _dot_exact(_eye(x.shape[1]), x, 1, 1, 0)


@jax.custom_vjp
def _to_cols(y):
    return _dot_exact(y, _eye(y.shape[0]), 0, 0, 1)


_to_rows.defvjp(lambda x: (_to_rows(x), None), lambda _, ct: (_to_cols(ct),))
_to_cols.defvjp(lambda y: (_to_cols(y), None), lambda _, ct: (_to_rows(ct),))


def _pick_col(arr, idx):
    return jnp.sum(jnp.where(_lane_iota(arr) == idx, arr, 0.0), axis=1, keepdims=True)


def _pick_row(arr, idx):
    return jnp.sum(jnp.where(_row_iota(arr) == idx, arr, 0.0), axis=0, keepdims=True)


def _chunk_gates(gb, direction, rev):
    n = gb.shape[0]
    incl, strict = _tri_masks(n, rev)
    gc = _cumsum_rows(gb, rev)
    gc_rows = _to_rows(gc)
    lanes = [direction * N_HEADS + h for h in range(N_HEADS)]
    cols = [_pick_col(gc, ln) for ln in lanes]
    rows = [_pick_row(gc_rows, ln) for ln in lanes]
    betas = [_pick_col(gb, 2 * N_HEADS + ln) for ln in lanes]
    decays = [jnp.where(incl, jnp.exp(jnp.where(incl, c - r, 0.0)), 0.0) for c, r in zip(cols, rows)]
    return cols, betas, decays, strict


def _chunk_gates_both(gb_f, gb_b):
    gates = dict(cols=[], betas=[], decays=[], lasts=[])
    for direction, gb in enumerate((gb_f, gb_b)):
        rev = direction == 1
        cols, betas, decays, _ = _chunk_gates(gb, direction, rev)
        last_idx = 0 if rev else gb.shape[0] - 1
        gates["cols"] += cols
        gates["betas"] += betas
        gates["decays"] += decays
        gates["lasts"] += [_pick_row(c, last_idx) for c in cols]
    return gates


def _chunk_lmat(k_f, k_b, gates):
    n = k_f.shape[0]
    stricts = [_tri_masks(n, False)[1]] * N_HEADS + [_tri_masks(n, True)[1]] * N_HEADS
    ks = _heads(k_f) + _heads(k_b)
    kk = [_dot(kh * b, kh, 1, 1) for kh, b in zip(ks, gates["betas"])]
    return tuple(jnp.where(s, x * d, 0.0) for s, x, d in zip(stricts, kk, gates["decays"]))


def _tri_inverse(lmats):
    n = lmats[0].shape[0]
    eye = _eye(n)
    powers = [-lm for lm in lmats]
    invs = [eye + p for p in powers]
    span = 2
    while span < n:
        powers = [_dot(p, p, 1, 0) for p in powers]
        steps = [_dot(p, i, 1, 0) for p, i in zip(powers, invs)]
        invs = [i + s for i, s in zip(invs, steps)]
        span *= 2
    return tuple(invs)


def _chunk_out(qkv_f, qkv_b, gates, tmats, states):
    qs, ks, vs = (_heads(qkv_f[p]) + _heads(qkv_b[p]) for p in range(3))
    cols, betas, decays, lasts = (gates[key] for key in ("cols", "betas", "decays", "lasts"))
    n = ks[0].shape[0]
    vk = [jnp.concatenate([vh * b, kh * b * jnp.exp(c)], axis=1) for vh, kh, b, c in zip(vs, ks, betas, cols)]
    uw = [_dot(tm, x, 1, 0) for tm, x in zip(tmats, vk)]
    attns = [_dot(qh, kh, 1, 1) * d for qh, kh, d in zip(qs, ks, decays)]
    wq = [jnp.concatenate([x[:, HEAD_DIM:], qh * jnp.exp(c)], axis=0) for x, qh, c in zip(uw, qs, cols)]
    wqs = [_dot(x, st, 1, 0) for x, st in zip(wq, states)]
    v_news = [x[:, :HEAD_DIM] - y[:n] for x, y in zip(uw, wqs)]
    inter = [y[n:] for y in wqs]
    intra = [_dot(a, vn, 1, 0) for a, vn in zip(attns, v_news)]
    adds = [_dot(kh * jnp.exp(l - c), vn, 0, 0) for kh, l, c, vn in zip(ks, lasts, cols, v_news)]
    new_states = tuple(st * jnp.exp(l) + a for st, l, a in zip(states, lasts, adds))
    outs = [x + y for x, y in zip(inter, intra)]
    return jnp.concatenate(outs[:N_HEADS], axis=1), jnp.concatenate(outs[N_HEADS:], axis=1), new_states


BOTH = 2 * N_HEADS


def _gdn_specs(n, first_backwards):
    idx = [(lambda i: n - 1 - i) if (d == 0) == first_backwards else (lambda i: i) for d in range(2)]

    def both(shape_of_block, index_tail):
        return [pl.BlockSpec(shape_of_block, lambda i, f=f: (f(i),) + index_tail) for f in idx]

    return idx, both


def _gdn_fwd(qkvn, gb, t):
    n = t // CHUNK
    idx, both = _gdn_specs(n, False)

    def body(qf, kf, vf, qb, kb, vb, gbf, gbb, of_ref, ob_ref, sf_ref, sb_ref, tf_ref, tb_ref, state):
        @pl.when(pl.program_id(0) == 0)
        def _():
            state[...] = jnp.zeros_like(state)

        qkv_f = (qf[...], kf[...], vf[...])
        qkv_b = (qb[...], kb[...], vb[...])
        gates = _chunk_gates_both(gbf[...], gbb[...])
        tmats = _tri_inverse(_chunk_lmat(qkv_f[1], qkv_b[1], gates))
        states = tuple(state[h] for h in range(BOTH))
        o_f, o_b, new_states = _chunk_out(qkv_f, qkv_b, gates, tmats, states)
        of_ref[...] = o_f
        ob_ref[...] = o_b
        for h in range(BOTH):
            s_ref, t_ref = (sf_ref, tf_ref) if h < N_HEADS else (sb_ref, tb_ref)
            s_ref[0, h % N_HEADS] = states[h]
            t_ref[0, h % N_HEADS] = tmats[h]
            state[h] = new_states[h]

    qkv_specs = [pl.BlockSpec((CHUNK, D_MODEL), lambda i, f=f, p=p: (f(i), p)) for f in idx for p in range(3)]
    return pl.pallas_call(
        body, name="gdn_fwd", grid=(n,),
        in_specs=qkv_specs + both((CHUNK, LANES), (0,)),
        out_specs=both((CHUNK, D_MODEL), (0,)) + both((1, N_HEADS, HEAD_DIM, HEAD_DIM), (0, 0, 0))
        + both((1, N_HEADS, CHUNK, CHUNK), (0, 0, 0)),
        out_shape=[jax.ShapeDtypeStruct((t, D_MODEL), F32)] * 2
        + [jax.ShapeDtypeStruct((n, N_HEADS, HEAD_DIM, HEAD_DIM), F32)] * 2
        + [jax.ShapeDtypeStruct((n, N_HEADS, CHUNK, CHUNK), F32)] * 2,
        scratch_shapes=[pltpu.VMEM((BOTH, HEAD_DIM, HEAD_DIM), F32)],
        compiler_params=_params(("arbitrary",)),
    )(*([qkvn] * 6), gb, gb)


def _gdn_bwd(qkvn, gb, s_f, s_b, t_f, t_b, do, t):
    n = t // CHUNK
    idx, both = _gdn_specs(n, True)

    def body(qf, kf, vf, qb, kb, vb, gbf, gbb, sf_ref, sb_ref, tf_ref, tb_ref, dof, dob,
             dqf_ref, dqb_ref, dgf_ref, dgb_ref, dstate):
        @pl.when(pl.program_id(0) == 0)
        def _():
            dstate[...] = jnp.zeros_like(dstate)

        qkv_f = (qf[...], kf[...], vf[...])
        qkv_b = (qb[...], kb[...], vb[...])
        tmats = tuple((tf_ref if h < N_HEADS else tb_ref)[0, h % N_HEADS] for h in range(BOTH))
        states = tuple((sf_ref if h < N_HEADS else sb_ref)[0, h % N_HEADS] for h in range(BOTH))
        gates, gates_vjp = jax.vjp(_chunk_gates_both, gbf[...], gbb[...])
        _, out_vjp = jax.vjp(_chunk_out, qkv_f, qkv_b, gates, tmats, states)
        d_f, d_b, dgates, dtm, dst = out_vjp((dof[...], dob[...], tuple(dstate[h] for h in range(BOTH))))
        firsts = [_dot(tm, d, 0, 0) for tm, d in zip(tmats, dtm)]
        dlm = tuple(-_dot(x, tm, 1, 1) for x, tm in zip(firsts, tmats))
        _, lmat_vjp = jax.vjp(_chunk_lmat, qkv_f[1], qkv_b[1], gates)
        dk_f, dk_b, dgates_lmat = lmat_vjp(dlm)
        dg_f, dg_b = gates_vjp(jax.tree.map(jnp.add, dgates, dgates_lmat))
        dqf_ref[...] = jnp.concatenate([d_f[0], d_f[1] + dk_f, d_f[2]], axis=1)
        dqb_ref[...] = jnp.concatenate([d_b[0], d_b[1] + dk_b, d_b[2]], axis=1)
        dgf_ref[...] = dg_f
        dgb_ref[...] = dg_b
        for h in range(BOTH):
            dstate[h] = dst[h]

    qkv_specs = [pl.BlockSpec((CHUNK, D_MODEL), lambda i, f=f, p=p: (f(i), p)) for f in idx for p in range(3)]
    return pl.pallas_call(
        body, name="gdn_bwd", grid=(n,),
        in_specs=qkv_specs + both((CHUNK, LANES), (0,)) + both((1, N_HEADS, HEAD_DIM, HEAD_DIM), (0, 0, 0))
        + both((1, N_HEADS, CHUNK, CHUNK), (0, 0, 0)) + both((CHUNK, D_MODEL), (0,)),
        out_specs=both((CHUNK, 3 * D_MODEL), (0,)) + both((CHUNK, LANES), (0,)),
        out_shape=[jax.ShapeDtypeStruct((t, 3 * D_MODEL), F32)] * 2 + [jax.ShapeDtypeStruct((t, LANES), F32)] * 2,
        scratch_shapes=[pltpu.VMEM((BOTH, HEAD_DIM, HEAD_DIM), F32)],
        compiler_params=_params(("arbitrary",)),
    )(*([qkvn] * 6), gb, gb, s_f, s_b, t_f, t_b, do, do)


def _mesh_pos():
    return lax.axis_index("x"), lax.axis_index("y"), lax.axis_index("c")


def _exchange_sems(na):
    return [pltpu.SemaphoreType.DMA((na, N_DEV - 1)), pltpu.SemaphoreType.DMA((na, N_DEV - 1)),
            pltpu.SemaphoreType.DMA((na,))]


def _gather_plan(x_refs, out_refs, send_sems, recv_sems, local_sems):
    na = len(x_refs)
    x, y, c = _mesh_pos()
    me, sibling = (x, y, c), (x, y, 1 - c)
    chips = [(1 - x, y), (x, 1 - y), (1 - x, 1 - y)]

    def block(a, px, py, pc):
        return out_refs[a].at[4 * px + 2 * py + pc]

    def copy(a, k, blk, to, src=None):
        return pltpu.make_async_remote_copy(
            src_ref=block(a, *blk) if src is None else src, dst_ref=block(a, *blk),
            send_sem=send_sems.at[a, k], recv_sem=recv_sems.at[a, k],
            device_id=to, device_id_type=pl.DeviceIdType.MESH)

    def mine(a):
        return pltpu.make_async_copy(x_refs[a], block(a, *me), local_sems.at[a])

    def first(a):
        return [copy(a, 0, me, sibling, src=x_refs[a])] + [
            copy(a, 1 + j, me, (*chip, c), src=x_refs[a]) for j, chip in enumerate(chips)]

    def passed(a, j):
        return copy(a, 4 + j, (*chips[j], c), sibling)

    def start():
        for a in range(na):
            mine(a).start()
            for cp in first(a):
                cp.start()

    def forward():
        for j, chip in enumerate(chips):
            for a in range(na):
                copy(a, 1 + j, (*chip, c), me).wait_recv()
                passed(a, j).start()

    def finish():
        for a in range(na):
            copy(a, 0, sibling, me).wait_recv()
            for j, chip in enumerate(chips):
                copy(a, 4 + j, (*chip, 1 - c), me).wait_recv()
        for a in range(na):
            for cp in first(a) + [passed(a, j) for j in range(len(chips))]:
                cp.wait_send()
            mine(a).wait()

    return start, forward, finish


def _scatter_plan(g_refs, land_refs, send_sems, recv_sems, local_sems):
    na = len(g_refs)
    x, y, c = _mesh_pos()
    mine = 4 * x + 2 * y + c

    def local(a):
        return pltpu.make_async_copy(g_refs[a].at[mine], land_refs[a].at[mine], local_sems.at[a])

    def peers():
        for k in range(1, N_DEV):
            px = 1 - x if k & 4 else x
            py = 1 - y if k & 2 else y
            pc = 1 - c if k & 1 else c
            yield k, (px, py, pc), 4 * px + 2 * py + pc

    def send(a, k, to, peer):
        return pltpu.make_async_remote_copy(
            src_ref=g_refs[a].at[peer], dst_ref=land_refs[a].at[mine],
            send_sem=send_sems.at[a, k - 1], recv_sem=recv_sems.at[a, k - 1],
            device_id=to, device_id_type=pl.DeviceIdType.MESH)

    def recv(a, k, peer):
        return pltpu.make_async_remote_copy(
            src_ref=g_refs[a].at[mine], dst_ref=land_refs[a].at[peer],
            send_sem=send_sems.at[a, k - 1], recv_sem=recv_sems.at[a, k - 1],
            device_id=(x, y, c), device_id_type=pl.DeviceIdType.MESH)

    def start():
        for a in range(na):
            local(a).start()
        for k, to, peer in peers():
            for a in range(na):
                send(a, k, to, peer).start()

    def finish():
        for k, to, peer in peers():
            for a in range(na):
                recv(a, k, peer).wait_recv()
        for k, to, peer in peers():
            for a in range(na):
                send(a, k, to, peer).wait_send()
        for a in range(na):
            local(a).wait()

    return start, (lambda: None), finish


def _exchange_call(name, plan, arrays, out_shapes):
    na = len(arrays)

    def body(*refs):
        start, forward, finish = plan(refs[:na], refs[na:2 * na], *refs[2 * na:])
        start()
        forward()
        finish()

    any_spec = pl.BlockSpec(memory_space=pl.ANY)
    return pl.pallas_call(body, name=name, out_shape=list(out_shapes), in_specs=[any_spec] * na,
                          out_specs=[any_spec] * na, scratch_shapes=_exchange_sems(na))(*arrays)


def _gathered_shapes(shards):
    return [jax.ShapeDtypeStruct((N_DEV,) + s.shape, s.dtype) for s in shards]


def _landed_shapes(blocks):
    return [jax.ShapeDtypeStruct(b.shape, b.dtype) for b in blocks]


def _sum_slots(land, name):
    _, rows, cols = land.shape
    tr = _row_tile(rows, cols * 4 * N_DEV, budget=4 << 20)

    def body(*refs):
        acc = refs[0][0].astype(F32)
        for ref in refs[1:N_DEV]:
            acc = acc + ref[0].astype(F32)
        refs[N_DEV][...] = acc

    return pl.pallas_call(
        body, name=name, grid=(rows // tr,),
        in_specs=[pl.BlockSpec((1, tr, cols), lambda i, s=s: (s, i, 0)) for s in range(N_DEV)],
        out_specs=pl.BlockSpec((tr, cols), lambda i: (i, 0)),
        out_shape=jax.ShapeDtypeStruct((rows, cols), F32),
        compiler_params=_params(("parallel",)),
    )(*([land] * N_DEV))


def _row_tile(rows, row_bytes, budget=1 << 20):
    if rows * row_bytes <= budget or rows % SUBLANES:
        return rows
    best = SUBLANES
    for tr in range(SUBLANES, rows + 1, SUBLANES):
        if rows % tr == 0 and tr * row_bytes <= budget:
            best = tr
    return best


def _adamw(w, g, m, v, name):
    shape = w.shape
    cols = shape[-1]
    rows = w.size // cols
    tr = _row_tile(rows, cols * 4)
    b1, b2 = ADAM["b1"], ADAM["b2"]

    def body(w_ref, g_ref, m_ref, v_ref, d_ref, nm_ref, nv_ref):
        gv = g_ref[...]
        nm = b1 * m_ref[...] + (1.0 - b1) * gv
        nv = b2 * v_ref[...] + (1.0 - b2) * jnp.square(gv)
        m_hat = nm / (1.0 - b1 ** ADAM["step"])
        v_hat = nv / (1.0 - b2 ** ADAM["step"])
        d_ref[...] = -ADAM["lr"] * (m_hat / (jnp.sqrt(v_hat) + ADAM["eps"]) + ADAM["wd"] * w_ref[...])
        nm_ref[...] = nm
        nv_ref[...] = nv

    spec = pl.BlockSpec((tr, cols), lambda i: (i, 0))
    outs = pl.pallas_call(
        body, name=name, grid=(rows // tr,), in_specs=[spec] * 4, out_specs=[spec] * 3,
        out_shape=[jax.ShapeDtypeStruct((rows, cols), F32)] * 3, compiler_params=_params(("parallel",)),
    )(*[a.reshape(rows, cols) for a in (w, g, m, v)])
    return [o.reshape(shape) for o in outs]


MATRICES = ("w_in", "w_o_gdn", "w_o_sc", "w_out", "w_up", "w_down")
CONVS = ("conv_qkv", "conv_sc")
SHARDED = ("w_in", "conv_qkv", "w_o_gdn", "conv_sc", "w_o_sc", "w_out", "w_up", "w_down")
SMALL = ("a_log", "dt_bias", "gdn_norm_w", "ln1_g", "ln1_b", "b_up", "b_down", "ln2_g", "ln2_b")
COLUMN_SHARDED = ("w_in", "conv_qkv", "conv_sc", "w_up")
PACK_COLS = 1024


def _pack(parts, row_multiple):
    flat = jnp.concatenate(parts, axis=-1)
    unit = PACK_COLS * row_multiple
    pad = -flat.shape[-1] % unit
    flat = jnp.pad(flat, [(0, 0)] * (flat.ndim - 1) + [(0, pad)])
    return flat.reshape(flat.shape[:-1] + (flat.shape[-1] // PACK_COLS, PACK_COLS))


def _unshard(name, blocks):
    _, l, r, c = blocks.shape
    if name in COLUMN_SHARDED:
        return blocks.transpose(1, 2, 0, 3).reshape(l, r, N_DEV * c)
    return blocks.transpose(1, 0, 2, 3).reshape(l, N_DEV * r, c)


def _to_shards(name, full):
    l, r, c = full.shape
    if name in COLUMN_SHARDED:
        return full.reshape(l, r, N_DEV, c // N_DEV).transpose(2, 0, 1, 3).reshape(N_DEV, -1)
    return full.reshape(l, N_DEV, r // N_DEV, c).transpose(1, 0, 2, 3).reshape(N_DEV, -1)


def _pack_weights(shards):
    parts, layout = [], []
    for name in MATRICES[1:]:
        parts.append(shards[name].astype(BF16).reshape(-1))
        layout.append((name, shards[name].shape, 1))
    for name in CONVS:
        parts.append(jnp.stack(_split3(shards[name])).reshape(-1))
        layout.append((name, shards[name].shape, 3))
    w_in = shards["w_in"]
    return [w_in.astype(BF16).reshape(-1, w_in.shape[-1]), _pack(parts, 16)], (layout, w_in.shape)


def _unpack_weights(gathered, meta):
    (w_in_all, rest_all), (layout, w_in_shape) = gathered, meta
    full = {"w_in": _unshard("w_in", w_in_all.reshape(N_DEV, *w_in_shape))}
    rest_all, off = rest_all.reshape(N_DEV, -1), 0
    for name, shape, pieces in layout:
        size = pieces * shape[0] * shape[1] * shape[2]
        blk = rest_all[:, off:off + size]
        off += size
        if pieces == 3:
            blk = jnp.sum(blk.reshape(N_DEV, 3, *shape).astype(F32), axis=1)
        full[name] = _unshard(name, blk.reshape(N_DEV, *shape))
    return full


def _unpack(flat, names, shapes):
    out, off = {}, 0
    for name in names:
        size = 1
        for s in shapes[name]:
            size *= s
        out[name] = flat[off:off + size].reshape(shapes[name])
        off += size
    return out


def _pack_layer_grads(grads):
    r, c = grads["w_in"].shape
    w_in = grads["w_in"].astype(BF16).reshape(r, N_DEV, c // N_DEV).transpose(1, 0, 2)
    rest = _pack([_to_shards(name, grads[name].astype(BF16)[None]) for name in SHARDED[1:]], LANES)
    return [w_in, rest]


def _pack_small_grads(small_grads):
    small = jnp.concatenate([small_grads[name].reshape(-1) for name in SMALL])
    return _pack([jnp.broadcast_to(small[None, :], (N_DEV, small.shape[0]))], SUBLANES)


def _sum_layer_grads(landed, layer_shapes):
    out = {"w_in": _sum_slots(landed[0], "grads_sum_w_in")}
    shapes = {name: (1,) + layer_shapes[name] for name in SHARDED[1:]}
    rest = _unpack(_sum_slots(landed[1], "grads_sum_rest").reshape(-1), SHARDED[1:], shapes)
    out.update({name: val[0] for name, val in rest.items()})
    return out


def _lane_row(values):
    flat = values.reshape(1, -1)
    return jnp.pad(flat, ((0, 0), (0, LANES - flat.shape[1])))


def _forward_layer(x, x16, w, t, next_arrays):
    if next_arrays is None:
        proj, gathered = _mm(x16, w["w_main"], "nn", "proj_main"), None
    else:
        proj, gathered = _mm(x16, w["w_main"], "nn", "proj_main_gather",
                             exchange=(_gather_plan, next_arrays, _gathered_shapes(next_arrays)))
    proj_ab = _mm(x16, w["w_ab"], "nn", "proj_ab")
    conv_out, qkvn = _qkv_conv_fwd(proj, w["conv_qkv"], t)
    gb = _gating_fwd(proj_ab, w["a_log"], w["dt_bias"], t)
    o_f, o_b, s_f, s_b, t_f, t_b = _gdn_fwd(qkvn, gb, t)
    og = _gate_norm_fwd(o_f, o_b, proj, w["gdn_norm_w"], t)
    s = _sc_fwd(proj, w["conv_sc"], t)
    y_a = _mm(og, w["w_o_gdn"], "nn", "y_gdn")
    y_b = _mm(s, w["w_o_sc"], "nn", "y_sc")
    mixed = _mix_fwd(proj, y_a, y_b, t)
    assert D_MODEL <= MM_TILE
    r1, x1, x1_16 = _mm(mixed, w["w_out"], "nn", "out_proj", tm=LN_MM_ROWS,
                        epilogue=_ln1_epilogue(x, w["ln1_g"], w["ln1_b"]))
    hpre, h = _mm(x1_16, w["w_up"], "nn", "mlp_up", epilogue=_relu2_epilogue(w["b_up"]))
    r2, x2, x2_16 = _mm(h, w["w_down"], "nn", "mlp_down",
                        epilogue=_ln2_epilogue(x1, w["b_down"], w["ln2_g"], w["ln2_b"]))
    saved = dict(x=x, x16=x16, proj=proj, proj_ab=proj_ab, conv_out=conv_out, qkvn=qkvn, gb=gb, o_f=o_f, o_b=o_b,
                 s_f=s_f, s_b=s_b, t_f=t_f, t_b=t_b, og=og, s=s, y_a=y_a, y_b=y_b, mixed=mixed, r1=r1, x1=x1,
                 x1_16=x1_16, hpre=hpre, h=h, r2=r2)
    return x2, x2_16, saved, gathered


def _backward_layer(dx2, w, a, t, pending):
    (dx1_a, dr2), (db_down, dg2, db2) = _ln2_bwd(a["x1"], a["r2"], w["b_down"], w["ln2_g"], w["ln2_b"], dx2, t)
    relu2_back = _relu2_back_epilogue(a["hpre"], w["b_up"])
    if pending is None:
        dhpre, db_up_rows = _mm(dr2, w["w_down"], "nt", "d_h", epilogue=relu2_back)
        dx1 = _mm(dhpre, w["w_up"], "nt", "d_x1", addends=[(1.0, dx1_a)])
        landed = None
    else:
        (dhpre, db_up_rows), landed_rest = _mm(
            dr2, w["w_down"], "nt", "d_h_scatter", epilogue=relu2_back,
            exchange=(_scatter_plan, pending[1:], _landed_shapes(pending[1:])))
        dx1, landed_w_in = _mm(dhpre, w["w_up"], "nt", "d_x1_scatter", addends=[(1.0, dx1_a)],
                               exchange=(_scatter_plan, pending[:1], _landed_shapes(pending[:1])))
        landed = [landed_w_in[0], landed_rest[0]]
    db_up = jnp.sum(db_up_rows, axis=0, keepdims=True)
    dw_down = _mm(a["h"], dr2, "tn", "dw_down")
    dw_up = _mm(a["x1_16"], dhpre, "tn", "dw_up")
    (dx_a, dr1), (dg1, db1) = _ln1_bwd(a["x"], a["r1"], w["ln1_g"], w["ln1_b"], dx1, t)
    dmixed = _mm(dr1, w["w_out"], "nt", "d_mixed", out_dtype=BF16)
    dw_out = _mm(a["mixed"], dr1, "tn", "dw_out")
    dgates, dy_a, dy_b = _mix_bwd(a["proj"], a["y_a"], a["y_b"], dmixed, t)
    dog = _mm(dy_a, w["w_o_gdn"], "nt", "d_og", out_dtype=BF16)
    dw_o_gdn = _mm(a["og"], dy_a, "tn", "dw_o_gdn")
    ds = _mm(dy_b, w["w_o_sc"], "nt", "d_s")
    dw_o_sc = _mm(a["s"], dy_b, "tn", "dw_o_sc")
    dsc, dconv_sc = _sc_bwd(a["proj"], w["conv_sc"], ds, t)
    do, dz, dnorm_w = _gate_norm_bwd(a["o_f"], a["o_b"], a["proj"], w["gdn_norm_w"], dog, t)
    dq_f, dq_b, dgb_f, dgb_b = _gdn_bwd(a["qkvn"], a["gb"], a["s_f"], a["s_b"], a["t_f"], a["t_b"], do, t)
    dqkv, dconv_qkv = _qkv_conv_bwd(a["proj"], w["conv_qkv"], a["conv_out"], dq_f, dq_b, t)
    dab, da_log, ddt_bias = _gating_bwd(a["proj_ab"], w["a_log"], w["dt_bias"], dgb_f, dgb_b, t)

    pieces = [dqkv, dz, dsc, dgates]
    dx = _mm(dab, w["w_ab"], "nt", "dx_ab", addends=[(1.0, dx_a)])
    dx = _mm_nt_pieces(pieces, w["w_main"], "dx_main", dx)
    dw_main = jnp.concatenate([_mm(a["x16"], piece, "tn", f"dw_in_{j}") for j, piece in enumerate(pieces)], axis=1)
    dw_ab = _mm(a["x16"], dab, "tn", "dw_ab")
    dw_in = jnp.concatenate([dw_main[:, :QKVZ_COLS], dw_ab[:, :AB_COLS], dw_main[:, QKVZ_COLS:]], axis=1)
    grads = dict(w_in=dw_in, conv_qkv=dconv_qkv[:3], a_log=da_log[0, :2 * N_HEADS].reshape(2, N_HEADS),
                 dt_bias=ddt_bias[0, :2 * N_HEADS].reshape(2, N_HEADS), gdn_norm_w=dnorm_w[0], w_o_gdn=dw_o_gdn,
                 conv_sc=dconv_sc[:3], w_o_sc=dw_o_sc, w_out=dw_out, ln1_g=dg1[0], ln1_b=db1[0], w_up=dw_up,
                 b_up=db_up[0], w_down=dw_down, b_down=db_down[0], ln2_g=dg2[0], ln2_b=db2[0])
    return dx, grads, landed


def kernel(x, w_in, conv_qkv, a_log, dt_bias, gdn_norm_w, w_o_gdn, conv_sc, w_o_sc, w_out, ln1_g, ln1_b, w_up, b_up, w_down, b_down, ln2_g, ln2_b, loss_target, m_w_in, m_conv_qkv, m_a_log, m_dt_bias, m_gdn_norm_w, m_w_o_gdn, m_conv_sc, m_w_o_sc, m_w_out, m_ln1_g, m_ln1_b, m_w_up, m_b_up, m_w_down, m_b_down, m_ln2_g, m_ln2_b, v_w_in, v_conv_qkv, v_a_log, v_dt_bias, v_gdn_norm_w, v_w_o_gdn, v_conv_sc, v_w_o_sc, v_w_out, v_ln1_g, v_ln1_b, v_w_up, v_b_up, v_w_down, v_b_down, v_ln2_g, v_ln2_b):
    weights = dict(w_in=w_in, conv_qkv=conv_qkv, a_log=a_log, dt_bias=dt_bias, gdn_norm_w=gdn_norm_w,
                   w_o_gdn=w_o_gdn, conv_sc=conv_sc, w_o_sc=w_o_sc, w_out=w_out, ln1_g=ln1_g, ln1_b=ln1_b,
                   w_up=w_up, b_up=b_up, w_down=w_down, b_down=b_down, ln2_g=ln2_g, ln2_b=ln2_b)
    m_in = dict(w_in=m_w_in, conv_qkv=m_conv_qkv, a_log=m_a_log, dt_bias=m_dt_bias, gdn_norm_w=m_gdn_norm_w,
                w_o_gdn=m_w_o_gdn, conv_sc=m_conv_sc, w_o_sc=m_w_o_sc, w_out=m_w_out, ln1_g=m_ln1_g, ln1_b=m_ln1_b,
                w_up=m_w_up, b_up=m_b_up, w_down=m_w_down, b_down=m_b_down, ln2_g=m_ln2_g, ln2_b=m_ln2_b)
    v_in = dict(w_in=v_w_in, conv_qkv=v_conv_qkv, a_log=v_a_log, dt_bias=v_dt_bias, gdn_norm_w=v_gdn_norm_w,
                w_o_gdn=v_w_o_gdn, conv_sc=v_conv_sc, w_o_sc=v_w_o_sc, w_out=v_w_out, ln1_g=v_ln1_g, ln1_b=v_ln1_b,
                w_up=v_w_up, b_up=v_b_up, w_down=v_w_down, b_down=v_b_down, ln2_g=v_ln2_g, ln2_b=v_ln2_b)
    t = x.shape[1]
    depth = w_in.shape[0]

    def layer_weights(full, l):
        w_in_l = full["w_in"][0]
        return dict(
            w_main=jnp.concatenate([w_in_l[:, :QKVZ_COLS], w_in_l[:, QKVZ_COLS + AB_COLS:]], axis=1),
            w_ab=jnp.pad(w_in_l[:, QKVZ_COLS:QKVZ_COLS + AB_COLS], ((0, 0), (0, LANES - AB_COLS))),
            conv_qkv=jnp.pad(full["conv_qkv"][0], ((0, SUBLANES - 3), (0, 0))),
            conv_sc=jnp.pad(full["conv_sc"][0], ((0, SUBLANES - 3), (0, 0))),
            a_log=_lane_row(a_log[l]), dt_bias=_lane_row(dt_bias[l]), gdn_norm_w=gdn_norm_w[l][None, :],
            w_o_gdn=full["w_o_gdn"][0], w_o_sc=full["w_o_sc"][0], w_out=full["w_out"][0],
            ln1_g=ln1_g[l][None, :], ln1_b=ln1_b[l][None, :], w_up=full["w_up"][0], b_up=b_up[l][None, :],
            w_down=full["w_down"][0], b_down=b_down[l][None, :], ln2_g=ln2_g[l][None, :], ln2_b=ln2_b[l][None, :])

    packed = [_pack_weights({name: weights[name][l:l + 1] for name in MATRICES + CONVS}) for l in range(depth)]
    gathered = _exchange_call("weights_all_gather", _gather_plan, packed[0][0], _gathered_shapes(packed[0][0]))
    h = x.reshape(t, D_MODEL)
    h16 = h.astype(BF16)
    layers, saved = [], []
    for l in range(depth):
        layers.append(layer_weights(_unpack_weights(gathered, packed[l][1]), l))
        next_arrays = packed[l + 1][0] if l + 1 < depth else None
        h, h16, acts, gathered = _forward_layer(h, h16, layers[l], t, next_arrays)
        saved.append(acts)
    dh, loss_local = _loss_stage(h, loss_target.reshape(t, D_MODEL), t)
    loss = lax.psum(loss_local, MESH_AXES)

    layer_shapes = {name: weights[name].shape[1:] for name in SHARDED}
    layer_grads, reduced, pending = [None] * depth, [None] * depth, None
    for l in reversed(range(depth)):
        dh, layer_grads[l], landed = _backward_layer(dh, layers[l], saved[l], t, pending)
        if landed is not None:
            reduced[l + 1] = _sum_layer_grads(landed, layer_shapes)
        pending = _pack_layer_grads(layer_grads[l])
    small = _pack_small_grads({name: jnp.stack([g[name] for g in layer_grads]) for name in SMALL})
    last = pending + [small]
    landed = _exchange_call("grads_scatter", _scatter_plan, last, _landed_shapes(last))
    reduced[0] = _sum_layer_grads(landed[:2], layer_shapes)
    grads = {name: jnp.stack([r[name] for r in reduced]) for name in SHARDED}
    small_shapes = {name: weights[name].shape for name in SMALL}
    grads.update(_unpack(_sum_slots(landed[2], "grads_sum_small").reshape(-1), SMALL, small_shapes))

    names = list(weights)
    updates = {n: _adamw(weights[n], grads[n], m_in[n], v_in[n], f"adamw_{n}") for n in names}
    return (loss, dh.reshape(x.shape), *[grads[n] for n in names], *[updates[n][0] for n in names],
            *[updates[n][1] for n in names], *[updates[n][2] for n in names])
```

```python
import functools

import jax
import jax.numpy as jnp
from jax import lax
from jax.experimental import pallas as pl
from jax.experimental.pallas import tpu as pltpu

F32 = jnp.float32
BF16 = jnp.bfloat16

D_MODEL = 1024
N_HEADS = 8
HEAD_DIM = 128
CHUNK = 64
D_FF = 4 * D_MODEL
DEPTH = 4
N_DEV = 8
LN_EPS = 1e-5
RMS_EPS = 1e-6
L2_EPS = 1e-6
ALPHA = (2 * DEPTH) ** 0.25
MAIN_COLS = 9 * D_MODEL
QKVZ_COLS = 4 * D_MODEL
AB_COLS = 4 * N_HEADS
W_IN_COLS = MAIN_COLS + AB_COLS
LANES = 128
SUBLANES = 8
VMEM_LIMIT = 48 * 1024 * 1024
MM_TILE = 1024
LN_MM_ROWS = 512
ROW_TILE = 256
WIDE_ROW_TILE = 128
ADAM = dict(lr=0.001, b1=0.9, b2=0.999, eps=1e-08, wd=0.01, step=10)
MESH_AXES = ("x", "y", "c")


def _params(sem=None):
    return pltpu.CompilerParams(dimension_semantics=sem, vmem_limit_bytes=VMEM_LIMIT)


_DIMS = {"nn": (1, 0), "nt": (1, 1), "tn": (0, 0)}


def _mm(a, b, mode, name, *, out_dtype=F32, addends=(), epilogue=None, exchange=None, tm=None):
    if mode == "nn":
        (m, k), n = a.shape, b.shape[1]
    elif mode == "nt":
        (m, k), n = a.shape, b.shape[0]
    else:
        (k, m), n = a.shape, b.shape[1]
    tm, tn, tk = min(m, tm or MM_TILE), min(n, MM_TILE), min(k, MM_TILE)
    assert m % tm == 0 and n % tn == 0 and k % tk == 0
    nk = k // tk
    ca, cb = _DIMS[mode]
    scales = tuple(s for s, _ in addends)
    na = len(addends)
    epi_fn, epi_in, epi_out = epilogue or (None, (), (("tile", out_dtype),))
    ne, no = len(epi_in), len(epi_out)
    plan, x_arrays, x_shapes = exchange or (None, (), ())
    nx = len(x_arrays)
    gi, gj = m // tm, n // tn

    def body(a_ref, b_ref, *rest):
        add_refs, epi_refs = rest[:na], rest[na:na + ne]
        xin_refs, rest = rest[na + ne:na + ne + nx], rest[na + ne + nx:]
        o_refs, xout_refs, rest = rest[:no], rest[no:no + nx], rest[no + nx:]
        kk = pl.program_id(2)
        if plan:
            start, forward, finish_exchange = plan(xin_refs, xout_refs, *rest[-3:])
            i, j = pl.program_id(0), pl.program_id(1)

            @pl.when((i == 0) & (j == 0) & (kk == 0))
            def _():
                start()

            @pl.when((i == gi // 2) & (j == gj // 2) & (kk == 0))
            def _():
                forward()

        p = lax.dot_general(a_ref[...].astype(BF16), b_ref[...].astype(BF16), (((ca,), (cb,)), ((), ())),
                            preferred_element_type=F32)

        def finish(r):
            for s, ref in zip(scales, add_refs):
                r = r + s * ref[...].astype(F32)
            vals = epi_fn(r, *[ref[...] for ref in epi_refs]) if epi_fn else (r,)
            for (kind, _), ref, val in zip(epi_out, o_refs, vals):
                if kind == "colsum":
                    val = jnp.where(lax.broadcasted_iota(jnp.int32, ref.shape, 0) == 0, val, 0.0)
                ref[...] = val.astype(ref.dtype)

        if nk == 1:
            finish(p)
        else:
            acc = rest[0]

            @pl.when(kk == 0)
            def _():
                acc[...] = p

            @pl.when(kk > 0)
            def _():
                acc[...] += p

            @pl.when(kk == nk - 1)
            def _():
                finish(acc[...])

        if plan:
            @pl.when((i == gi - 1) & (j == gj - 1) & (kk == nk - 1))
            def _():
                finish_exchange()

    if mode == "nn":
        a_spec = pl.BlockSpec((tm, tk), lambda i, j, kk: (i, kk))
        b_spec = pl.BlockSpec((tk, tn), lambda i, j, kk: (kk, j))
    elif mode == "nt":
        a_spec = pl.BlockSpec((tm, tk), lambda i, j, kk: (i, kk))
        b_spec = pl.BlockSpec((tn, tk), lambda i, j, kk: (j, kk))
    else:
        a_spec = pl.BlockSpec((tk, tm), lambda i, j, kk: (kk, i))
        b_spec = pl.BlockSpec((tk, tn), lambda i, j, kk: (kk, j))
    kinds = {"tile": (pl.BlockSpec((tm, tn), lambda i, j, kk: (i, j)), (m, n)),
             "row": (pl.BlockSpec((1, tn), lambda i, j, kk: (0, j)), (1, n)),
             "colsum": (pl.BlockSpec((SUBLANES, tn), lambda i, j, kk: (i, j)), (SUBLANES * (m // tm), n))}
    o_spec = kinds["tile"][0]
    any_spec = pl.BlockSpec(memory_space=pl.ANY)
    res = pl.pallas_call(
        body, name=name, grid=(gi, gj, nk),
        in_specs=[a_spec, b_spec] + [o_spec] * na + [kinds[kind][0] for _, kind in epi_in] + [any_spec] * nx,
        out_specs=[kinds[kind][0] for kind, _ in epi_out] + [any_spec] * nx,
        out_shape=[jax.ShapeDtypeStruct(kinds[kind][1], dt) for kind, dt in epi_out] + list(x_shapes),
        scratch_shapes=([pltpu.VMEM((tm, tn), F32)] if nk > 1 else []) + (_exchange_sems(nx) if plan else []),
        compiler_params=_params(("arbitrary",) * 3 if plan else ("parallel", "parallel", "arbitrary")),
    )(a, b, *[arr for _, arr in addends], *[arr for arr, _ in epi_in], *x_arrays)
    if plan:
        return (tuple(res[:no]) if epilogue else res[0]), tuple(res[no:])
    return res if epilogue else res[0]


def _mm_nt_pieces(pieces, b, name, addend, exchange):
    m, n = pieces[0].shape[0], b.shape[0]
    tm, tn, tk = min(m, MM_TILE), min(n, MM_TILE), min(b.shape[1], MM_TILE)
    assert m % tm == 0 and n % tn == 0 and all(p.shape[1] % tk == 0 for p in pieces)
    counts = [p.shape[1] // tk for p in pieces]
    starts = [sum(counts[:j]) for j in range(len(pieces))]
    nk, npieces = sum(counts), len(pieces)
    assert nk * tk == b.shape[1]
    plan, x_arrays, x_shapes = exchange
    nx = len(x_arrays)
    gi, gj = m // tm, n // tn

    def body(*refs):
        a_refs, b_ref, add_ref = refs[:npieces], refs[npieces], refs[npieces + 1]
        xin_refs, rest = refs[npieces + 2:npieces + 2 + nx], refs[npieces + 2 + nx:]
        o_ref, xout_refs, acc = rest[0], rest[1:1 + nx], rest[1 + nx]
        start_exchange, forward, finish_exchange = plan(xin_refs, xout_refs, *rest[-3:])
        i, j, kk = pl.program_id(0), pl.program_id(1), pl.program_id(2)

        @pl.when((i == 0) & (j == 0) & (kk == 0))
        def _():
            start_exchange()

        @pl.when((i == gi // 2) & (j == gj // 2) & (kk == 0))
        def _():
            forward()

        for a_ref, start, count in zip(a_refs, starts, counts):
            @pl.when((kk >= start) & (kk < start + count))
            def _(a_ref=a_ref):
                p = lax.dot_general(a_ref[...].astype(BF16), b_ref[...].astype(BF16), (((1,), (1,)), ((), ())),
                                    preferred_element_type=F32)

                @pl.when(kk == 0)
                def _():
                    acc[...] = p

                @pl.when(kk > 0)
                def _():
                    acc[...] += p

        @pl.when(kk == nk - 1)
        def _():
            o_ref[...] = acc[...] + add_ref[...]

        @pl.when((i == gi - 1) & (j == gj - 1) & (kk == nk - 1))
        def _():
            finish_exchange()

    a_specs = [pl.BlockSpec((tm, tk), lambda i, j, kk, s=s, c=c: (i, jnp.clip(kk - s, 0, c - 1)))
               for s, c in zip(starts, counts)]
    o_spec = pl.BlockSpec((tm, tn), lambda i, j, kk: (i, j))
    any_spec = pl.BlockSpec(memory_space=pl.ANY)
    res = pl.pallas_call(
        body, name=name, grid=(gi, gj, nk),
        in_specs=a_specs + [pl.BlockSpec((tn, tk), lambda i, j, kk: (j, kk)), o_spec] + [any_spec] * nx,
        out_specs=[o_spec] + [any_spec] * nx,
        out_shape=[jax.ShapeDtypeStruct((m, n), F32)] + list(x_shapes),
        scratch_shapes=[pltpu.VMEM((tm, tn), F32)] + _exchange_sems(nx),
        compiler_params=_params(("arbitrary",) * 3),
    )(*pieces, b, addend, *x_arrays)
    return res[0], tuple(res[1:])


def _tile_call(name, body, t, tm, tiled, halo, params, outs, accs):
    tm = min(tm, t)
    assert t % tm == 0 and tm % SUBLANES == 0
    steps = t // tm
    hb = tm // SUBLANES
    nt, nh, npar, no = len(tiled), len(halo), len(params), len(outs)

    def kern(*refs):
        i = pl.program_id(0)
        t_refs = refs[:nt + nh]
        h_refs = refs[nt + nh:nt + 3 * nh]
        p_refs = refs[nt + 3 * nh:nt + 3 * nh + npar]
        o_refs = refs[nt + 3 * nh + npar:nt + 3 * nh + npar + no]
        a_refs = refs[nt + 3 * nh + npar + no:]
        tiles = [r[...].astype(F32) for r in t_refs]
        halos = []
        for j in range(nh):
            prev = h_refs[2 * j][SUBLANES - 1:SUBLANES, :].astype(F32)
            nxt = h_refs[2 * j + 1][0:1, :].astype(F32)
            halos.append((jnp.where(i > 0, prev, 0.0), jnp.where(i < steps - 1, nxt, 0.0)))
        o_vals, a_vals = body(tiles, halos, [r[...] for r in p_refs])
        for ref, val in zip(o_refs, o_vals):
            ref[...] = val.astype(ref.dtype)
        for ref, val in zip(a_refs, a_vals):
            @pl.when(i == 0)
            def _(ref=ref, val=val):
                ref[...] = val

            @pl.when(i > 0)
            def _(ref=ref, val=val):
                ref[...] += val

    in_specs, args = [], []
    for arr, nc, cb in list(tiled) + list(halo):
        in_specs.append(pl.BlockSpec((tm, nc), lambda i, cb=cb: (i, cb)))
        args.append(arr)
    last = t // SUBLANES - 1
    for arr, nc, cb in halo:
        in_specs.append(pl.BlockSpec((SUBLANES, nc), lambda i, cb=cb: (jnp.maximum(i * hb - 1, 0), cb)))
        in_specs.append(pl.BlockSpec((SUBLANES, nc), lambda i, cb=cb: (jnp.minimum((i + 1) * hb, last), cb)))
        args += [arr, arr]
    for arr in params:
        in_specs.append(pl.BlockSpec(arr.shape, lambda i: (0, 0)))
        args.append(arr)
    out_specs = [pl.BlockSpec((tm, nc), lambda i: (i, 0)) for nc, _ in outs]
    out_specs += [pl.BlockSpec(shape, lambda i: (0, 0)) for shape in accs]
    out_shape = [jax.ShapeDtypeStruct((t, nc), dt) for nc, dt in outs]
    out_shape += [jax.ShapeDtypeStruct(shape, F32) for shape in accs]
    res = pl.pallas_call(kern, name=name, grid=(steps,), in_specs=in_specs, out_specs=out_specs,
                         out_shape=out_shape, compiler_params=_params(("arbitrary",)))(*args)
    return res[:no], res[no:]


def _row_iota(x):
    return lax.broadcasted_iota(jnp.int32, x.shape, 0)


def _lane_iota(x):
    return lax.broadcasted_iota(jnp.int32, x.shape, 1)


def _shift_down(x, first_row):
    return jnp.where(_row_iota(x) == 0, first_row, pltpu.roll(x, 1, 0))


def _shift_up(x, last_row):
    n = x.shape[0]
    return jnp.where(_row_iota(x) == n - 1, last_row, pltpu.roll(x, n - 1, 0))


def _taps(w):
    return w[0:1, :], w[1:2, :], w[2:3, :]


def _tap_rows(d0, d1, d2, rows=SUBLANES):
    r = lax.broadcasted_iota(jnp.int32, (rows, d0.shape[1]), 0)
    return jnp.where(r == 0, d0, jnp.where(r == 1, d1, jnp.where(r == 2, d2, 0.0)))


def _colsum(x):
    return jnp.sum(x, axis=0, keepdims=True)


def _silu(x):
    return x * jax.nn.sigmoid(x)


def _softplus(x):
    return jnp.maximum(x, 0.0) + jnp.log(1.0 + jnp.exp(-jnp.abs(x)))


def _heads(x):
    return [x[:, h * HEAD_DIM:(h + 1) * HEAD_DIM] for h in range(x.shape[1] // HEAD_DIM)]


def _post_conv(c):
    blocks = _heads(_silu(c))
    out = []
    for j, blk in enumerate(blocks):
        if j < 2 * N_HEADS:
            blk = blk * lax.rsqrt(jnp.sum(blk * blk, axis=-1, keepdims=True) + L2_EPS)
        if j < N_HEADS:
            blk = blk * (HEAD_DIM ** -0.5)
        out.append(blk)
    return jnp.concatenate(out, axis=1)


def _gating(ab, a_log, dt_bias):
    lane = _lane_iota(ab)
    g = -jnp.exp(a_log) * _softplus(ab + dt_bias)
    return jnp.where(lane < 2 * N_HEADS, g, jnp.where(lane < AB_COLS, jax.nn.sigmoid(ab), 0.0))


def _gate_norm(o_f, o_b, z, norm_w):
    out = []
    for oh, zh in zip(_heads(o_f + o_b), _heads(z)):
        out.append(oh * lax.rsqrt(jnp.mean(oh * oh, axis=-1, keepdims=True) + RMS_EPS) * norm_w * _silu(zh))
    return jnp.concatenate(out, axis=1)


def _mix(gate_a, gate_b, y_a, y_b):
    return jax.nn.sigmoid(gate_a) * y_a + jax.nn.sigmoid(gate_b) * y_b


def _layer_norm(u, g, b):
    mu = jnp.mean(u, axis=-1, keepdims=True)
    var = jnp.mean(jnp.square(u - mu), axis=-1, keepdims=True)
    return (u - mu) * lax.rsqrt(var + LN_EPS) * g + b


def _ln1(x, r, g, b):
    return _layer_norm(ALPHA * x + r, g, b)


def _ln2(x, r, bias, g, b):
    return _layer_norm(ALPHA * x + r + bias, g, b)


def _relu2(hpre, bias):
    return jnp.square(jnp.maximum(hpre + bias, 0.0))


def _qkv_conv_fwd(proj, conv_w, t):
    def body(tiles, halos, params):
        (x,), ((xp, xn),), (w,) = tiles, halos, params
        w0, w1, w2 = _taps(w)
        c = w0 * _shift_down(x, xp) + w1 * x + w2 * _shift_up(x, xn)
        return [c, _post_conv(c)], []

    (c, qkvn), _ = _tile_call("qkv_conv_fwd", body, t, WIDE_ROW_TILE, [], [(proj, 3 * D_MODEL, 0)], [conv_w],
                              [(3 * D_MODEL, F32), (3 * D_MODEL, F32)], [])
    return c, qkvn


def _gating_fwd(proj_ab, a_log, dt_bias, t):
    def body(tiles, halos, params):
        return [_gating(tiles[0], params[0], params[1])], []

    (gb,), _ = _tile_call("gating_fwd", body, t, ROW_TILE, [(proj_ab, LANES, 0)], [], [a_log, dt_bias],
                          [(LANES, F32)], [])
    return gb


def _gate_norm_fwd(o_f, o_b, proj, norm_w, t):
    def body(tiles, halos, params):
        return [_gate_norm(tiles[0], tiles[1], tiles[2], params[0])], []

    (og,), _ = _tile_call("gate_norm_fwd", body, t, ROW_TILE,
                          [(o_f, D_MODEL, 0), (o_b, D_MODEL, 0), (proj, D_MODEL, 3)], [], [norm_w],
                          [(D_MODEL, BF16)], [])
    return og


def _sc_fwd(proj, conv_w, t):
    def body(tiles, halos, params):
        (sb,), ((cp, cn), (xp, xn)), (w,) = tiles[:1], halos, params
        sc, sx = tiles[1], tiles[2]
        w0, w1, w2 = _taps(w)
        u = sc * sx
        return [sb * (w0 * _shift_down(u, cp * xp) + w1 * u + w2 * _shift_up(u, cn * xn))], []

    (s,), _ = _tile_call("sc_fwd", body, t, ROW_TILE, [(proj, D_MODEL, 4)],
                         [(proj, D_MODEL, 5), (proj, D_MODEL, 6)], [conv_w], [(D_MODEL, BF16)], [])
    return s


def _mix_fwd(proj, y_a, y_b, t):
    def body(tiles, halos, params):
        return [_mix(*tiles)], []

    (mixed,), _ = _tile_call("mix_fwd", body, t, ROW_TILE,
                             [(proj, D_MODEL, 7), (proj, D_MODEL, 8), (y_a, D_MODEL, 0), (y_b, D_MODEL, 0)], [], [],
                             [(D_MODEL, BF16)], [])
    return mixed


LN_OUTS = [("tile", F32), ("tile", F32), ("tile", BF16)]


def _ln1_epilogue(x, g, b):
    def fn(r, xv, gv, bv):
        y = _ln1(xv, r, gv, bv)
        return r, y, y

    return fn, [(x, "tile"), (g, "row"), (b, "row")], LN_OUTS


def _ln2_epilogue(x, bias, g, b):
    def fn(r, xv, biasv, gv, bv):
        y = _ln2(xv, r, biasv, gv, bv)
        return r, y, y

    return fn, [(x, "tile"), (bias, "row"), (g, "row"), (b, "row")], LN_OUTS


def _relu2_epilogue(bias):
    return (lambda r, b: (r, _relu2(r, b))), [(bias, "row")], [("tile", F32), ("tile", BF16)]


def _relu2_back_epilogue(hpre, bias):
    def fn(r, hp, b):
        _, vjp = jax.vjp(_relu2, hp, b)
        return vjp(r)

    return fn, [(hpre, "tile"), (bias, "row")], [("tile", BF16), ("colsum", F32)]


def _loss_stage(y, target, t):
    def body(tiles, halos, params):
        d = tiles[0] - tiles[1]
        part = 0.5 * jnp.sum(jnp.mean(d * d, axis=-1, keepdims=True), axis=0, keepdims=True)
        return [d * (1.0 / D_MODEL)], [jnp.broadcast_to(part, (1, LANES))]

    (dy,), (loss,) = _tile_call("loss", body, t, ROW_TILE, [(y, D_MODEL, 0), (target, D_MODEL, 0)], [], [],
                                [(D_MODEL, F32)], [(1, LANES)])
    return dy, loss[0, 0]


def _ln2_bwd(x, r, bias, g, b, dy, t):
    def body(tiles, halos, params):
        _, vjp = jax.vjp(_ln2, tiles[0], tiles[1], params[0], params[1], params[2])
        dx, dr, dbias, dg, db = vjp(tiles[2])
        return [dx, dr], [dbias, dg, db]

    return _tile_call("ln2_bwd", body, t, ROW_TILE, [(x, D_MODEL, 0), (r, D_MODEL, 0), (dy, D_MODEL, 0)], [],
                      [bias, g, b], [(D_MODEL, F32), (D_MODEL, BF16)], [(1, D_MODEL)] * 3)


def _ln1_bwd(x, r, g, b, dy, t):
    def body(tiles, halos, params):
        _, vjp = jax.vjp(_ln1, tiles[0], tiles[1], params[0], params[1])
        dx, dr, dg, db = vjp(tiles[2])
        return [dx, dr], [dg, db]

    return _tile_call("ln1_bwd", body, t, ROW_TILE, [(x, D_MODEL, 0), (r, D_MODEL, 0), (dy, D_MODEL, 0)], [],
                      [g, b], [(D_MODEL, F32), (D_MODEL, BF16)], [(1, D_MODEL)] * 2)


def _mix_bwd(proj, y_a, y_b, dmixed, t):
    def body(tiles, halos, params):
        _, vjp = jax.vjp(_mix, *tiles[:4])
        dga, dgb, dya, dyb = vjp(tiles[4])
        return [jnp.concatenate([dga, dgb], axis=1), dya, dyb], []

    (dgates, dya, dyb), _ = _tile_call(
        "mix_bwd", body, t, ROW_TILE,
        [(proj, D_MODEL, 7), (proj, D_MODEL, 8), (y_a, D_MODEL, 0), (y_b, D_MODEL, 0), (dmixed, D_MODEL, 0)], [], [],
        [(2 * D_MODEL, BF16), (D_MODEL, BF16), (D_MODEL, BF16)], [])
    return dgates, dya, dyb


def _sc_bwd(proj, conv_w, ds, t):
    def body(tiles, halos, params):
        ds_, sb, sc, sx = tiles
        (dsp, dsn), (sbp, sbn), (scp, scn), (sxp, sxn) = halos
        w0, w1, w2 = _taps(params[0])
        u = sc * sx
        u_prev, u_next = _shift_down(u, scp * sxp), _shift_up(u, scn * sxn)
        dconv = ds_ * sb
        du = w0 * _shift_up(dconv, dsn * sbn) + w1 * dconv + w2 * _shift_down(dconv, dsp * sbp)
        dsb = ds_ * (w0 * u_prev + w1 * u + w2 * u_next)
        dw = _tap_rows(_colsum(dconv * u_prev), _colsum(dconv * u), _colsum(dconv * u_next))
        return [jnp.concatenate([dsb, du * sx, du * sc], axis=1)], [dw]

    (dsc,), (dw,) = _tile_call("sc_bwd", body, t, ROW_TILE, [],
                               [(ds, D_MODEL, 0), (proj, D_MODEL, 4), (proj, D_MODEL, 5), (proj, D_MODEL, 6)],
                               [conv_w], [(3 * D_MODEL, BF16)], [(SUBLANES, D_MODEL)])
    return dsc, dw


def _gate_norm_bwd(o_f, o_b, proj, norm_w, dog, t):
    def body(tiles, halos, params):
        _, vjp = jax.vjp(_gate_norm, tiles[0], tiles[1], tiles[2], params[0])
        do, _, dz, dnw = vjp(tiles[3])
        return [do, dz], [dnw]

    (do, dz), (dnw,) = _tile_call(
        "gate_norm_bwd", body, t, ROW_TILE,
        [(o_f, D_MODEL, 0), (o_b, D_MODEL, 0), (proj, D_MODEL, 3), (dog, D_MODEL, 0)], [], [norm_w],
        [(D_MODEL, F32), (D_MODEL, BF16)], [(1, HEAD_DIM)])
    return do, dz, dnw


def _qkv_conv_bwd(proj, conv_w, c, dq_f, dq_b, t):
    def post_conv_back(cv, ct):
        _, vjp = jax.vjp(_post_conv, cv)
        return vjp(ct)[0]

    def body(tiles, halos, params):
        cv, df, db, x = tiles
        (cp, cn), (dfp, dfn), (dbp, dbn), (xp, xn) = halos
        w0, w1, w2 = _taps(params[0])
        d = post_conv_back(cv, df + db)
        d_prev, d_next = post_conv_back(cp, dfp + dbp), post_conv_back(cn, dfn + dbn)
        dx = w0 * _shift_up(d, d_next) + w1 * d + w2 * _shift_down(d, d_prev)
        dw = _tap_rows(_colsum(d * _shift_down(x, xp)), _colsum(d * x), _colsum(d * _shift_up(x, xn)))
        return [dx], [dw]

    wide = 3 * D_MODEL
    (dqkv,), (dw,) = _tile_call("qkv_conv_bwd", body, t, WIDE_ROW_TILE, [],
                                [(c, wide, 0), (dq_f, wide, 0), (dq_b, wide, 0), (proj, wide, 0)], [conv_w],
                                [(wide, BF16)], [(SUBLANES, wide)])
    return dqkv, dw


def _gating_bwd(proj_ab, a_log, dt_bias, dgb_f, dgb_b, t):
    def body(tiles, halos, params):
        _, vjp = jax.vjp(_gating, tiles[0], params[0], params[1])
        dab, dal, ddt = vjp(tiles[1] + tiles[2])
        return [dab], [dal, ddt]

    (dab,), (dal, ddt) = _tile_call("gating_bwd", body, t, ROW_TILE,
                                    [(proj_ab, LANES, 0), (dgb_f, LANES, 0), (dgb_b, LANES, 0)], [],
                                    [a_log, dt_bias], [(LANES, BF16)], [(1, LANES)] * 2)
    return dab, dal, ddt


@functools.partial(jax.custom_vjp, nondiff_argnums=(2, 3))
def _dot(a, b, ca, cb):
    return lax.dot_general(a.astype(BF16), b.astype(BF16), (((ca,), (cb,)), ((), ())), preferred_element_type=F32)


def _dot_fwd(a, b, ca, cb):
    return _dot(a, b, ca, cb), (a, b)


def _dot_bwd(ca, cb, res, ct):
    a, b = res
    fa, fb = 1 - ca, 1 - cb
    da = _dot(ct, b, 1, fb) if ca == 1 else _dot(b, ct, fb, 1)
    db = _dot(a, ct, fa, 0) if cb == 0 else _dot(ct, a, 0, fa)
    return da, db


_dot.defvjp(_dot_fwd, _dot_bwd)


def _split3(x):
    hi = x.astype(BF16)
    r1 = x - hi.astype(F32)
    mid = r1.astype(BF16)
    return hi, mid, (r1 - mid.astype(F32)).astype(BF16)


def _dot_exact(a, b, ca, cb, exact):
    dims = (((ca,), (cb,)), ((), ()))
    if exact == 0:
        return sum(lax.dot_general(a.astype(BF16), p, dims, preferred_element_type=F32) for p in _split3(b))
    return sum(lax.dot_general(p, b.astype(BF16), dims, preferred_element_type=F32) for p in _split3(a))


def _tri_masks(n, rev):
    r = lax.broadcasted_iota(jnp.int32, (n, n), 0)
    c = lax.broadcasted_iota(jnp.int32, (n, n), 1)
    return ((c >= r), (c > r)) if rev else ((c <= r), (c < r))


@functools.partial(jax.custom_vjp, nondiff_argnums=(1,))
def _cumsum_rows(g, rev):
    incl, _ = _tri_masks(g.shape[0], rev)
    return _dot_exact(incl.astype(F32), g, 1, 0, 0)


_cumsum_rows.defvjp(lambda g, rev: (_cumsum_rows(g, rev), None),
                    lambda rev, _, ct: (_cumsum_rows(ct, not rev),))


def _eye(n):
    return (lax.broadcasted_iota(jnp.int32, (n, n), 0) == lax.broadcasted_iota(jnp.int32, (n, n), 1)).astype(F32)


@jax.custom_vjp
def _to_rows(x):
    return _dot_exact(_eye(x.shape[1]), x, 1, 1, 0)


@jax.custom_vjp
def _to_cols(y):
    return _dot_exact(y, _eye(y.shape[0]), 0, 0, 1)


_to_rows.defvjp(lambda x: (_to_rows(x), None), lambda _, ct: (_to_cols(ct),))
_to_cols.defvjp(lambda y: (_to_cols(y), None), lambda _, ct: (_to_rows(ct),))


def _pick_col(arr, idx):
    return jnp.sum(jnp.where(_lane_iota(arr) == idx, arr, 0.0), axis=1, keepdims=True)


def _pick_row(arr, idx):
    return jnp.sum(jnp.where(_row_iota(arr) == idx, arr, 0.0), axis=0, keepdims=True)


def _chunk_gates(gb, direction, rev):
    n = gb.shape[0]
    incl, strict = _tri_masks(n, rev)
    gc = _cumsum_rows(gb, rev)
    gc_rows = _to_rows(gc)
    lanes = [direction * N_HEADS + h for h in range(N_HEADS)]
    cols = [_pick_col(gc, ln) for ln in lanes]
    rows = [_pick_row(gc_rows, ln) for ln in lanes]
    betas = [_pick_col(gb, 2 * N_HEADS + ln) for ln in lanes]
    decays = [jnp.where(incl, jnp.exp(jnp.where(incl, c - r, 0.0)), 0.0) for c, r in zip(cols, rows)]
    return cols, betas, decays, strict


def _chunk_gates_both(gb_f, gb_b):
    gates = dict(cols=[], betas=[], decays=[], lasts=[])
    for direction, gb in enumerate((gb_f, gb_b)):
        rev = direction == 1
        cols, betas, decays, _ = _chunk_gates(gb, direction, rev)
        last_idx = 0 if rev else gb.shape[0] - 1
        gates["cols"] += cols
        gates["betas"] += betas
        gates["decays"] += decays
        gates["lasts"] += [_pick_row(c, last_idx) for c in cols]
    return gates


def _chunk_lmat(k_f, k_b, gates):
    n = k_f.shape[0]
    stricts = [_tri_masks(n, False)[1]] * N_HEADS + [_tri_masks(n, True)[1]] * N_HEADS
    ks = _heads(k_f) + _heads(k_b)
    kk = [_dot(kh * b, kh, 1, 1) for kh, b in zip(ks, gates["betas"])]
    return tuple(jnp.where(s, x * d, 0.0) for s, x, d in zip(stricts, kk, gates["decays"]))


def _tri_inverse(lmats):
    n = lmats[0].shape[0]
    eye = _eye(n)
    powers = [-lm for lm in lmats]
    invs = [eye + p for p in powers]
    span = 2
    while span < n:
        powers = [_dot(p, p, 1, 0) for p in powers]
        steps = [_dot(p, i, 1, 0) for p, i in zip(powers, invs)]
        invs = [i + s for i, s in zip(invs, steps)]
        span *= 2
    return tuple(invs)


def _chunk_out(qkv_f, qkv_b, gates, tmats, states):
    qs, ks, vs = (_heads(qkv_f[p]) + _heads(qkv_b[p]) for p in range(3))
    cols, betas, decays, lasts = (gates[key] for key in ("cols", "betas", "decays", "lasts"))
    n = ks[0].shape[0]
    vk = [jnp.concatenate([vh * b, kh * b * jnp.exp(c)], axis=1) for vh, kh, b, c in zip(vs, ks, betas, cols)]
    uw = [_dot(tm, x, 1, 0) for tm, x in zip(tmats, vk)]
    attns = [_dot(qh, kh, 1, 1) * d for qh, kh, d in zip(qs, ks, decays)]
    wq = [jnp.concatenate([x[:, HEAD_DIM:], qh * jnp.exp(c)], axis=0) for x, qh, c in zip(uw, qs, cols)]
    wqs = [_dot(x, st, 1, 0) for x, st in zip(wq, states)]
    v_news = [x[:, :HEAD_DIM] - y[:n] for x, y in zip(uw, wqs)]
    inter = [y[n:] for y in wqs]
    intra = [_dot(a, vn, 1, 0) for a, vn in zip(attns, v_news)]
    adds = [_dot(kh * jnp.exp(l - c), vn, 0, 0) for kh, l, c, vn in zip(ks, lasts, cols, v_news)]
    new_states = tuple(st * jnp.exp(l) + a for st, l, a in zip(states, lasts, adds))
    outs = [x + y for x, y in zip(inter, intra)]
    return jnp.concatenate(outs[:N_HEADS], axis=1), jnp.concatenate(outs[N_HEADS:], axis=1), new_states


BOTH = 2 * N_HEADS


def _gdn_specs(n, first_backwards):
    idx = [(lambda i: n - 1 - i) if (d == 0) == first_backwards else (lambda i: i) for d in range(2)]

    def both(shape_of_block, index_tail):
        return [pl.BlockSpec(shape_of_block, lambda i, f=f: (f(i),) + index_tail) for f in idx]

    return idx, both


def _gdn_fwd(qkvn, gb, t):
    n = t // CHUNK
    idx, both = _gdn_specs(n, False)

    def body(qf, kf, vf, qb, kb, vb, gbf, gbb, of_ref, ob_ref, sf_ref, sb_ref, tf_ref, tb_ref, state):
        @pl.when(pl.program_id(0) == 0)
        def _():
            state[...] = jnp.zeros_like(state)

        qkv_f = (qf[...], kf[...], vf[...])
        qkv_b = (qb[...], kb[...], vb[...])
        gates = _chunk_gates_both(gbf[...], gbb[...])
        tmats = _tri_inverse(_chunk_lmat(qkv_f[1], qkv_b[1], gates))
        states = tuple(state[h] for h in range(BOTH))
        o_f, o_b, new_states = _chunk_out(qkv_f, qkv_b, gates, tmats, states)
        of_ref[...] = o_f
        ob_ref[...] = o_b
        for h in range(BOTH):
            s_ref, t_ref = (sf_ref, tf_ref) if h < N_HEADS else (sb_ref, tb_ref)
            s_ref[0, h % N_HEADS] = states[h]
            t_ref[0, h % N_HEADS] = tmats[h]
            state[h] = new_states[h]

    qkv_specs = [pl.BlockSpec((CHUNK, D_MODEL), lambda i, f=f, p=p: (f(i), p)) for f in idx for p in range(3)]
    return pl.pallas_call(
        body, name="gdn_fwd", grid=(n,),
        in_specs=qkv_specs + both((CHUNK, LANES), (0,)),
        out_specs=both((CHUNK, D_MODEL), (0,)) + both((1, N_HEADS, HEAD_DIM, HEAD_DIM), (0, 0, 0))
        + both((1, N_HEADS, CHUNK, CHUNK), (0, 0, 0)),
        out_shape=[jax.ShapeDtypeStruct((t, D_MODEL), F32)] * 2
        + [jax.ShapeDtypeStruct((n, N_HEADS, HEAD_DIM, HEAD_DIM), F32)] * 2
        + [jax.ShapeDtypeStruct((n, N_HEADS, CHUNK, CHUNK), F32)] * 2,
        scratch_shapes=[pltpu.VMEM((BOTH, HEAD_DIM, HEAD_DIM), F32)],
        compiler_params=_params(("arbitrary",)),
    )(*([qkvn] * 6), gb, gb)


def _gdn_bwd(qkvn, gb, s_f, s_b, t_f, t_b, do, t):
    n = t // CHUNK
    idx, both = _gdn_specs(n, True)

    def body(qf, kf, vf, qb, kb, vb, gbf, gbb, sf_ref, sb_ref, tf_ref, tb_ref, dof, dob,
             dqf_ref, dqb_ref, dgf_ref, dgb_ref, dstate):
        @pl.when(pl.program_id(0) == 0)
        def _():
            dstate[...] = jnp.zeros_like(dstate)

        qkv_f = (qf[...], kf[...], vf[...])
        qkv_b = (qb[...], kb[...], vb[...])
        tmats = tuple((tf_ref if h < N_HEADS else tb_ref)[0, h % N_HEADS] for h in range(BOTH))
        states = tuple((sf_ref if h < N_HEADS else sb_ref)[0, h % N_HEADS] for h in range(BOTH))
        gates, gates_vjp = jax.vjp(_chunk_gates_both, gbf[...], gbb[...])
        _, out_vjp = jax.vjp(_chunk_out, qkv_f, qkv_b, gates, tmats, states)
        d_f, d_b, dgates, dtm, dst = out_vjp((dof[...], dob[...], tuple(dstate[h] for h in range(BOTH))))
        firsts = [_dot(tm, d, 0, 0) for tm, d in zip(tmats, dtm)]
        dlm = tuple(-_dot(x, tm, 1, 1) for x, tm in zip(firsts, tmats))
        _, lmat_vjp = jax.vjp(_chunk_lmat, qkv_f[1], qkv_b[1], gates)
        dk_f, dk_b, dgates_lmat = lmat_vjp(dlm)
        dg_f, dg_b = gates_vjp(jax.tree.map(jnp.add, dgates, dgates_lmat))
        dqf_ref[...] = jnp.concatenate([d_f[0], d_f[1] + dk_f, d_f[2]], axis=1)
        dqb_ref[...] = jnp.concatenate([d_b[0], d_b[1] + dk_b, d_b[2]], axis=1)
        dgf_ref[...] = dg_f
        dgb_ref[...] = dg_b
        for h in range(BOTH):
            dstate[h] = dst[h]

    qkv_specs = [pl.BlockSpec((CHUNK, D_MODEL), lambda i, f=f, p=p: (f(i), p)) for f in idx for p in range(3)]
    return pl.pallas_call(
        body, name="gdn_bwd", grid=(n,),
        in_specs=qkv_specs + both((CHUNK, LANES), (0,)) + both((1, N_HEADS, HEAD_DIM, HEAD_DIM), (0, 0, 0))
        + both((1, N_HEADS, CHUNK, CHUNK), (0, 0, 0)) + both((CHUNK, D_MODEL), (0,)),
        out_specs=both((CHUNK, 3 * D_MODEL), (0,)) + both((CHUNK, LANES), (0,)),
        out_shape=[jax.ShapeDtypeStruct((t, 3 * D_MODEL), F32)] * 2 + [jax.ShapeDtypeStruct((t, LANES), F32)] * 2,
        scratch_shapes=[pltpu.VMEM((BOTH, HEAD_DIM, HEAD_DIM), F32)],
        compiler_params=_params(("arbitrary",)),
    )(*([qkvn] * 6), gb, gb, s_f, s_b, t_f, t_b, do, do)


def _mesh_pos():
    return lax.axis_index("x"), lax.axis_index("y"), lax.axis_index("c")


def _exchange_sems(na):
    return [pltpu.SemaphoreType.DMA((na, N_DEV - 1)), pltpu.SemaphoreType.DMA((na, N_DEV - 1)),
            pltpu.SemaphoreType.DMA((na,))]


def _gather_plan(x_refs, out_refs, send_sems, recv_sems, local_sems):
    na = len(x_refs)
    x, y, c = _mesh_pos()
    me, sibling = (x, y, c), (x, y, 1 - c)
    chips = [(1 - x, y), (x, 1 - y), (1 - x, 1 - y)]

    def block(a, px, py, pc):
        return out_refs[a].at[4 * px + 2 * py + pc]

    def copy(a, k, blk, to, src=None):
        return pltpu.make_async_remote_copy(
            src_ref=block(a, *blk) if src is None else src, dst_ref=block(a, *blk),
            send_sem=send_sems.at[a, k], recv_sem=recv_sems.at[a, k],
            device_id=to, device_id_type=pl.DeviceIdType.MESH)

    def mine(a):
        return pltpu.make_async_copy(x_refs[a], block(a, *me), local_sems.at[a])

    def first(a):
        return [copy(a, 0, me, sibling, src=x_refs[a])] + [
            copy(a, 1 + j, me, (*chip, c), src=x_refs[a]) for j, chip in enumerate(chips)]

    def passed(a, j):
        return copy(a, 4 + j, (*chips[j], c), sibling)

    def start():
        for a in range(na):
            mine(a).start()
            for cp in first(a):
                cp.start()

    def forward():
        for j, chip in enumerate(chips):
            for a in range(na):
                copy(a, 1 + j, (*chip, c), me).wait_recv()
                passed(a, j).start()

    def finish():
        for a in range(na):
            copy(a, 0, sibling, me).wait_recv()
            for j, chip in enumerate(chips):
                copy(a, 4 + j, (*chip, 1 - c), me).wait_recv()
        for a in range(na):
            for cp in first(a) + [passed(a, j) for j in range(len(chips))]:
                cp.wait_send()
            mine(a).wait()

    return start, forward, finish


def _scatter_plan(g_refs, land_refs, send_sems, recv_sems, local_sems):
    na = len(g_refs)
    x, y, c = _mesh_pos()
    mine = 4 * x + 2 * y + c

    def local(a):
        return pltpu.make_async_copy(g_refs[a].at[mine], land_refs[a].at[mine], local_sems.at[a])

    def peers():
        for k in range(1, N_DEV):
            px = 1 - x if k & 4 else x
            py = 1 - y if k & 2 else y
            pc = 1 - c if k & 1 else c
            yield k, (px, py, pc), 4 * px + 2 * py + pc

    def send(a, k, to, peer):
        return pltpu.make_async_remote_copy(
            src_ref=g_refs[a].at[peer], dst_ref=land_refs[a].at[mine],
            send_sem=send_sems.at[a, k - 1], recv_sem=recv_sems.at[a, k - 1],
            device_id=to, device_id_type=pl.DeviceIdType.MESH)

    def recv(a, k, peer):
        return pltpu.make_async_remote_copy(
            src_ref=g_refs[a].at[mine], dst_ref=land_refs[a].at[peer],
            send_sem=send_sems.at[a, k - 1], recv_sem=recv_sems.at[a, k - 1],
            device_id=(x, y, c), device_id_type=pl.DeviceIdType.MESH)

    def start():
        for a in range(na):
            local(a).start()
        for k, to, peer in peers():
            for a in range(na):
                send(a, k, to, peer).start()

    def finish():
        for k, to, peer in peers():
            for a in range(na):
                recv(a, k, peer).wait_recv()
        for k, to, peer in peers():
            for a in range(na):
                send(a, k, to, peer).wait_send()
        for a in range(na):
            local(a).wait()

    return start, (lambda: None), finish


def _exchange_call(name, plan, arrays, out_shapes):
    na = len(arrays)

    def body(*refs):
        start, forward, finish = plan(refs[:na], refs[na:2 * na], *refs[2 * na:])
        start()
        forward()
        finish()

    any_spec = pl.BlockSpec(memory_space=pl.ANY)
    return pl.pallas_call(body, name=name, out_shape=list(out_shapes), in_specs=[any_spec] * na,
                          out_specs=[any_spec] * na, scratch_shapes=_exchange_sems(na))(*arrays)


def _gathered_shapes(shards):
    return [jax.ShapeDtypeStruct((N_DEV,) + s.shape, s.dtype) for s in shards]


def _landed_shapes(blocks):
    return [jax.ShapeDtypeStruct(b.shape, b.dtype) for b in blocks]


def _sum_slots(land, name):
    _, rows, cols = land.shape
    tr = _row_tile(rows, cols * 4 * N_DEV, budget=4 << 20)

    def body(*refs):
        acc = refs[0][0].astype(F32)
        for ref in refs[1:N_DEV]:
            acc = acc + ref[0].astype(F32)
        refs[N_DEV][...] = acc

    return pl.pallas_call(
        body, name=name, grid=(rows // tr,),
        in_specs=[pl.BlockSpec((1, tr, cols), lambda i, s=s: (s, i, 0)) for s in range(N_DEV)],
        out_specs=pl.BlockSpec((tr, cols), lambda i: (i, 0)),
        out_shape=jax.ShapeDtypeStruct((rows, cols), F32),
        compiler_params=_params(("parallel",)),
    )(*([land] * N_DEV))


def _row_tile(rows, row_bytes, budget=1 << 20):
    if rows * row_bytes <= budget or rows % SUBLANES:
        return rows
    best = SUBLANES
    for tr in range(SUBLANES, rows + 1, SUBLANES):
        if rows % tr == 0 and tr * row_bytes <= budget:
            best = tr
    return best


def _adamw(w, g, m, v, name):
    shape = w.shape
    cols = shape[-1]
    rows = w.size // cols
    tr = _row_tile(rows, cols * 4)
    b1, b2 = ADAM["b1"], ADAM["b2"]

    def body(w_ref, g_ref, m_ref, v_ref, d_ref, nm_ref, nv_ref):
        gv = g_ref[...]
        nm = b1 * m_ref[...] + (1.0 - b1) * gv
        nv = b2 * v_ref[...] + (1.0 - b2) * jnp.square(gv)
        m_hat = nm / (1.0 - b1 ** ADAM["step"])
        v_hat = nv / (1.0 - b2 ** ADAM["step"])
        d_ref[...] = -ADAM["lr"] * (m_hat / (jnp.sqrt(v_hat) + ADAM["eps"]) + ADAM["wd"] * w_ref[...])
        nm_ref[...] = nm
        nv_ref[...] = nv

    spec = pl.BlockSpec((tr, cols), lambda i: (i, 0))
    outs = pl.pallas_call(
        body, name=name, grid=(rows // tr,), in_specs=[spec] * 4, out_specs=[spec] * 3,
        out_shape=[jax.ShapeDtypeStruct((rows, cols), F32)] * 3, compiler_params=_params(("parallel",)),
    )(*[a.reshape(rows, cols) for a in (w, g, m, v)])
    return [o.reshape(shape) for o in outs]


MATRICES = ("w_in", "w_o_gdn", "w_o_sc", "w_out", "w_up", "w_down")
CONVS = ("conv_qkv", "conv_sc")
SHARDED = ("w_in", "conv_qkv", "w_o_gdn", "conv_sc", "w_o_sc", "w_out", "w_up", "w_down")
SMALL = ("a_log", "dt_bias", "gdn_norm_w", "ln1_g", "ln1_b", "b_up", "b_down", "ln2_g", "ln2_b")
COLUMN_SHARDED = ("w_in", "conv_qkv", "conv_sc", "w_up")
PACK_COLS = 1024


def _pack(parts, row_multiple):
    flat = jnp.concatenate(parts, axis=-1)
    unit = PACK_COLS * row_multiple
    pad = -flat.shape[-1] % unit
    flat = jnp.pad(flat, [(0, 0)] * (flat.ndim - 1) + [(0, pad)])
    return flat.reshape(flat.shape[:-1] + (flat.shape[-1] // PACK_COLS, PACK_COLS))


def _unshard(name, blocks):
    _, l, r, c = blocks.shape
    if name in COLUMN_SHARDED:
        return blocks.transpose(1, 2, 0, 3).reshape(l, r, N_DEV * c)
    return blocks.transpose(1, 0, 2, 3).reshape(l, N_DEV * r, c)


def _to_shards(name, full):
    l, r, c = full.shape
    if name in COLUMN_SHARDED:
        return full.reshape(l, r, N_DEV, c // N_DEV).transpose(2, 0, 1, 3).reshape(N_DEV, -1)
    return full.reshape(l, N_DEV, r // N_DEV, c).transpose(1, 0, 2, 3).reshape(N_DEV, -1)


def _pack_weights(shards):
    parts, layout = [], []
    for name in MATRICES[1:]:
        parts.append(shards[name].astype(BF16).reshape(-1))
        layout.append((name, shards[name].shape, 1))
    for name in CONVS:
        parts.append(jnp.stack(_split3(shards[name])).reshape(-1))
        layout.append((name, shards[name].shape, 3))
    w_in = shards["w_in"]
    return [w_in.astype(BF16).reshape(-1, w_in.shape[-1]), _pack(parts, 16)], (layout, w_in.shape)


def _unpack_weights(gathered, meta):
    (w_in_all, rest_all), (layout, w_in_shape) = gathered, meta
    full = {"w_in": _unshard("w_in", w_in_all.reshape(N_DEV, *w_in_shape))}
    rest_all, off = rest_all.reshape(N_DEV, -1), 0
    for name, shape, pieces in layout:
        size = pieces * shape[0] * shape[1] * shape[2]
        blk = rest_all[:, off:off + size]
        off += size
        if pieces == 3:
            blk = jnp.sum(blk.reshape(N_DEV, 3, *shape).astype(F32), axis=1)
        full[name] = _unshard(name, blk.reshape(N_DEV, *shape))
    return full


def _unpack(flat, names, shapes):
    out, off = {}, 0
    for name in names:
        size = 1
        for s in shapes[name]:
            size *= s
        out[name] = flat[off:off + size].reshape(shapes[name])
        off += size
    return out


def _pack_w_in_grad(dw_in):
    r, c = dw_in.shape
    return dw_in.astype(BF16).reshape(r, N_DEV, c // N_DEV).transpose(1, 0, 2)


def _pack_rest_grads(grads):
    return _pack([_to_shards(name, grads[name].astype(BF16)[None]) for name in SHARDED[1:]], LANES)


def _pack_small_grads(small_grads):
    small = jnp.concatenate([small_grads[name].reshape(-1) for name in SMALL])
    return _pack([jnp.broadcast_to(small[None, :], (N_DEV, small.shape[0]))], SUBLANES)


def _sum_rest_grads(landed_rest, layer_shapes):
    shapes = {name: (1,) + layer_shapes[name] for name in SHARDED[1:]}
    rest = _unpack(_sum_slots(landed_rest, "grads_sum_rest").reshape(-1), SHARDED[1:], shapes)
    return {name: val[0] for name, val in rest.items()}


def _lane_row(values):
    flat = values.reshape(1, -1)
    return jnp.pad(flat, ((0, 0), (0, LANES - flat.shape[1])))


def _forward_layer(x, x16, w, t, next_arrays):
    if next_arrays is None:
        proj, gathered = _mm(x16, w["w_main"], "nn", "proj_main"), None
    else:
        proj, gathered = _mm(x16, w["w_main"], "nn", "proj_main_gather",
                             exchange=(_gather_plan, next_arrays, _gathered_shapes(next_arrays)))
    proj_ab = _mm(x16, w["w_ab"], "nn", "proj_ab")
    conv_out, qkvn = _qkv_conv_fwd(proj, w["conv_qkv"], t)
    gb = _gating_fwd(proj_ab, w["a_log"], w["dt_bias"], t)
    o_f, o_b, s_f, s_b, t_f, t_b = _gdn_fwd(qkvn, gb, t)
    og = _gate_norm_fwd(o_f, o_b, proj, w["gdn_norm_w"], t)
    s = _sc_fwd(proj, w["conv_sc"], t)
    y_a = _mm(og, w["w_o_gdn"], "nn", "y_gdn")
    y_b = _mm(s, w["w_o_sc"], "nn", "y_sc")
    mixed = _mix_fwd(proj, y_a, y_b, t)
    assert D_MODEL <= MM_TILE
    r1, x1, x1_16 = _mm(mixed, w["w_out"], "nn", "out_proj", tm=LN_MM_ROWS,
                        epilogue=_ln1_epilogue(x, w["ln1_g"], w["ln1_b"]))
    hpre, h = _mm(x1_16, w["w_up"], "nn", "mlp_up", epilogue=_relu2_epilogue(w["b_up"]))
    r2, x2, x2_16 = _mm(h, w["w_down"], "nn", "mlp_down",
                        epilogue=_ln2_epilogue(x1, w["b_down"], w["ln2_g"], w["ln2_b"]))
    saved = dict(x=x, x16=x16, proj=proj, proj_ab=proj_ab, conv_out=conv_out, qkvn=qkvn, gb=gb, o_f=o_f, o_b=o_b,
                 s_f=s_f, s_b=s_b, t_f=t_f, t_b=t_b, og=og, s=s, y_a=y_a, y_b=y_b, mixed=mixed, r1=r1, x1=x1,
                 x1_16=x1_16, hpre=hpre, h=h, r2=r2)
    return x2, x2_16, saved, gathered


def _backward_layer(dx2, w, a, t, pending_w_in):
    (dx1_a, dr2), (db_down, dg2, db2) = _ln2_bwd(a["x1"], a["r2"], w["b_down"], w["ln2_g"], w["ln2_b"], dx2, t)
    relu2_back = _relu2_back_epilogue(a["hpre"], w["b_up"])
    if pending_w_in is None:
        (dhpre, db_up_rows), landed_w_in = _mm(dr2, w["w_down"], "nt", "d_h", epilogue=relu2_back), None
    else:
        (dhpre, db_up_rows), (landed_w_in,) = _mm(
            dr2, w["w_down"], "nt", "d_h_scatter", epilogue=relu2_back,
            exchange=(_scatter_plan, [pending_w_in], _landed_shapes([pending_w_in])))
    dx1 = _mm(dhpre, w["w_up"], "nt", "d_x1", addends=[(1.0, dx1_a)])
    db_up = jnp.sum(db_up_rows, axis=0, keepdims=True)
    dw_down = _mm(a["h"], dr2, "tn", "dw_down")
    dw_up = _mm(a["x1_16"], dhpre, "tn", "dw_up")
    (dx_a, dr1), (dg1, db1) = _ln1_bwd(a["x"], a["r1"], w["ln1_g"], w["ln1_b"], dx1, t)
    dmixed = _mm(dr1, w["w_out"], "nt", "d_mixed", out_dtype=BF16)
    dw_out = _mm(a["mixed"], dr1, "tn", "dw_out")
    dgates, dy_a, dy_b = _mix_bwd(a["proj"], a["y_a"], a["y_b"], dmixed, t)
    dog = _mm(dy_a, w["w_o_gdn"], "nt", "d_og", out_dtype=BF16)
    dw_o_gdn = _mm(a["og"], dy_a, "tn", "dw_o_gdn")
    ds = _mm(dy_b, w["w_o_sc"], "nt", "d_s")
    dw_o_sc = _mm(a["s"], dy_b, "tn", "dw_o_sc")
    dsc, dconv_sc = _sc_bwd(a["proj"], w["conv_sc"], ds, t)
    do, dz, dnorm_w = _gate_norm_bwd(a["o_f"], a["o_b"], a["proj"], w["gdn_norm_w"], dog, t)
    dq_f, dq_b, dgb_f, dgb_b = _gdn_bwd(a["qkvn"], a["gb"], a["s_f"], a["s_b"], a["t_f"], a["t_b"], do, t)
    dqkv, dconv_qkv = _qkv_conv_bwd(a["proj"], w["conv_qkv"], a["conv_out"], dq_f, dq_b, t)
    dab, da_log, ddt_bias = _gating_bwd(a["proj_ab"], w["a_log"], w["dt_bias"], dgb_f, dgb_b, t)

    grads = dict(conv_qkv=dconv_qkv[:3], a_log=da_log[0, :2 * N_HEADS].reshape(2, N_HEADS),
                 dt_bias=ddt_bias[0, :2 * N_HEADS].reshape(2, N_HEADS), gdn_norm_w=dnorm_w[0], w_o_gdn=dw_o_gdn,
                 conv_sc=dconv_sc[:3], w_o_sc=dw_o_sc, w_out=dw_out, ln1_g=dg1[0], ln1_b=db1[0], w_up=dw_up,
                 b_up=db_up[0], w_down=dw_down, b_down=db_down[0], ln2_g=dg2[0], ln2_b=db2[0])
    rest_blocks = [_pack_rest_grads(grads)]
    pieces = [dqkv, dz, dsc, dgates]
    dx = _mm(dab, w["w_ab"], "nt", "dx_ab", addends=[(1.0, dx_a)])
    dx, (landed_rest,) = _mm_nt_pieces(pieces, w["w_main"], "dx_main", dx,
                                       (_scatter_plan, rest_blocks, _landed_shapes(rest_blocks)))
    dw_main = jnp.concatenate([_mm(a["x16"], piece, "tn", f"dw_in_{j}") for j, piece in enumerate(pieces)], axis=1)
    dw_ab = _mm(a["x16"], dab, "tn", "dw_ab")
    grads["w_in"] = jnp.concatenate([dw_main[:, :QKVZ_COLS], dw_ab[:, :AB_COLS], dw_main[:, QKVZ_COLS:]], axis=1)
    return dx, grads, landed_w_in, landed_rest


def kernel(x, w_in, conv_qkv, a_log, dt_bias, gdn_norm_w, w_o_gdn, conv_sc, w_o_sc, w_out, ln1_g, ln1_b, w_up, b_up, w_down, b_down, ln2_g, ln2_b, loss_target, m_w_in, m_conv_qkv, m_a_log, m_dt_bias, m_gdn_norm_w, m_w_o_gdn, m_conv_sc, m_w_o_sc, m_w_out, m_ln1_g, m_ln1_b, m_w_up, m_b_up, m_w_down, m_b_down, m_ln2_g, m_ln2_b, v_w_in, v_conv_qkv, v_a_log, v_dt_bias, v_gdn_norm_w, v_w_o_gdn, v_conv_sc, v_w_o_sc, v_w_out, v_ln1_g, v_ln1_b, v_w_up, v_b_up, v_w_down, v_b_down, v_ln2_g, v_ln2_b):
    weights = dict(w_in=w_in, conv_qkv=conv_qkv, a_log=a_log, dt_bias=dt_bias, gdn_norm_w=gdn_norm_w,
                   w_o_gdn=w_o_gdn, conv_sc=conv_sc, w_o_sc=w_o_sc, w_out=w_out, ln1_g=ln1_g, ln1_b=ln1_b,
                   w_up=w_up, b_up=b_up, w_down=w_down, b_down=b_down, ln2_g=ln2_g, ln2_b=ln2_b)
    m_in = dict(w_in=m_w_in, conv_qkv=m_conv_qkv, a_log=m_a_log, dt_bias=m_dt_bias, gdn_norm_w=m_gdn_norm_w,
                w_o_gdn=m_w_o_gdn, conv_sc=m_conv_sc, w_o_sc=m_w_o_sc, w_out=m_w_out, ln1_g=m_ln1_g, ln1_b=m_ln1_b,
                w_up=m_w_up, b_up=m_b_up, w_down=m_w_down, b_down=m_b_down, ln2_g=m_ln2_g, ln2_b=m_ln2_b)
    v_in = dict(w_in=v_w_in, conv_qkv=v_conv_qkv, a_log=v_a_log, dt_bias=v_dt_bias, gdn_norm_w=v_gdn_norm_w,
                w_o_gdn=v_w_o_gdn, conv_sc=v_conv_sc, w_o_sc=v_w_o_sc, w_out=v_w_out, ln1_g=v_ln1_g, ln1_b=v_ln1_b,
                w_up=v_w_up, b_up=v_b_up, w_down=v_w_down, b_down=v_b_down, ln2_g=v_ln2_g, ln2_b=v_ln2_b)
    t = x.shape[1]
    depth = w_in.shape[0]

    def layer_weights(full, l):
        w_in_l = full["w_in"][0]
        return dict(
            w_main=jnp.concatenate([w_in_l[:, :QKVZ_COLS], w_in_l[:, QKVZ_COLS + AB_COLS:]], axis=1),
            w_ab=jnp.pad(w_in_l[:, QKVZ_COLS:QKVZ_COLS + AB_COLS], ((0, 0), (0, LANES - AB_COLS))),
            conv_qkv=jnp.pad(full["conv_qkv"][0], ((0, SUBLANES - 3), (0, 0))),
            conv_sc=jnp.pad(full["conv_sc"][0], ((0, SUBLANES - 3), (0, 0))),
            a_log=_lane_row(a_log[l]), dt_bias=_lane_row(dt_bias[l]), gdn_norm_w=gdn_norm_w[l][None, :],
            w_o_gdn=full["w_o_gdn"][0], w_o_sc=full["w_o_sc"][0], w_out=full["w_out"][0],
            ln1_g=ln1_g[l][None, :], ln1_b=ln1_b[l][None, :], w_up=full["w_up"][0], b_up=b_up[l][None, :],
            w_down=full["w_down"][0], b_down=b_down[l][None, :], ln2_g=ln2_g[l][None, :], ln2_b=ln2_b[l][None, :])

    packed = [_pack_weights({name: weights[name][l:l + 1] for name in MATRICES + CONVS}) for l in range(depth)]
    gathered = _exchange_call("weights_all_gather", _gather_plan, packed[0][0], _gathered_shapes(packed[0][0]))
    h = x.reshape(t, D_MODEL)
    h16 = h.astype(BF16)
    layers, saved = [], []
    for l in range(depth):
        layers.append(layer_weights(_unpack_weights(gathered, packed[l][1]), l))
        next_arrays = packed[l + 1][0] if l + 1 < depth else None
        h, h16, acts, gathered = _forward_layer(h, h16, layers[l], t, next_arrays)
        saved.append(acts)
    dh, loss_local = _loss_stage(h, loss_target.reshape(t, D_MODEL), t)
    loss = lax.psum(loss_local, MESH_AXES)

    layer_shapes = {name: weights[name].shape[1:] for name in SHARDED}
    layer_grads, reduced, pending_w_in = [None] * depth, [None] * depth, None
    for l in reversed(range(depth)):
        dh, layer_grads[l], landed_w_in, landed_rest = _backward_layer(dh, layers[l], saved[l], t, pending_w_in)
        reduced[l] = _sum_rest_grads(landed_rest, layer_shapes)
        if landed_w_in is not None:
            reduced[l + 1]["w_in"] = _sum_slots(landed_w_in, "grads_sum_w_in")
        pending_w_in = _pack_w_in_grad(layer_grads[l]["w_in"])
    small = _pack_small_grads({name: jnp.stack([g[name] for g in layer_grads]) for name in SMALL})
    last = [pending_w_in, small]
    landed = _exchange_call("grads_scatter", _scatter_plan, last, _landed_shapes(last))
    reduced[0]["w_in"] = _sum_slots(landed[0], "grads_sum_w_in")
    grads = {name: jnp.stack([r[name] for r in reduced]) for name in SHARDED}
    small_shapes = {name: weights[name].shape for name in SMALL}
    grads.update(_unpack(_sum_slots(landed[1], "grads_sum_small").reshape(-1), SMALL, small_shapes))

    names = list(weights)
    updates = {n: _adamw(weights[n], grads[n], m_in[n], v_in[n], f"adamw_{n}") for n in names}
    return (loss, dh.reshape(x.shape), *[grads[n] for n in names], *[updates[n][0] for n in names],
            *[updates[n][1] for n in names], *[updates[n][2] for n in names])
```

```python
import functools

import jax
import jax.numpy as jnp
from jax import lax
from jax.experimental import pallas as pl
from jax.experimental.pallas import tpu as pltpu

F32 = jnp.float32
BF16 = jnp.bfloat16

D_MODEL = 1024
N_HEADS = 8
HEAD_DIM = 128
CHUNK = 64
D_FF = 4 * D_MODEL
DEPTH = 4
N_DEV = 8
LN_EPS = 1e-5
RMS_EPS = 1e-6
L2_EPS = 1e-6
ALPHA = (2 * DEPTH) ** 0.25
MAIN_COLS = 9 * D_MODEL
QKVZ_COLS = 4 * D_MODEL
AB_COLS = 4 * N_HEADS
W_IN_COLS = MAIN_COLS + AB_COLS
LANES = 128
SUBLANES = 8
VMEM_LIMIT = 48 * 1024 * 1024
MM_TILE = 1024
LN_BWD_ROW_TILE = 512
ROW_TILE = 256
WIDE_ROW_TILE = 128
ADAM = dict(lr=0.001, b1=0.9, b2=0.999, eps=1e-08, wd=0.01, step=10)
MESH_AXES = ("x", "y", "c")


def _params(sem=None):
    return pltpu.CompilerParams(dimension_semantics=sem, vmem_limit_bytes=VMEM_LIMIT)


_DIMS = {"nn": (1, 0), "nt": (1, 1), "tn": (0, 0)}


def _mm(a, b, mode, name, *, out_dtype=F32, addends=(), epilogue=None, exchange=None):
    if mode == "nn":
        (m, k), n = a.shape, b.shape[1]
    elif mode == "nt":
        (m, k), n = a.shape, b.shape[0]
    else:
        (k, m), n = a.shape, b.shape[1]
    tm, tn, tk = min(m, MM_TILE), min(n, MM_TILE), min(k, MM_TILE)
    assert m % tm == 0 and n % tn == 0 and k % tk == 0
    nk = k // tk
    ca, cb = _DIMS[mode]
    scales = tuple(s for s, _ in addends)
    na = len(addends)
    epi_fn, epi_in, epi_out = epilogue or (None, (), (("tile", out_dtype),))
    ne, no = len(epi_in), len(epi_out)
    plan, x_arrays, x_shapes = exchange or (None, (), ())
    nx = len(x_arrays)
    gi, gj = m // tm, n // tn

    def body(a_ref, b_ref, *rest):
        add_refs, epi_refs = rest[:na], rest[na:na + ne]
        xin_refs, rest = rest[na + ne:na + ne + nx], rest[na + ne + nx:]
        o_refs, xout_refs, rest = rest[:no], rest[no:no + nx], rest[no + nx:]
        kk = pl.program_id(2)
        if plan:
            start, forward, finish_exchange = plan(xin_refs, xout_refs, *rest[-3:])
            i, j = pl.program_id(0), pl.program_id(1)

            @pl.when((i == 0) & (j == 0) & (kk == 0))
            def _():
                start()

            @pl.when((i == gi // 2) & (j == gj // 2) & (kk == 0))
            def _():
                forward()

        p = lax.dot_general(a_ref[...].astype(BF16), b_ref[...].astype(BF16), (((ca,), (cb,)), ((), ())),
                            preferred_element_type=F32)

        def finish(r):
            for s, ref in zip(scales, add_refs):
                r = r + s * ref[...].astype(F32)
            vals = epi_fn(r, *[ref[...] for ref in epi_refs]) if epi_fn else (r,)
            for (kind, _), ref, val in zip(epi_out, o_refs, vals):
                if kind == "colsum":
                    val = jnp.where(lax.broadcasted_iota(jnp.int32, ref.shape, 0) == 0, val, 0.0)
                ref[...] = val.astype(ref.dtype)

        if nk == 1:
            finish(p)
        else:
            acc = rest[0]

            @pl.when(kk == 0)
            def _():
                acc[...] = p

            @pl.when(kk > 0)
            def _():
                acc[...] += p

            @pl.when(kk == nk - 1)
            def _():
                finish(acc[...])

        if plan:
            @pl.when((i == gi - 1) & (j == gj - 1) & (kk == nk - 1))
            def _():
                finish_exchange()

    if mode == "nn":
        a_spec = pl.BlockSpec((tm, tk), lambda i, j, kk: (i, kk))
        b_spec = pl.BlockSpec((tk, tn), lambda i, j, kk: (kk, j))
    elif mode == "nt":
        a_spec = pl.BlockSpec((tm, tk), lambda i, j, kk: (i, kk))
        b_spec = pl.BlockSpec((tn, tk), lambda i, j, kk: (j, kk))
    else:
        a_spec = pl.BlockSpec((tk, tm), lambda i, j, kk: (kk, i))
        b_spec = pl.BlockSpec((tk, tn), lambda i, j, kk: (kk, j))
    kinds = {"tile": (pl.BlockSpec((tm, tn), lambda i, j, kk: (i, j)), (m, n)),
             "row": (pl.BlockSpec((1, tn), lambda i, j, kk: (0, j)), (1, n)),
             "colsum": (pl.BlockSpec((SUBLANES, tn), lambda i, j, kk: (i, j)), (SUBLANES * (m // tm), n))}
    o_spec = kinds["tile"][0]
    any_spec = pl.BlockSpec(memory_space=pl.ANY)
    res = pl.pallas_call(
        body, name=name, grid=(gi, gj, nk),
        in_specs=[a_spec, b_spec] + [o_spec] * na + [kinds[kind][0] for _, kind in epi_in] + [any_spec] * nx,
        out_specs=[kinds[kind][0] for kind, _ in epi_out] + [any_spec] * nx,
        out_shape=[jax.ShapeDtypeStruct(kinds[kind][1], dt) for kind, dt in epi_out] + list(x_shapes),
        scratch_shapes=([pltpu.VMEM((tm, tn), F32)] if nk > 1 else []) + (_exchange_sems(nx) if plan else []),
        compiler_params=_params(("arbitrary",) * 3 if plan else ("parallel", "parallel", "arbitrary")),
    )(a, b, *[arr for _, arr in addends], *[arr for arr, _ in epi_in], *x_arrays)
    if plan:
        return (tuple(res[:no]) if epilogue else res[0]), tuple(res[no:])
    return res if epilogue else res[0]


def _mm_nt_pieces(pieces, b, name, addend, exchange):
    m, n = pieces[0].shape[0], b.shape[0]
    tm, tn, tk = min(m, MM_TILE), min(n, MM_TILE), min(b.shape[1], MM_TILE)
    assert m % tm == 0 and n % tn == 0 and all(p.shape[1] % tk == 0 for p in pieces)
    counts = [p.shape[1] // tk for p in pieces]
    starts = [sum(counts[:j]) for j in range(len(pieces))]
    nk, npieces = sum(counts), len(pieces)
    assert nk * tk == b.shape[1]
    plan, x_arrays, x_shapes = exchange
    nx = len(x_arrays)
    gi, gj = m // tm, n // tn

    def body(*refs):
        a_refs, b_ref, add_ref = refs[:npieces], refs[npieces], refs[npieces + 1]
        xin_refs, rest = refs[npieces + 2:npieces + 2 + nx], refs[npieces + 2 + nx:]
        o_ref, xout_refs, acc = rest[0], rest[1:1 + nx], rest[1 + nx]
        start_exchange, forward, finish_exchange = plan(xin_refs, xout_refs, *rest[-3:])
        i, j, kk = pl.program_id(0), pl.program_id(1), pl.program_id(2)

        @pl.when((i == 0) & (j == 0) & (kk == 0))
        def _():
            start_exchange()

        @pl.when((i == gi // 2) & (j == gj // 2) & (kk == 0))
        def _():
            forward()

        for a_ref, start, count in zip(a_refs, starts, counts):
            @pl.when((kk >= start) & (kk < start + count))
            def _(a_ref=a_ref):
                p = lax.dot_general(a_ref[...].astype(BF16), b_ref[...].astype(BF16), (((1,), (1,)), ((), ())),
                                    preferred_element_type=F32)

                @pl.when(kk == 0)
                def _():
                    acc[...] = p

                @pl.when(kk > 0)
                def _():
                    acc[...] += p

        @pl.when(kk == nk - 1)
        def _():
            o_ref[...] = acc[...] + add_ref[...]

        @pl.when((i == gi - 1) & (j == gj - 1) & (kk == nk - 1))
        def _():
            finish_exchange()

    a_specs = [pl.BlockSpec((tm, tk), lambda i, j, kk, s=s, c=c: (i, jnp.clip(kk - s, 0, c - 1)))
               for s, c in zip(starts, counts)]
    o_spec = pl.BlockSpec((tm, tn), lambda i, j, kk: (i, j))
    any_spec = pl.BlockSpec(memory_space=pl.ANY)
    res = pl.pallas_call(
        body, name=name, grid=(gi, gj, nk),
        in_specs=a_specs + [pl.BlockSpec((tn, tk), lambda i, j, kk: (j, kk)), o_spec] + [any_spec] * nx,
        out_specs=[o_spec] + [any_spec] * nx,
        out_shape=[jax.ShapeDtypeStruct((m, n), F32)] + list(x_shapes),
        scratch_shapes=[pltpu.VMEM((tm, tn), F32)] + _exchange_sems(nx),
        compiler_params=_params(("arbitrary",) * 3),
    )(*pieces, b, addend, *x_arrays)
    return res[0], tuple(res[1:])


def _tile_call(name, body, t, tm, tiled, halo, params, outs, accs):
    tm = min(tm, t)
    assert t % tm == 0 and tm % SUBLANES == 0
    steps = t // tm
    hb = tm // SUBLANES
    nt, nh, npar, no = len(tiled), len(halo), len(params), len(outs)

    def kern(*refs):
        i = pl.program_id(0)
        t_refs = refs[:nt + nh]
        h_refs = refs[nt + nh:nt + 3 * nh]
        p_refs = refs[nt + 3 * nh:nt + 3 * nh + npar]
        o_refs = refs[nt + 3 * nh + npar:nt + 3 * nh + npar + no]
        a_refs = refs[nt + 3 * nh + npar + no:]
        tiles = [r[...].astype(F32) for r in t_refs]
        halos = []
        for j in range(nh):
            prev = h_refs[2 * j][SUBLANES - 1:SUBLANES, :].astype(F32)
            nxt = h_refs[2 * j + 1][0:1, :].astype(F32)
            halos.append((jnp.where(i > 0, prev, 0.0), jnp.where(i < steps - 1, nxt, 0.0)))
        o_vals, a_vals = body(tiles, halos, [r[...] for r in p_refs])
        for ref, val in zip(o_refs, o_vals):
            ref[...] = val.astype(ref.dtype)
        for ref, val in zip(a_refs, a_vals):
            @pl.when(i == 0)
            def _(ref=ref, val=val):
                ref[...] = val

            @pl.when(i > 0)
            def _(ref=ref, val=val):
                ref[...] += val

    in_specs, args = [], []
    for arr, nc, cb in list(tiled) + list(halo):
        in_specs.append(pl.BlockSpec((tm, nc), lambda i, cb=cb: (i, cb)))
        args.append(arr)
    last = t // SUBLANES - 1
    for arr, nc, cb in halo:
        in_specs.append(pl.BlockSpec((SUBLANES, nc), lambda i, cb=cb: (jnp.maximum(i * hb - 1, 0), cb)))
        in_specs.append(pl.BlockSpec((SUBLANES, nc), lambda i, cb=cb: (jnp.minimum((i + 1) * hb, last), cb)))
        args += [arr, arr]
    for arr in params:
        in_specs.append(pl.BlockSpec(arr.shape, lambda i: (0, 0)))
        args.append(arr)
    out_specs = [pl.BlockSpec((tm, nc), lambda i: (i, 0)) for nc, _ in outs]
    out_specs += [pl.BlockSpec(shape, lambda i: (0, 0)) for shape in accs]
    out_shape = [jax.ShapeDtypeStruct((t, nc), dt) for nc, dt in outs]
    out_shape += [jax.ShapeDtypeStruct(shape, F32) for shape in accs]
    res = pl.pallas_call(kern, name=name, grid=(steps,), in_specs=in_specs, out_specs=out_specs,
                         out_shape=out_shape, compiler_params=_params(("arbitrary",)))(*args)
    return res[:no], res[no:]


def _row_iota(x):
    return lax.broadcasted_iota(jnp.int32, x.shape, 0)


def _lane_iota(x):
    return lax.broadcasted_iota(jnp.int32, x.shape, 1)


def _shift_down(x, first_row):
    return jnp.where(_row_iota(x) == 0, first_row, pltpu.roll(x, 1, 0))


def _shift_up(x, last_row):
    n = x.shape[0]
    return jnp.where(_row_iota(x) == n - 1, last_row, pltpu.roll(x, n - 1, 0))


def _taps(w):
    return w[0:1, :], w[1:2, :], w[2:3, :]


def _tap_rows(d0, d1, d2, rows=SUBLANES):
    r = lax.broadcasted_iota(jnp.int32, (rows, d0.shape[1]), 0)
    return jnp.where(r == 0, d0, jnp.where(r == 1, d1, jnp.where(r == 2, d2, 0.0)))


def _colsum(x):
    return jnp.sum(x, axis=0, keepdims=True)


def _silu(x):
    return x * jax.nn.sigmoid(x)


def _softplus(x):
    return jnp.maximum(x, 0.0) + jnp.log(1.0 + jnp.exp(-jnp.abs(x)))


def _heads(x):
    return [x[:, h * HEAD_DIM:(h + 1) * HEAD_DIM] for h in range(x.shape[1] // HEAD_DIM)]


def _post_conv(c):
    blocks = _heads(_silu(c))
    out = []
    for j, blk in enumerate(blocks):
        if j < 2 * N_HEADS:
            blk = blk * lax.rsqrt(jnp.sum(blk * blk, axis=-1, keepdims=True) + L2_EPS)
        if j < N_HEADS:
            blk = blk * (HEAD_DIM ** -0.5)
        out.append(blk)
    return jnp.concatenate(out, axis=1)


def _gating(ab, a_log, dt_bias):
    lane = _lane_iota(ab)
    g = -jnp.exp(a_log) * _softplus(ab + dt_bias)
    return jnp.where(lane < 2 * N_HEADS, g, jnp.where(lane < AB_COLS, jax.nn.sigmoid(ab), 0.0))


def _gate_norm(o_f, o_b, z, norm_w):
    out = []
    for oh, zh in zip(_heads(o_f + o_b), _heads(z)):
        out.append(oh * lax.rsqrt(jnp.mean(oh * oh, axis=-1, keepdims=True) + RMS_EPS) * norm_w * _silu(zh))
    return jnp.concatenate(out, axis=1)


def _mix(gate_a, gate_b, y_a, y_b):
    return jax.nn.sigmoid(gate_a) * y_a + jax.nn.sigmoid(gate_b) * y_b


def _layer_norm(u, g, b):
    mu = jnp.mean(u, axis=-1, keepdims=True)
    var = jnp.mean(jnp.square(u - mu), axis=-1, keepdims=True)
    return (u - mu) * lax.rsqrt(var + LN_EPS) * g + b


def _ln1(x, r, g, b):
    return _layer_norm(ALPHA * x + r, g, b)


def _ln2(x, r, bias, g, b):
    return _layer_norm(ALPHA * x + r + bias, g, b)


def _relu2(hpre, bias):
    return jnp.square(jnp.maximum(hpre + bias, 0.0))


def _qkv_conv_fwd(proj, conv_w, t):
    def body(tiles, halos, params):
        (x,), ((xp, xn),), (w,) = tiles, halos, params
        w0, w1, w2 = _taps(w)
        c = w0 * _shift_down(x, xp) + w1 * x + w2 * _shift_up(x, xn)
        return [c, _post_conv(c)], []

    (c, qkvn), _ = _tile_call("qkv_conv_fwd", body, t, WIDE_ROW_TILE, [], [(proj, 3 * D_MODEL, 0)], [conv_w],
                              [(3 * D_MODEL, F32), (3 * D_MODEL, F32)], [])
    return c, qkvn


def _gating_fwd(proj_ab, a_log, dt_bias, t):
    def body(tiles, halos, params):
        return [_gating(tiles[0], params[0], params[1])], []

    (gb,), _ = _tile_call("gating_fwd", body, t, ROW_TILE, [(proj_ab, LANES, 0)], [], [a_log, dt_bias],
                          [(LANES, F32)], [])
    return gb


def _gate_norm_fwd(o_f, o_b, proj, norm_w, t):
    def body(tiles, halos, params):
        return [_gate_norm(tiles[0], tiles[1], tiles[2], params[0])], []

    (og,), _ = _tile_call("gate_norm_fwd", body, t, ROW_TILE,
                          [(o_f, D_MODEL, 0), (o_b, D_MODEL, 0), (proj, D_MODEL, 3)], [], [norm_w],
                          [(D_MODEL, BF16)], [])
    return og


def _sc_fwd(proj, conv_w, t):
    def body(tiles, halos, params):
        (sb,), ((cp, cn), (xp, xn)), (w,) = tiles[:1], halos, params
        sc, sx = tiles[1], tiles[2]
        w0, w1, w2 = _taps(w)
        u = sc * sx
        return [sb * (w0 * _shift_down(u, cp * xp) + w1 * u + w2 * _shift_up(u, cn * xn))], []

    (s,), _ = _tile_call("sc_fwd", body, t, ROW_TILE, [(proj, D_MODEL, 4)],
                         [(proj, D_MODEL, 5), (proj, D_MODEL, 6)], [conv_w], [(D_MODEL, BF16)], [])
    return s


def _mix_fwd(proj, y_a, y_b, t):
    def body(tiles, halos, params):
        return [_mix(*tiles)], []

    (mixed,), _ = _tile_call("mix_fwd", body, t, ROW_TILE,
                             [(proj, D_MODEL, 7), (proj, D_MODEL, 8), (y_a, D_MODEL, 0), (y_b, D_MODEL, 0)], [], [],
                             [(D_MODEL, BF16)], [])
    return mixed


LN_OUTS = [("tile", F32), ("tile", F32), ("tile", BF16)]


def _ln1_epilogue(x, g, b):
    def fn(r, xv, gv, bv):
        y = _ln1(xv, r, gv, bv)
        return r, y, y

    return fn, [(x, "tile"), (g, "row"), (b, "row")], LN_OUTS


def _ln2_epilogue(x, bias, g, b):
    def fn(r, xv, biasv, gv, bv):
        y = _ln2(xv, r, biasv, gv, bv)
        return r, y, y

    return fn, [(x, "tile"), (bias, "row"), (g, "row"), (b, "row")], LN_OUTS


def _relu2_epilogue(bias):
    return (lambda r, b: (r, _relu2(r, b))), [(bias, "row")], [("tile", F32), ("tile", BF16)]


def _relu2_back_epilogue(hpre, bias):
    def fn(r, hp, b):
        _, vjp = jax.vjp(_relu2, hp, b)
        return vjp(r)

    return fn, [(hpre, "tile"), (bias, "row")], [("tile", BF16), ("colsum", F32)]


def _loss_stage(y, target, t):
    def body(tiles, halos, params):
        d = tiles[0] - tiles[1]
        part = 0.5 * jnp.sum(jnp.mean(d * d, axis=-1, keepdims=True), axis=0, keepdims=True)
        return [d * (1.0 / D_MODEL)], [jnp.broadcast_to(part, (1, LANES))]

    (dy,), (loss,) = _tile_call("loss", body, t, ROW_TILE, [(y, D_MODEL, 0), (target, D_MODEL, 0)], [], [],
                                [(D_MODEL, F32)], [(1, LANES)])
    return dy, loss[0, 0]


def _ln2_bwd(x, r, bias, g, b, dy, t):
    def body(tiles, halos, params):
        _, vjp = jax.vjp(_ln2, tiles[0], tiles[1], params[0], params[1], params[2])
        dx, dr, dbias, dg, db = vjp(tiles[2])
        return [dx, dr], [dbias, dg, db]

    return _tile_call("ln2_bwd", body, t, LN_BWD_ROW_TILE, [(x, D_MODEL, 0), (r, D_MODEL, 0), (dy, D_MODEL, 0)], [],
                      [bias, g, b], [(D_MODEL, F32), (D_MODEL, BF16)], [(1, D_MODEL)] * 3)


def _ln1_bwd(x, r, g, b, dy, t):
    def body(tiles, halos, params):
        _, vjp = jax.vjp(_ln1, tiles[0], tiles[1], params[0], params[1])
        dx, dr, dg, db = vjp(tiles[2])
        return [dx, dr], [dg, db]

    return _tile_call("ln1_bwd", body, t, LN_BWD_ROW_TILE, [(x, D_MODEL, 0), (r, D_MODEL, 0), (dy, D_MODEL, 0)], [],
                      [g, b], [(D_MODEL, F32), (D_MODEL, BF16)], [(1, D_MODEL)] * 2)


def _mix_bwd(proj, y_a, y_b, dmixed, t):
    def body(tiles, halos, params):
        _, vjp = jax.vjp(_mix, *tiles[:4])
        dga, dgb, dya, dyb = vjp(tiles[4])
        return [jnp.concatenate([dga, dgb], axis=1), dya, dyb], []

    (dgates, dya, dyb), _ = _tile_call(
        "mix_bwd", body, t, ROW_TILE,
        [(proj, D_MODEL, 7), (proj, D_MODEL, 8), (y_a, D_MODEL, 0), (y_b, D_MODEL, 0), (dmixed, D_MODEL, 0)], [], [],
        [(2 * D_MODEL, BF16), (D_MODEL, BF16), (D_MODEL, BF16)], [])
    return dgates, dya, dyb


def _sc_bwd(proj, conv_w, ds, t):
    def body(tiles, halos, params):
        ds_, sb, sc, sx = tiles
        (dsp, dsn), (sbp, sbn), (scp, scn), (sxp, sxn) = halos
        w0, w1, w2 = _taps(params[0])
        u = sc * sx
        u_prev, u_next = _shift_down(u, scp * sxp), _shift_up(u, scn * sxn)
        dconv = ds_ * sb
        du = w0 * _shift_up(dconv, dsn * sbn) + w1 * dconv + w2 * _shift_down(dconv, dsp * sbp)
        dsb = ds_ * (w0 * u_prev + w1 * u + w2 * u_next)
        dw = _tap_rows(_colsum(dconv * u_prev), _colsum(dconv * u), _colsum(dconv * u_next))
        return [jnp.concatenate([dsb, du * sx, du * sc], axis=1)], [dw]

    (dsc,), (dw,) = _tile_call("sc_bwd", body, t, ROW_TILE, [],
                               [(ds, D_MODEL, 0), (proj, D_MODEL, 4), (proj, D_MODEL, 5), (proj, D_MODEL, 6)],
                               [conv_w], [(3 * D_MODEL, BF16)], [(SUBLANES, D_MODEL)])
    return dsc, dw


def _gate_norm_bwd(o_f, o_b, proj, norm_w, dog, t):
    def body(tiles, halos, params):
        _, vjp = jax.vjp(_gate_norm, tiles[0], tiles[1], tiles[2], params[0])
        do, _, dz, dnw = vjp(tiles[3])
        return [do, dz], [dnw]

    (do, dz), (dnw,) = _tile_call(
        "gate_norm_bwd", body, t, ROW_TILE,
        [(o_f, D_MODEL, 0), (o_b, D_MODEL, 0), (proj, D_MODEL, 3), (dog, D_MODEL, 0)], [], [norm_w],
        [(D_MODEL, F32), (D_MODEL, BF16)], [(1, HEAD_DIM)])
    return do, dz, dnw


def _qkv_conv_bwd(proj, conv_w, c, dq_f, dq_b, t):
    def post_conv_back(cv, ct):
        _, vjp = jax.vjp(_post_conv, cv)
        return vjp(ct)[0]

    def body(tiles, halos, params):
        cv, df, db, x = tiles
        (cp, cn), (dfp, dfn), (dbp, dbn), (xp, xn) = halos
        w0, w1, w2 = _taps(params[0])
        d = post_conv_back(cv, df + db)
        d_prev, d_next = post_conv_back(cp, dfp + dbp), post_conv_back(cn, dfn + dbn)
        dx = w0 * _shift_up(d, d_next) + w1 * d + w2 * _shift_down(d, d_prev)
        dw = _tap_rows(_colsum(d * _shift_down(x, xp)), _colsum(d * x), _colsum(d * _shift_up(x, xn)))
        return [dx], [dw]

    wide = 3 * D_MODEL
    (dqkv,), (dw,) = _tile_call("qkv_conv_bwd", body, t, WIDE_ROW_TILE, [],
                                [(c, wide, 0), (dq_f, wide, 0), (dq_b, wide, 0), (proj, wide, 0)], [conv_w],
                                [(wide, BF16)], [(SUBLANES, wide)])
    return dqkv, dw


def _gating_bwd(proj_ab, a_log, dt_bias, dgb_f, dgb_b, t):
    def body(tiles, halos, params):
        _, vjp = jax.vjp(_gating, tiles[0], params[0], params[1])
        dab, dal, ddt = vjp(tiles[1] + tiles[2])
        return [dab], [dal, ddt]

    (dab,), (dal, ddt) = _tile_call("gating_bwd", body, t, ROW_TILE,
                                    [(proj_ab, LANES, 0), (dgb_f, LANES, 0), (dgb_b, LANES, 0)], [],
                                    [a_log, dt_bias], [(LANES, BF16)], [(1, LANES)] * 2)
    return dab, dal, ddt


@functools.partial(jax.custom_vjp, nondiff_argnums=(2, 3))
def _dot(a, b, ca, cb):
    return lax.dot_general(a.astype(BF16), b.astype(BF16), (((ca,), (cb,)), ((), ())), preferred_element_type=F32)


def _dot_fwd(a, b, ca, cb):
    return _dot(a, b, ca, cb), (a, b)


def _dot_bwd(ca, cb, res, ct):
    a, b = res
    fa, fb = 1 - ca, 1 - cb
    da = _dot(ct, b, 1, fb) if ca == 1 else _dot(b, ct, fb, 1)
    db = _dot(a, ct, fa, 0) if cb == 0 else _dot(ct, a, 0, fa)
    return da, db


_dot.defvjp(_dot_fwd, _dot_bwd)


def _split3(x):
    hi = x.astype(BF16)
    r1 = x - hi.astype(F32)
    mid = r1.astype(BF16)
    return hi, mid, (r1 - mid.astype(F32)).astype(BF16)


def _dot_exact(a, b, ca, cb, exact):
    dims = (((ca,), (cb,)), ((), ()))
    if exact == 0:
        return sum(lax.dot_general(a.astype(BF16), p, dims, preferred_element_type=F32) for p in _split3(b))
    return sum(lax.dot_general(p, b.astype(BF16), dims, preferred_element_type=F32) for p in _split3(a))


def _tri_masks(n, rev):
    r = lax.broadcasted_iota(jnp.int32, (n, n), 0)
    c = lax.broadcasted_iota(jnp.int32, (n, n), 1)
    return ((c >= r), (c > r)) if rev else ((c <= r), (c < r))


@functools.partial(jax.custom_vjp, nondiff_argnums=(1,))
def _cumsum_rows(g, rev):
    incl, _ = _tri_masks(g.shape[0], rev)
    return _dot_exact(incl.astype(F32), g, 1, 0, 0)


_cumsum_rows.defvjp(lambda g, rev: (_cumsum_rows(g, rev), None),
                    lambda rev, _, ct: (_cumsum_rows(ct, not rev),))


def _eye(n):
    return (lax.broadcasted_iota(jnp.int32, (n, n), 0) == lax.broadcasted_iota(jnp.int32, (n, n), 1)).astype(F32)


@jax.custom_vjp
def _to_rows(x):
    return _dot_exact(_eye(x.shape[1]), x, 1, 1, 0)


@jax.custom_vjp
def _to_cols(y):
    return _dot_exact(y, _eye(y.shape[0]), 0, 0, 1)


_to_rows.defvjp(lambda x: (_to_rows(x), None), lambda _, ct: (_to_cols(ct),))
_to_cols.defvjp(lambda y: (_to_cols(y), None), lambda _, ct: (_to_rows(ct),))


def _pick_col(arr, idx):
    return jnp.sum(jnp.where(_lane_iota(arr) == idx, arr, 0.0), axis=1, keepdims=True)


def _pick_row(arr, idx):
    return jnp.sum(jnp.where(_row_iota(arr) == idx, arr, 0.0), axis=0, keepdims=True)


def _chunk_gates(gb, direction, rev):
    n = gb.shape[0]
    incl, strict = _tri_masks(n, rev)
    gc = _cumsum_rows(gb, rev)
    gc_rows = _to_rows(gc)
    lanes = [direction * N_HEADS + h for h in range(N_HEADS)]
    cols = [_pick_col(gc, ln) for ln in lanes]
    rows = [_pick_row(gc_rows, ln) for ln in lanes]
    betas = [_pick_col(gb, 2 * N_HEADS + ln) for ln in lanes]
    decays = [jnp.where(incl, jnp.exp(jnp.where(incl, c - r, 0.0)), 0.0) for c, r in zip(cols, rows)]
    return cols, betas, decays, strict


def _chunk_gates_both(gb_f, gb_b):
    gates = dict(cols=[], betas=[], decays=[], lasts=[])
    for direction, gb in enumerate((gb_f, gb_b)):
        rev = direction == 1
        cols, betas, decays, _ = _chunk_gates(gb, direction, rev)
        last_idx = 0 if rev else gb.shape[0] - 1
        gates["cols"] += cols
        gates["betas"] += betas
        gates["decays"] += decays
        gates["lasts"] += [_pick_row(c, last_idx) for c in cols]
    return gates


def _chunk_lmat(k_f, k_b, gates):
    n = k_f.shape[0]
    stricts = [_tri_masks(n, False)[1]] * N_HEADS + [_tri_masks(n, True)[1]] * N_HEADS
    ks = _heads(k_f) + _heads(k_b)
    kk = [_dot(kh * b, kh, 1, 1) for kh, b in zip(ks, gates["betas"])]
    return tuple(jnp.where(s, x * d, 0.0) for s, x, d in zip(stricts, kk, gates["decays"]))


def _tri_inverse(lmats):
    n = lmats[0].shape[0]
    eye = _eye(n)
    powers = [-lm for lm in lmats]
    invs = [eye + p for p in powers]
    span = 2
    while span < n:
        powers = [_dot(p, p, 1, 0) for p in powers]
        steps = [_dot(p, i, 1, 0) for p, i in zip(powers, invs)]
        invs = [i + s for i, s in zip(invs, steps)]
        span *= 2
    return tuple(invs)


def _chunk_out(qkv_f, qkv_b, gates, tmats, states):
    qs, ks, vs = (_heads(qkv_f[p]) + _heads(qkv_b[p]) for p in range(3))
    cols, betas, decays, lasts = (gates[key] for key in ("cols", "betas", "decays", "lasts"))
    n = ks[0].shape[0]
    vk = [jnp.concatenate([vh * b, kh * b * jnp.exp(c)], axis=1) for vh, kh, b, c in zip(vs, ks, betas, cols)]
    uw = [_dot(tm, x, 1, 0) for tm, x in zip(tmats, vk)]
    attns = [_dot(qh, kh, 1, 1) * d for qh, kh, d in zip(qs, ks, decays)]
    wq = [jnp.concatenate([x[:, HEAD_DIM:], qh * jnp.exp(c)], axis=0) for x, qh, c in zip(uw, qs, cols)]
    wqs = [_dot(x, st, 1, 0) for x, st in zip(wq, states)]
    v_news = [x[:, :HEAD_DIM] - y[:n] for x, y in zip(uw, wqs)]
    inter = [y[n:] for y in wqs]
    intra = [_dot(a, vn, 1, 0) for a, vn in zip(attns, v_news)]
    adds = [_dot(kh * jnp.exp(l - c), vn, 0, 0) for kh, l, c, vn in zip(ks, lasts, cols, v_news)]
    new_states = tuple(st * jnp.exp(l) + a for st, l, a in zip(states, lasts, adds))
    outs = [x + y for x, y in zip(inter, intra)]
    return jnp.concatenate(outs[:N_HEADS], axis=1), jnp.concatenate(outs[N_HEADS:], axis=1), new_states


BOTH = 2 * N_HEADS


def _gdn_specs(n, first_backwards):
    idx = [(lambda i: n - 1 - i) if (d == 0) == first_backwards else (lambda i: i) for d in range(2)]

    def both(shape_of_block, index_tail):
        return [pl.BlockSpec(shape_of_block, lambda i, f=f: (f(i),) + index_tail) for f in idx]

    return idx, both


def _gdn_fwd(qkvn, gb, t):
    n = t // CHUNK
    idx, both = _gdn_specs(n, False)

    def body(qf, kf, vf, qb, kb, vb, gbf, gbb, of_ref, ob_ref, sf_ref, sb_ref, tf_ref, tb_ref, state):
        @pl.when(pl.program_id(0) == 0)
        def _():
            state[...] = jnp.zeros_like(state)

        qkv_f = (qf[...], kf[...], vf[...])
        qkv_b = (qb[...], kb[...], vb[...])
        gates = _chunk_gates_both(gbf[...], gbb[...])
        tmats = _tri_inverse(_chunk_lmat(qkv_f[1], qkv_b[1], gates))
        states = tuple(state[h] for h in range(BOTH))
        o_f, o_b, new_states = _chunk_out(qkv_f, qkv_b, gates, tmats, states)
        of_ref[...] = o_f
        ob_ref[...] = o_b
        for h in range(BOTH):
            s_ref, t_ref = (sf_ref, tf_ref) if h < N_HEADS else (sb_ref, tb_ref)
            s_ref[0, h % N_HEADS] = states[h]
            t_ref[0, h % N_HEADS] = tmats[h]
            state[h] = new_states[h]

    qkv_specs = [pl.BlockSpec((CHUNK, D_MODEL), lambda i, f=f, p=p: (f(i), p)) for f in idx for p in range(3)]
    return pl.pallas_call(
        body, name="gdn_fwd", grid=(n,),
        in_specs=qkv_specs + both((CHUNK, LANES), (0,)),
        out_specs=both((CHUNK, D_MODEL), (0,)) + both((1, N_HEADS, HEAD_DIM, HEAD_DIM), (0, 0, 0))
        + both((1, N_HEADS, CHUNK, CHUNK), (0, 0, 0)),
        out_shape=[jax.ShapeDtypeStruct((t, D_MODEL), F32)] * 2
        + [jax.ShapeDtypeStruct((n, N_HEADS, HEAD_DIM, HEAD_DIM), F32)] * 2
        + [jax.ShapeDtypeStruct((n, N_HEADS, CHUNK, CHUNK), F32)] * 2,
        scratch_shapes=[pltpu.VMEM((BOTH, HEAD_DIM, HEAD_DIM), F32)],
        compiler_params=_params(("arbitrary",)),
    )(*([qkvn] * 6), gb, gb)


def _gdn_bwd(qkvn, gb, s_f, s_b, t_f, t_b, do, t):
    n = t // CHUNK
    idx, both = _gdn_specs(n, True)

    def body(qf, kf, vf, qb, kb, vb, gbf, gbb, sf_ref, sb_ref, tf_ref, tb_ref, dof, dob,
             dqf_ref, dqb_ref, dgf_ref, dgb_ref, dstate):
        @pl.when(pl.program_id(0) == 0)
        def _():
            dstate[...] = jnp.zeros_like(dstate)

        qkv_f = (qf[...], kf[...], vf[...])
        qkv_b = (qb[...], kb[...], vb[...])
        tmats = tuple((tf_ref if h < N_HEADS else tb_ref)[0, h % N_HEADS] for h in range(BOTH))
        states = tuple((sf_ref if h < N_HEADS else sb_ref)[0, h % N_HEADS] for h in range(BOTH))
        gates, gates_vjp = jax.vjp(_chunk_gates_both, gbf[...], gbb[...])
        _, out_vjp = jax.vjp(_chunk_out, qkv_f, qkv_b, gates, tmats, states)
        d_f, d_b, dgates, dtm, dst = out_vjp((dof[...], dob[...], tuple(dstate[h] for h in range(BOTH))))
        firsts = [_dot(tm, d, 0, 0) for tm, d in zip(tmats, dtm)]
        dlm = tuple(-_dot(x, tm, 1, 1) for x, tm in zip(firsts, tmats))
        _, lmat_vjp = jax.vjp(_chunk_lmat, qkv_f[1], qkv_b[1], gates)
        dk_f, dk_b, dgates_lmat = lmat_vjp(dlm)
        dg_f, dg_b = gates_vjp(jax.tree.map(jnp.add, dgates, dgates_lmat))
        dqf_ref[...] = jnp.concatenate([d_f[0], d_f[1] + dk_f, d_f[2]], axis=1)
        dqb_ref[...] = jnp.concatenate([d_b[0], d_b[1] + dk_b, d_b[2]], axis=1)
        dgf_ref[...] = dg_f
        dgb_ref[...] = dg_b
        for h in range(BOTH):
            dstate[h] = dst[h]

    qkv_specs = [pl.BlockSpec((CHUNK, D_MODEL), lambda i, f=f, p=p: (f(i), p)) for f in idx for p in range(3)]
    return pl.pallas_call(
        body, name="gdn_bwd", grid=(n,),
        in_specs=qkv_specs + both((CHUNK, LANES), (0,)) + both((1, N_HEADS, HEAD_DIM, HEAD_DIM), (0, 0, 0))
        + both((1, N_HEADS, CHUNK, CHUNK), (0, 0, 0)) + both((CHUNK, D_MODEL), (0,)),
        out_specs=both((CHUNK, 3 * D_MODEL), (0,)) + both((CHUNK, LANES), (0,)),
        out_shape=[jax.ShapeDtypeStruct((t, 3 * D_MODEL), F32)] * 2 + [jax.ShapeDtypeStruct((t, LANES), F32)] * 2,
        scratch_shapes=[pltpu.VMEM((BOTH, HEAD_DIM, HEAD_DIM), F32)],
        compiler_params=_params(("arbitrary",)),
    )(*([qkvn] * 6), gb, gb, s_f, s_b, t_f, t_b, do, do)


def _mesh_pos():
    return lax.axis_index("x"), lax.axis_index("y"), lax.axis_index("c")


def _exchange_sems(na):
    return [pltpu.SemaphoreType.DMA((na, N_DEV - 1)), pltpu.SemaphoreType.DMA((na, N_DEV - 1)),
            pltpu.SemaphoreType.DMA((na,))]


def _gather_plan(x_refs, out_refs, send_sems, recv_sems, local_sems):
    na = len(x_refs)
    x, y, c = _mesh_pos()
    me, sibling = (x, y, c), (x, y, 1 - c)
    chips = [(1 - x, y), (x, 1 - y), (1 - x, 1 - y)]

    def block(a, px, py, pc):
        return out_refs[a].at[4 * px + 2 * py + pc]

    def copy(a, k, blk, to, src=None):
        return pltpu.make_async_remote_copy(
            src_ref=block(a, *blk) if src is None else src, dst_ref=block(a, *blk),
            send_sem=send_sems.at[a, k], recv_sem=recv_sems.at[a, k],
            device_id=to, device_id_type=pl.DeviceIdType.MESH)

    def mine(a):
        return pltpu.make_async_copy(x_refs[a], block(a, *me), local_sems.at[a])

    def first(a):
        return [copy(a, 0, me, sibling, src=x_refs[a])] + [
            copy(a, 1 + j, me, (*chip, c), src=x_refs[a]) for j, chip in enumerate(chips)]

    def passed(a, j):
        return copy(a, 4 + j, (*chips[j], c), sibling)

    def start():
        for a in range(na):
            mine(a).start()
            for cp in first(a):
                cp.start()

    def forward():
        for j, chip in enumerate(chips):
            for a in range(na):
                copy(a, 1 + j, (*chip, c), me).wait_recv()
                passed(a, j).start()

    def finish():
        for a in range(na):
            copy(a, 0, sibling, me).wait_recv()
            for j, chip in enumerate(chips):
                copy(a, 4 + j, (*chip, 1 - c), me).wait_recv()
        for a in range(na):
            for cp in first(a) + [passed(a, j) for j in range(len(chips))]:
                cp.wait_send()
            mine(a).wait()

    return start, forward, finish


def _scatter_plan(g_refs, land_refs, send_sems, recv_sems, local_sems):
    na = len(g_refs)
    x, y, c = _mesh_pos()
    mine = 4 * x + 2 * y + c

    def local(a):
        return pltpu.make_async_copy(g_refs[a].at[mine], land_refs[a].at[mine], local_sems.at[a])

    def peers():
        for k in range(1, N_DEV):
            px = 1 - x if k & 4 else x
            py = 1 - y if k & 2 else y
            pc = 1 - c if k & 1 else c
            yield k, (px, py, pc), 4 * px + 2 * py + pc

    def send(a, k, to, peer):
        return pltpu.make_async_remote_copy(
            src_ref=g_refs[a].at[peer], dst_ref=land_refs[a].at[mine],
            send_sem=send_sems.at[a, k - 1], recv_sem=recv_sems.at[a, k - 1],
            device_id=to, device_id_type=pl.DeviceIdType.MESH)

    def recv(a, k, peer):
        return pltpu.make_async_remote_copy(
            src_ref=g_refs[a].at[mine], dst_ref=land_refs[a].at[peer],
            send_sem=send_sems.at[a, k - 1], recv_sem=recv_sems.at[a, k - 1],
            device_id=(x, y, c), device_id_type=pl.DeviceIdType.MESH)

    def start():
        for a in range(na):
            local(a).start()
        for k, to, peer in peers():
            for a in range(na):
                send(a, k, to, peer).start()

    def finish():
        for k, to, peer in peers():
            for a in range(na):
                recv(a, k, peer).wait_recv()
        for k, to, peer in peers():
            for a in range(na):
                send(a, k, to, peer).wait_send()
        for a in range(na):
            local(a).wait()

    return start, (lambda: None), finish


def _exchange_call(name, plan, arrays, out_shapes):
    na = len(arrays)

    def body(*refs):
        start, forward, finish = plan(refs[:na], refs[na:2 * na], *refs[2 * na:])
        start()
        forward()
        finish()

    any_spec = pl.BlockSpec(memory_space=pl.ANY)
    return pl.pallas_call(body, name=name, out_shape=list(out_shapes), in_specs=[any_spec] * na,
                          out_specs=[any_spec] * na, scratch_shapes=_exchange_sems(na))(*arrays)


def _gathered_shapes(shards):
    return [jax.ShapeDtypeStruct((N_DEV,) + s.shape, s.dtype) for s in shards]


def _landed_shapes(blocks):
    return [jax.ShapeDtypeStruct(b.shape, b.dtype) for b in blocks]


def _sum_slots(land, name):
    _, rows, cols = land.shape
    tr = _row_tile(rows, cols * 4 * N_DEV, budget=4 << 20)

    def body(*refs):
        acc = refs[0][0].astype(F32)
        for ref in refs[1:N_DEV]:
            acc = acc + ref[0].astype(F32)
        refs[N_DEV][...] = acc

    return pl.pallas_call(
        body, name=name, grid=(rows // tr,),
        in_specs=[pl.BlockSpec((1, tr, cols), lambda i, s=s: (s, i, 0)) for s in range(N_DEV)],
        out_specs=pl.BlockSpec((tr, cols), lambda i: (i, 0)),
        out_shape=jax.ShapeDtypeStruct((rows, cols), F32),
        compiler_params=_params(("parallel",)),
    )(*([land] * N_DEV))


def _row_tile(rows, row_bytes, budget=1 << 20):
    if rows * row_bytes <= budget or rows % SUBLANES:
        return rows
    best = SUBLANES
    for tr in range(SUBLANES, rows + 1, SUBLANES):
        if rows % tr == 0 and tr * row_bytes <= budget:
            best = tr
    return best


def _adamw(w, g, m, v, name):
    shape = w.shape
    cols = shape[-1]
    rows = w.size // cols
    tr = _row_tile(rows, cols * 4)
    b1, b2 = ADAM["b1"], ADAM["b2"]

    def body(w_ref, g_ref, m_ref, v_ref, d_ref, nm_ref, nv_ref):
        gv = g_ref[...]
        nm = b1 * m_ref[...] + (1.0 - b1) * gv
        nv = b2 * v_ref[...] + (1.0 - b2) * jnp.square(gv)
        m_hat = nm / (1.0 - b1 ** ADAM["step"])
        v_hat = nv / (1.0 - b2 ** ADAM["step"])
        d_ref[...] = -ADAM["lr"] * (m_hat / (jnp.sqrt(v_hat) + ADAM["eps"]) + ADAM["wd"] * w_ref[...])
        nm_ref[...] = nm
        nv_ref[...] = nv

    spec = pl.BlockSpec((tr, cols), lambda i: (i, 0))
    outs = pl.pallas_call(
        body, name=name, grid=(rows // tr,), in_specs=[spec] * 4, out_specs=[spec] * 3,
        out_shape=[jax.ShapeDtypeStruct((rows, cols), F32)] * 3, compiler_params=_params(("parallel",)),
    )(*[a.reshape(rows, cols) for a in (w, g, m, v)])
    return [o.reshape(shape) for o in outs]


MATRICES = ("w_in", "w_o_gdn", "w_o_sc", "w_out", "w_up", "w_down")
CONVS = ("conv_qkv", "conv_sc")
SHARDED = ("w_in", "conv_qkv", "w_o_gdn", "conv_sc", "w_o_sc", "w_out", "w_up", "w_down")
SMALL = ("a_log", "dt_bias", "gdn_norm_w", "ln1_g", "ln1_b", "b_up", "b_down", "ln2_g", "ln2_b")
COLUMN_SHARDED = ("w_in", "conv_qkv", "conv_sc", "w_up")
PACK_COLS = 1024


def _pack(parts, row_multiple):
    flat = jnp.concatenate(parts, axis=-1)
    unit = PACK_COLS * row_multiple
    pad = -flat.shape[-1] % unit
    flat = jnp.pad(flat, [(0, 0)] * (flat.ndim - 1) + [(0, pad)])
    return flat.reshape(flat.shape[:-1] + (flat.shape[-1] // PACK_COLS, PACK_COLS))


def _unshard(name, blocks):
    _, l, r, c = blocks.shape
    if name in COLUMN_SHARDED:
        return blocks.transpose(1, 2, 0, 3).reshape(l, r, N_DEV * c)
    return blocks.transpose(1, 0, 2, 3).reshape(l, N_DEV * r, c)


def _to_shards(name, full):
    l, r, c = full.shape
    if name in COLUMN_SHARDED:
        return full.reshape(l, r, N_DEV, c // N_DEV).transpose(2, 0, 1, 3).reshape(N_DEV, -1)
    return full.reshape(l, N_DEV, r // N_DEV, c).transpose(1, 0, 2, 3).reshape(N_DEV, -1)


def _pack_weights(shards):
    parts, layout = [], []
    for name in MATRICES[1:]:
        parts.append(shards[name].astype(BF16).reshape(-1))
        layout.append((name, shards[name].shape, 1))
    for name in CONVS:
        parts.append(jnp.stack(_split3(shards[name])).reshape(-1))
        layout.append((name, shards[name].shape, 3))
    w_in = shards["w_in"]
    return [w_in.astype(BF16).reshape(-1, w_in.shape[-1]), _pack(parts, 16)], (layout, w_in.shape)


def _unpack_weights(gathered, meta):
    (w_in_all, rest_all), (layout, w_in_shape) = gathered, meta
    full = {"w_in": _unshard("w_in", w_in_all.reshape(N_DEV, *w_in_shape))}
    rest_all, off = rest_all.reshape(N_DEV, -1), 0
    for name, shape, pieces in layout:
        size = pieces * shape[0] * shape[1] * shape[2]
        blk = rest_all[:, off:off + size]
        off += size
        if pieces == 3:
            blk = jnp.sum(blk.reshape(N_DEV, 3, *shape).astype(F32), axis=1)
        full[name] = _unshard(name, blk.reshape(N_DEV, *shape))
    return full


def _unpack(flat, names, shapes):
    out, off = {}, 0
    for name in names:
        size = 1
        for s in shapes[name]:
            size *= s
        out[name] = flat[off:off + size].reshape(shapes[name])
        off += size
    return out


def _pack_w_in_grad(dw_in):
    r, c = dw_in.shape
    return dw_in.astype(BF16).reshape(r, N_DEV, c // N_DEV).transpose(1, 0, 2)


def _pack_rest_grads(grads):
    return _pack([_to_shards(name, grads[name].astype(BF16)[None]) for name in SHARDED[1:]], LANES)


def _pack_small_grads(small_grads):
    small = jnp.concatenate([small_grads[name].reshape(-1) for name in SMALL])
    return _pack([jnp.broadcast_to(small[None, :], (N_DEV, small.shape[0]))], SUBLANES)


def _sum_rest_grads(landed_rest, layer_shapes):
    shapes = {name: (1,) + layer_shapes[name] for name in SHARDED[1:]}
    rest = _unpack(_sum_slots(landed_rest, "grads_sum_rest").reshape(-1), SHARDED[1:], shapes)
    return {name: val[0] for name, val in rest.items()}


def _lane_row(values):
    flat = values.reshape(1, -1)
    return jnp.pad(flat, ((0, 0), (0, LANES - flat.shape[1])))


def _forward_layer(x, x16, w, t, next_arrays):
    if next_arrays is None:
        proj, gathered = _mm(x16, w["w_main"], "nn", "proj_main"), None
    else:
        proj, gathered = _mm(x16, w["w_main"], "nn", "proj_main_gather",
                             exchange=(_gather_plan, next_arrays, _gathered_shapes(next_arrays)))
    proj_ab = _mm(x16, w["w_ab"], "nn", "proj_ab")
    conv_out, qkvn = _qkv_conv_fwd(proj, w["conv_qkv"], t)
    gb = _gating_fwd(proj_ab, w["a_log"], w["dt_bias"], t)
    o_f, o_b, s_f, s_b, t_f, t_b = _gdn_fwd(qkvn, gb, t)
    og = _gate_norm_fwd(o_f, o_b, proj, w["gdn_norm_w"], t)
    s = _sc_fwd(proj, w["conv_sc"], t)
    y_a = _mm(og, w["w_o_gdn"], "nn", "y_gdn")
    y_b = _mm(s, w["w_o_sc"], "nn", "y_sc")
    mixed = _mix_fwd(proj, y_a, y_b, t)
    assert D_MODEL <= MM_TILE
    r1, x1, x1_16 = _mm(mixed, w["w_out"], "nn", "out_proj", epilogue=_ln1_epilogue(x, w["ln1_g"], w["ln1_b"]))
    hpre, h = _mm(x1_16, w["w_up"], "nn", "mlp_up", epilogue=_relu2_epilogue(w["b_up"]))
    r2, x2, x2_16 = _mm(h, w["w_down"], "nn", "mlp_down",
                        epilogue=_ln2_epilogue(x1, w["b_down"], w["ln2_g"], w["ln2_b"]))
    saved = dict(x=x, x16=x16, proj=proj, proj_ab=proj_ab, conv_out=conv_out, qkvn=qkvn, gb=gb, o_f=o_f, o_b=o_b,
                 s_f=s_f, s_b=s_b, t_f=t_f, t_b=t_b, og=og, s=s, y_a=y_a, y_b=y_b, mixed=mixed, r1=r1, x1=x1,
                 x1_16=x1_16, hpre=hpre, h=h, r2=r2)
    return x2, x2_16, saved, gathered


def _backward_layer(dx2, w, a, t, pending_w_in):
    (dx1_a, dr2), (db_down, dg2, db2) = _ln2_bwd(a["x1"], a["r2"], w["b_down"], w["ln2_g"], w["ln2_b"], dx2, t)
    relu2_back = _relu2_back_epilogue(a["hpre"], w["b_up"])
    if pending_w_in is None:
        (dhpre, db_up_rows), landed_w_in = _mm(dr2, w["w_down"], "nt", "d_h", epilogue=relu2_back), None
    else:
        (dhpre, db_up_rows), (landed_w_in,) = _mm(
            dr2, w["w_down"], "nt", "d_h_scatter", epilogue=relu2_back,
            exchange=(_scatter_plan, [pending_w_in], _landed_shapes([pending_w_in])))
    dx1 = _mm(dhpre, w["w_up"], "nt", "d_x1", addends=[(1.0, dx1_a)])
    db_up = jnp.sum(db_up_rows, axis=0, keepdims=True)
    dw_down = _mm(a["h"], dr2, "tn", "dw_down")
    dw_up = _mm(a["x1_16"], dhpre, "tn", "dw_up")
    (dx_a, dr1), (dg1, db1) = _ln1_bwd(a["x"], a["r1"], w["ln1_g"], w["ln1_b"], dx1, t)
    dmixed = _mm(dr1, w["w_out"], "nt", "d_mixed", out_dtype=BF16)
    dw_out = _mm(a["mixed"], dr1, "tn", "dw_out")
    dgates, dy_a, dy_b = _mix_bwd(a["proj"], a["y_a"], a["y_b"], dmixed, t)
    dog = _mm(dy_a, w["w_o_gdn"], "nt", "d_og", out_dtype=BF16)
    dw_o_gdn = _mm(a["og"], dy_a, "tn", "dw_o_gdn")
    ds = _mm(dy_b, w["w_o_sc"], "nt", "d_s")
    dw_o_sc = _mm(a["s"], dy_b, "tn", "dw_o_sc")
    dsc, dconv_sc = _sc_bwd(a["proj"], w["conv_sc"], ds, t)
    do, dz, dnorm_w = _gate_norm_bwd(a["o_f"], a["o_b"], a["proj"], w["gdn_norm_w"], dog, t)
    dq_f, dq_b, dgb_f, dgb_b = _gdn_bwd(a["qkvn"], a["gb"], a["s_f"], a["s_b"], a["t_f"], a["t_b"], do, t)
    dqkv, dconv_qkv = _qkv_conv_bwd(a["proj"], w["conv_qkv"], a["conv_out"], dq_f, dq_b, t)
    dab, da_log, ddt_bias = _gating_bwd(a["proj_ab"], w["a_log"], w["dt_bias"], dgb_f, dgb_b, t)

    grads = dict(conv_qkv=dconv_qkv[:3], a_log=da_log[0, :2 * N_HEADS].reshape(2, N_HEADS),
                 dt_bias=ddt_bias[0, :2 * N_HEADS].reshape(2, N_HEADS), gdn_norm_w=dnorm_w[0], w_o_gdn=dw_o_gdn,
                 conv_sc=dconv_sc[:3], w_o_sc=dw_o_sc, w_out=dw_out, ln1_g=dg1[0], ln1_b=db1[0], w_up=dw_up,
                 b_up=db_up[0], w_down=dw_down, b_down=db_down[0], ln2_g=dg2[0], ln2_b=db2[0])
    rest_blocks = [_pack_rest_grads(grads)]
    pieces = [dqkv, dz, dsc, dgates]
    dx = _mm(dab, w["w_ab"], "nt", "dx_ab", addends=[(1.0, dx_a)])
    dx, (landed_rest,) = _mm_nt_pieces(pieces, w["w_main"], "dx_main", dx,
                                       (_scatter_plan, rest_blocks, _landed_shapes(rest_blocks)))
    dw_main = jnp.concatenate([_mm(a["x16"], piece, "tn", f"dw_in_{j}") for j, piece in enumerate(pieces)], axis=1)
    dw_ab = _mm(a["x16"], dab, "tn", "dw_ab")
    grads["w_in"] = jnp.concatenate([dw_main[:, :QKVZ_COLS], dw_ab[:, :AB_COLS], dw_main[:, QKVZ_COLS:]], axis=1)
    return dx, grads, landed_w_in, landed_rest


def kernel(x, w_in, conv_qkv, a_log, dt_bias, gdn_norm_w, w_o_gdn, conv_sc, w_o_sc, w_out, ln1_g, ln1_b, w_up, b_up, w_down, b_down, ln2_g, ln2_b, loss_target, m_w_in, m_conv_qkv, m_a_log, m_dt_bias, m_gdn_norm_w, m_w_o_gdn, m_conv_sc, m_w_o_sc, m_w_out, m_ln1_g, m_ln1_b, m_w_up, m_b_up, m_w_down, m_b_down, m_ln2_g, m_ln2_b, v_w_in, v_conv_qkv, v_a_log, v_dt_bias, v_gdn_norm_w, v_w_o_gdn, v_conv_sc, v_w_o_sc, v_w_out, v_ln1_g, v_ln1_b, v_w_up, v_b_up, v_w_down, v_b_down, v_ln2_g, v_ln2_b):
    weights = dict(w_in=w_in, conv_qkv=conv_qkv, a_log=a_log, dt_bias=dt_bias, gdn_norm_w=gdn_norm_w,
                   w_o_gdn=w_o_gdn, conv_sc=conv_sc, w_o_sc=w_o_sc, w_out=w_out, ln1_g=ln1_g, ln1_b=ln1_b,
                   w_up=w_up, b_up=b_up, w_down=w_down, b_down=b_down, ln2_g=ln2_g, ln2_b=ln2_b)
    m_in = dict(w_in=m_w_in, conv_qkv=m_conv_qkv, a_log=m_a_log, dt_bias=m_dt_bias, gdn_norm_w=m_gdn_norm_w,
                w_o_gdn=m_w_o_gdn, conv_sc=m_conv_sc, w_o_sc=m_w_o_sc, w_out=m_w_out, ln1_g=m_ln1_g, ln1_b=m_ln1_b,
                w_up=m_w_up, b_up=m_b_up, w_down=m_w_down, b_down=m_b_down, ln2_g=m_ln2_g, ln2_b=m_ln2_b)
    v_in = dict(w_in=v_w_in, conv_qkv=v_conv_qkv, a_log=v_a_log, dt_bias=v_dt_bias, gdn_norm_w=v_gdn_norm_w,
                w_o_gdn=v_w_o_gdn, conv_sc=v_conv_sc, w_o_sc=v_w_o_sc, w_out=v_w_out, ln1_g=v_ln1_g, ln1_b=v_ln1_b,
                w_up=v_w_up, b_up=v_b_up, w_down=v_w_down, b_down=v_b_down, ln2_g=v_ln2_g, ln2_b=v_ln2_b)
    t = x.shape[1]
    depth = w_in.shape[0]

    def layer_weights(full, l):
        w_in_l = full["w_in"][0]
        return dict(
            w_main=jnp.concatenate([w_in_l[:, :QKVZ_COLS], w_in_l[:, QKVZ_COLS + AB_COLS:]], axis=1),
            w_ab=jnp.pad(w_in_l[:, QKVZ_COLS:QKVZ_COLS + AB_COLS], ((0, 0), (0, LANES - AB_COLS))),
            conv_qkv=jnp.pad(full["conv_qkv"][0], ((0, SUBLANES - 3), (0, 0))),
            conv_sc=jnp.pad(full["conv_sc"][0], ((0, SUBLANES - 3), (0, 0))),
            a_log=_lane_row(a_log[l]), dt_bias=_lane_row(dt_bias[l]), gdn_norm_w=gdn_norm_w[l][None, :],
            w_o_gdn=full["w_o_gdn"][0], w_o_sc=full["w_o_sc"][0], w_out=full["w_out"][0],
            ln1_g=ln1_g[l][None, :], ln1_b=ln1_b[l][None, :], w_up=full["w_up"][0], b_up=b_up[l][None, :],
            w_down=full["w_down"][0], b_down=b_down[l][None, :], ln2_g=ln2_g[l][None, :], ln2_b=ln2_b[l][None, :])

    packed = [_pack_weights({name: weights[name][l:l + 1] for name in MATRICES + CONVS}) for l in range(depth)]
    gathered = _exchange_call("weights_all_gather", _gather_plan, packed[0][0], _gathered_shapes(packed[0][0]))
    h = x.reshape(t, D_MODEL)
    h16 = h.astype(BF16)
    layers, saved = [], []
    for l in range(depth):
        layers.append(layer_weights(_unpack_weights(gathered, packed[l][1]), l))
        next_arrays = packed[l + 1][0] if l + 1 < depth else None
        h, h16, acts, gathered = _forward_layer(h, h16, layers[l], t, next_arrays)
        saved.append(acts)
    dh, loss_local = _loss_stage(h, loss_target.reshape(t, D_MODEL), t)
    loss = lax.psum(loss_local, MESH_AXES)

    layer_shapes = {name: weights[name].shape[1:] for name in SHARDED}
    layer_grads, reduced, pending_w_in = [None] * depth, [None] * depth, None
    for l in reversed(range(depth)):
        dh, layer_grads[l], landed_w_in, landed_rest = _backward_layer(dh, layers[l], saved[l], t, pending_w_in)
        reduced[l] = _sum_rest_grads(landed_rest, layer_shapes)
        if landed_w_in is not None:
            reduced[l + 1]["w_in"] = _sum_slots(landed_w_in, "grads_sum_w_in")
        pending_w_in = _pack_w_in_grad(layer_grads[l]["w_in"])
    small = _pack_small_grads({name: jnp.stack([g[name] for g in layer_grads]) for name in SMALL})
    last = [pending_w_in, small]
    landed = _exchange_call("grads_scatter", _scatter_plan, last, _landed_shapes(last))
    reduced[0]["w_in"] = _sum_slots(landed[0], "grads_sum_w_in")
    grads = {name: jnp.stack([r[name] for r in reduced]) for name in SHARDED}
    small_shapes = {name: weights[name].shape for name in SMALL}
    grads.update(_unpack(_sum_slots(landed[1], "grads_sum_small").reshape(-1), SMALL, small_shapes))

    names = list(weights)
    updates = {n: _adamw(weights[n], grads[n], m_in[n], v_in[n], f"adamw_{n}") for n in names}
    return (loss, dh.reshape(x.shape), *[grads[n] for n in names], *[updates[n][0] for n in names],
            *[updates[n][1] for n in names], *[updates[n][2] for n in names])
```
